```python
import math
import jax, jax.numpy as jnp
from jax import lax
import numpy as np

D_MODEL = 2048
BATCH = 1
SEQ = 16384
DEPTH = 1

CONV_CH = D_MODEL
CONV_K = 3
N_HEADS = 16
N_KV_HEADS = 4
GROUP = N_HEADS // N_KV_HEADS
HEAD_DIM = 128
CMP_LEN = 32
CMP_STRIDE = 16
CMP_HIDDEN = 2 * HEAD_DIM
SLC_LEN = 64
SLC_TOPN = 16
WINDOW = 512
Q_BLOCK = 128
D_FF = 4 * D_MODEL
PLE_DIM = 256
REL_BUCKETS = 32
REL_EXACT = REL_BUCKETS // 2
REL_MAX_DIST = 4096
LN_EPS = 1e-5
NEG_INF = -1e30
FORCE_SCORE = 1e9
DN_ALPHA = (2 * DEPTH) ** 0.25
DN_BETA = (8 * DEPTH) ** -0.25

kernel_name = "hybrid_conv_nsa_gated_block"


def _in_sizes():
    kv = N_KV_HEADS * HEAD_DIM
    return [CONV_CH, CONV_CH, CONV_CH, N_HEADS * HEAD_DIM, kv, kv, kv, kv, kv, kv, N_HEADS * 3, D_MODEL, D_MODEL]


def _split_in(proj):
    offsets = [int(o) for o in np.cumsum(_in_sizes())[:-1]]
    return jnp.split(proj, offsets, axis=-1)


def _layer_norm(x, g, b):
    xf = x.astype(jnp.float32)
    mu = jnp.mean(xf, axis=-1, keepdims=True)
    var = jnp.mean(jnp.square(xf - mu), axis=-1, keepdims=True)
    y = (xf - mu) * lax.rsqrt(var + LN_EPS) * g.astype(jnp.float32) + b.astype(jnp.float32)
    return y.astype(x.dtype)


def _rel_bucket(dist):
    n = jnp.maximum(dist, 0)
    nf = jnp.maximum(n, 1).astype(jnp.float32)
    large = REL_EXACT + (jnp.log(nf / REL_EXACT) / math.log(REL_MAX_DIST / REL_EXACT)
                         * (REL_BUCKETS - REL_EXACT)).astype(jnp.int32)
    large = jnp.minimum(large, REL_BUCKETS - 1)
    return jnp.where(n < REL_EXACT, n, large)


def _masked_softmax(s, mask):
    p = jax.nn.softmax(jnp.where(mask, s, NEG_INF), axis=-1)
    return jnp.where(mask, p, 0.0)


def _causal_dwconv(z, w):
    C = z.shape[-1]
    return lax.conv_general_dilated(z, w[:, None, :].astype(z.dtype), window_strides=(1,),
                                    padding=[(CONV_K - 1, 0)],
                                    dimension_numbers=('NWC', 'WIO', 'NWC'),
                                    feature_group_count=C)


def _compress(k, pe, w1, w2):
    B, T, Hk, Dh = k.shape
    n_cmp = (T - CMP_LEN) // CMP_STRIDE + 1
    idx = jnp.arange(n_cmp)[:, None] * CMP_STRIDE + jnp.arange(CMP_LEN)[None, :]
    blocks = k[:, idx] + pe[None, None, :, None, :]
    blocks = jnp.moveaxis(blocks, 3, 2).reshape(B, n_cmp, Hk, CMP_LEN * Dh)
    return jax.nn.gelu(blocks @ w1) @ w2


def _gather_rows(kv, pos):
    return jax.vmap(jax.vmap(lambda a, i: a[i]))(kv, pos)


def _nsa(q, k_c, v_c, k_s, v_s, k_w, v_w, gates, pe_k, w1_k, w2_k, pe_v, w1_v, w2_v, rel_bias):
    B, T = q.shape[0], q.shape[1]
    scale = HEAD_DIM ** -0.5
    n_q = T // Q_BLOCK
    n_slc = T // SLC_LEN
    n_sel = min(SLC_TOPN, n_slc)

    kc = _compress(k_c, pe_k, w1_k, w2_k)
    vc = _compress(v_c, pe_v, w1_v, w2_v)
    n_cmp = kc.shape[1]
    cmp_start = jnp.arange(n_cmp) * CMP_STRIDE
    cmp_end = cmp_start + CMP_LEN - 1
    sblk = jnp.arange(n_slc)
    overlap = ((cmp_start[:, None] < (sblk[None, :] + 1) * SLC_LEN)
               & (cmp_start[:, None] + CMP_LEN > sblk[None, :] * SLC_LEN)).astype(jnp.float32)

    ks_t = jnp.transpose(k_s, (0, 2, 1, 3))
    vs_t = jnp.transpose(v_s, (0, 2, 1, 3))
    kw_pad = jnp.pad(k_w, ((0, 0), (WINDOW, 0), (0, 0), (0, 0)))
    vw_pad = jnp.pad(v_w, ((0, 0), (WINDOW, 0), (0, 0), (0, 0)))
    table = rel_bias.astype(jnp.float32)
    table_hg = table.reshape(REL_BUCKETS, N_KV_HEADS, GROUP)

    def bias_dense(dist):
        bsel = table[_rel_bucket(dist)].reshape(dist.shape + (N_KV_HEADS, GROUP))
        return jnp.transpose(bsel, (2, 3, 0, 1))

    def block_fn(args):
        qb, gb, blk = args
        t = blk * Q_BLOCK + jnp.arange(Q_BLOCK)

        dist_c = t[:, None] - cmp_end[None, :]
        s = jnp.einsum('bqhgd,bchd->bhgqc', qb, kc).astype(jnp.float32) * scale + bias_dense(dist_c)
        p_cmp = _masked_softmax(s, dist_c >= 0)
        o_cmp = jnp.einsum('bhgqc,bchd->bqhgd', p_cmp.astype(vc.dtype), vc)

        imp = jnp.einsum('bhgqc,cs->bhqs', p_cmp, overlap)
        jt = (t // SLC_LEN)[:, None]
        forced = (sblk[None, :] == 0) | (sblk[None, :] == jt) | (sblk[None, :] == jt - 1)
        future = sblk[None, :] * SLC_LEN > t[:, None]
        imp = jnp.where(forced, FORCE_SCORE, jnp.where(future, -1.0, imp))
        _, sel = lax.top_k(imp, n_sel)
        pos = (sel[..., None] * SLC_LEN + jnp.arange(SLC_LEN)).reshape(B, N_KV_HEADS, Q_BLOCK, n_sel * SLC_LEN)

        ks = _gather_rows(ks_t, pos)
        vs = _gather_rows(vs_t, pos)
        dist_s = t[None, None, :, None] - pos
        bias_s = table_hg[_rel_bucket(dist_s), jnp.arange(N_KV_HEADS)[None, :, None, None]]
        bias_s = jnp.moveaxis(bias_s, -1, 2)
        s = jnp.einsum('bqhgd,bhqkd->bhgqk', qb, ks).astype(jnp.float32) * scale + bias_s
        p_s = _masked_softmax(s, (dist_s >= 0)[:, :, None])
        o_slc = jnp.einsum('bhgqk,bhqkd->bqhgd', p_s.astype(vs.dtype), vs)

        start = blk * Q_BLOCK
        kw = lax.dynamic_slice_in_dim(kw_pad, start, Q_BLOCK + WINDOW, axis=1)
        vw = lax.dynamic_slice_in_dim(vw_pad, start, Q_BLOCK + WINDOW, axis=1)
        kpos = start - WINDOW + jnp.arange(Q_BLOCK + WINDOW)
        dist_w = t[:, None] - kpos[None, :]
        mask_w = (dist_w >= 0) & (dist_w < WINDOW) & (kpos >= 0)[None, :]
        s = jnp.einsum('bqhgd,bkhd->bhgqk', qb, kw).astype(jnp.float32) * scale + bias_dense(dist_w)
        p_w = _masked_softmax(s, mask_w)
        o_win = jnp.einsum('bhgqk,bkhd->bqhgd', p_w.astype(vw.dtype), vw)

        return gb[..., 0:1] * o_cmp + gb[..., 1:2] * o_slc + gb[..., 2:3] * o_win

    q_blocks = jnp.moveaxis(q.reshape(B, n_q, Q_BLOCK, N_KV_HEADS, GROUP, HEAD_DIM), 1, 0)
    g_blocks = jnp.moveaxis(gates.reshape(B, n_q, Q_BLOCK, N_KV_HEADS, GROUP, 3), 1, 0)
    out = lax.map(block_fn, (q_blocks, g_blocks, jnp.arange(n_q)))
    return jnp.moveaxis(out, 0, 1).reshape(B, T, N_HEADS * HEAD_DIM)


def setup_inputs(seed: int = 0) -> dict:
    key = jax.random.key(seed)
    ks = jax.random.split(key, 24)
    f32 = jnp.float32
    n_in = sum(_in_sizes())
    kv_flat = CMP_LEN * HEAD_DIM

    def nrm(k, shape, scale):
        return jax.random.normal(k, shape, f32) * scale

    return {
        "x": nrm(ks[0], (BATCH, SEQ, D_MODEL), 1.0),
        "p": nrm(ks[1], (DEPTH, BATCH, SEQ, PLE_DIM), 1.0),
        "w_in": nrm(ks[2], (DEPTH, D_MODEL, n_in), D_MODEL ** -0.5),
        "conv_w": nrm(ks[3], (DEPTH, CONV_K, CONV_CH), CONV_K ** -0.5),
        "cmp_pe_k": nrm(ks[4], (DEPTH, CMP_LEN, HEAD_DIM), 0.1),
        "cmp_w1_k": nrm(ks[5], (DEPTH, kv_flat, CMP_HIDDEN), kv_flat ** -0.5),
        "cmp_w2_k": nrm(ks[6], (DEPTH, CMP_HIDDEN, HEAD_DIM), CMP_HIDDEN ** -0.5),
        "cmp_pe_v": nrm(ks[7], (DEPTH, CMP_LEN, HEAD_DIM), 0.1),
        "cmp_w1_v": nrm(ks[8], (DEPTH, kv_flat, CMP_HIDDEN), kv_flat ** -0.5),
        "cmp_w2_v": nrm(ks[9], (DEPTH, CMP_HIDDEN, HEAD_DIM), CMP_HIDDEN ** -0.5),
        "w_conv_out": nrm(ks[10], (DEPTH, CONV_CH, D_MODEL), CONV_CH ** -0.5),
        "w_attn_out": nrm(ks[11], (DEPTH, N_HEADS * HEAD_DIM, D_MODEL), (N_HEADS * HEAD_DIM) ** -0.5),
        "w_mix_out": nrm(ks[12], (DEPTH, D_MODEL, D_MODEL), DN_BETA * D_MODEL ** -0.5),
        "ln1_g": 1.0 + nrm(ks[13], (DEPTH, D_MODEL), 0.02),
        "ln1_b": nrm(ks[14], (DEPTH, D_MODEL), 0.02),
        "w_mlp_up": nrm(ks[15], (DEPTH, D_MODEL, D_FF), D_MODEL ** -0.5),
        "w_mlp_down": nrm(ks[16], (DEPTH, D_FF, D_MODEL), DN_BETA * D_FF ** -0.5),
        "w_ple": nrm(ks[17], (DEPTH, PLE_DIM, D_MODEL), DN_BETA * PLE_DIM ** -0.5),
        "w_ple_gate": nrm(ks[18], (DEPTH, D_MODEL, D_MODEL), D_MODEL ** -0.5),
        "ln2_g": 1.0 + nrm(ks[19], (DEPTH, D_MODEL), 0.02),
        "ln2_b": nrm(ks[20], (DEPTH, D_MODEL), 0.02),
        "rel_bias": nrm(ks[21], (REL_BUCKETS, N_HEADS), 0.5),
    }


def reference(x, p, w_in, conv_w, cmp_pe_k, cmp_w1_k, cmp_w2_k, cmp_pe_v, cmp_w1_v, cmp_w2_v,
              w_conv_out, w_attn_out, w_mix_out, ln1_g, ln1_b, w_mlp_up, w_mlp_down,
              w_ple, w_ple_gate, ln2_g, ln2_b, rel_bias):
    B, T, _ = x.shape
    for i in range(DEPTH):
        proj = x @ w_in[i]
        bg, cg, hx, q, k_c, v_c, k_s, v_s, k_w, v_w, ng, ma, mb = _split_in(proj)

        y_a = (bg * _causal_dwconv(cg * hx, conv_w[i])) @ w_conv_out[i]

        kv_shape = (B, T, N_KV_HEADS, HEAD_DIM)
        o_nsa = _nsa(q.reshape(B, T, N_KV_HEADS, GROUP, HEAD_DIM),
                     k_c.reshape(kv_shape), v_c.reshape(kv_shape),
                     k_s.reshape(kv_shape), v_s.reshape(kv_shape),
                     k_w.reshape(kv_shape), v_w.reshape(kv_shape),
                     jax.nn.sigmoid(ng).reshape(B, T, N_KV_HEADS, GROUP, 3),
                     cmp_pe_k[i], cmp_w1_k[i], cmp_w2_k[i], cmp_pe_v[i], cmp_w1_v[i], cmp_w2_v[i],
                     rel_bias)
        y_b = o_nsa @ w_attn_out[i]

        mixed = jax.nn.sigmoid(ma) * y_a + jax.nn.sigmoid(mb) * y_b
        x = _layer_norm(DN_ALPHA * x + mixed @ w_mix_out[i], ln1_g[i], ln1_b[i])

        h = jnp.square(jax.nn.relu(x @ w_mlp_up[i])) @ w_mlp_down[i]
        ple = (p[i] @ w_ple[i]) * jax.nn.sigmoid(x @ w_ple_gate[i])
        x = _layer_norm(DN_ALPHA * x + h + ple, ln2_g[i], ln2_b[i])
    return x
```

```python
import functools
import math

import jax
import jax.numpy as jnp
from jax import lax
from jax.experimental import pallas as pl
from jax.experimental.pallas import tpu as pltpu

f32 = jnp.float32
bf16 = jnp.bfloat16
i32 = jnp.int32

D_MODEL = 2048
N_HEADS = 16
N_KV = 4
GROUP = N_HEADS // N_KV
HD = 128
CMP_LEN = 32
CMP_STRIDE = 16
CMP_HIDDEN = 2 * HD
SLC_LEN = 64
SLC_TOPN = 16
WINDOW = 512
QB = 128
D_FF = 4 * D_MODEL
PLE_DIM = 256
REL_BUCKETS = 32
REL_EXACT = REL_BUCKETS // 2
REL_MAX_DIST = 4096
LN_EPS = 1e-5
NEG = -1e30
FORCE_SCORE = 1e9
LOG2E = 1.4426950408889634

LANES = 128
KT = 512
SEL_BIG = 65536.0
SEL_CHUNK = 128
N_TOK_BIAS = 25
N_CMP_BIAS = 40
VMEM_LIMIT = 56 * 1024 * 1024

_C_BG, _C_CG, _C_HX = 0, 16, 32
_C_KC, _C_VC, _C_KS, _C_KW = 48, 52, 56, 60
_C_MA, _C_MB = 64, 80
_N_NAT = 96 * LANES
_R_Q, _R_VS, _R_VW, _R_G = 0, 16, 20, 24
_N_T = 25 * LANES


def _dot(a, b):
    return jnp.dot(a, b, preferred_element_type=f32)


def _dot_nt(a, b):
    return lax.dot_general(a, b, (((1,), (1,)), ((), ())), preferred_element_type=f32)


def _params(sem, vmem=VMEM_LIMIT):
    return pltpu.CompilerParams(dimension_semantics=sem, vmem_limit_bytes=vmem)


def _mm_kernel(x_ref, w_ref, o_ref):
    o_ref[...] = _dot(x_ref[...], w_ref[...]).astype(o_ref.dtype)


def _proj_nat(xb, w, tm, tn):
    m, k = xb.shape
    n = w.shape[1]
    return pl.pallas_call(
        _mm_kernel,
        grid=(m // tm, n // tn),
        in_specs=[pl.BlockSpec((tm, k), lambda i, j: (i, 0)),
                  pl.BlockSpec((k, tn), lambda i, j: (0, j))],
        out_specs=pl.BlockSpec((tm, tn), lambda i, j: (i, j)),
        out_shape=jax.ShapeDtypeStruct((m, n), bf16),
        compiler_params=_params(("parallel", "arbitrary")),
        name="proj_nat",
    )(xb, w)


def _mmt_kernel(w_ref, x_ref, o_ref):
    o_ref[0] = _dot_nt(w_ref[...], x_ref[...]).astype(o_ref.dtype)


def _proj_t(wt, xb, tm, tn):
    n, k = wt.shape
    t = xb.shape[0]
    return pl.pallas_call(
        _mmt_kernel,
        grid=(n // tm, t // tn),
        in_specs=[pl.BlockSpec((tm, k), lambda i, j: (i, 0)),
                  pl.BlockSpec((tn, k), lambda i, j: (j, 0))],
        out_specs=pl.BlockSpec((1, tm, tn), lambda i, j: (j, i, 0)),
        out_shape=jax.ShapeDtypeStruct((t // tn, n, tn), bf16),
        compiler_params=_params(("parallel", "arbitrary")),
        name="proj_t",
    )(wt, xb)


def _ya_kernel(bg_ref, cg_ref, hx_ref, cgh_ref, hxh_ref, cw_ref, w_ref, ma_ref, o_ref, a_ref):
    i = pl.program_id(0)
    tm = bg_ref.shape[0]
    cc = 512
    row = lax.broadcasted_iota(i32, (tm, cc), 0)
    for c in range(D_MODEL // cc):
        sl = slice(c * cc, (c + 1) * cc)
        u = cg_ref[:, sl].astype(f32) * hx_ref[:, sl].astype(f32)
        uh = cgh_ref[:, sl].astype(f32) * hxh_ref[:, sl].astype(f32)
        uh = jnp.where(i > 0, uh, 0.0)
        u1 = jnp.where(row == 0, uh[15:16, :], pltpu.roll(u, 1, 0))
        u2 = jnp.where(row == 0, uh[14:15, :], jnp.where(row == 1, uh[15:16, :], pltpu.roll(u, 2, 0)))
        cw = cw_ref[:, sl]
        conv = cw[0:1, :] * u2 + cw[1:2, :] * u1 + cw[2:3, :] * u
        a_ref[:, sl] = (bg_ref[:, sl].astype(f32) * conv).astype(bf16)
    y = _dot(a_ref[...], w_ref[...])
    o_ref[...] = jax.nn.sigmoid(ma_ref[...].astype(f32)) * y


def _mixer_a(projn, conv_w, w_out, tm):
    t = projn.shape[0]
    hb = tm // 16
    wide = lambda c: pl.BlockSpec((tm, D_MODEL), lambda i, c=c: (i, c))
    halo = lambda c: pl.BlockSpec((16, D_MODEL), lambda i, c=c: (jnp.maximum(i * hb - 1, 0), c))
    return pl.pallas_call(
        _ya_kernel,
        grid=(t // tm,),
        in_specs=[wide(0), wide(1), wide(2), halo(1), halo(2),
                  pl.BlockSpec((3, D_MODEL), lambda i: (0, 0)),
                  pl.BlockSpec((D_MODEL, D_MODEL), lambda i: (0, 0)),
                  wide(_C_MA // 16)],
        out_specs=pl.BlockSpec((tm, D_MODEL), lambda i: (i, 0)),
        out_shape=jax.ShapeDtypeStruct((t, D_MODEL), f32),
        scratch_shapes=[pltpu.VMEM((tm, D_MODEL), bf16)],
        compiler_params=_params(("parallel",)),
        name="mixer_a",
    )(projn, projn, projn, projn, projn, conv_w, w_out, projn)


def _compress_kernel(x_ref, w_ref, pe_ref, w2_ref, w2t_ref, on_ref, ot_ref, acc_ref, pet_ref):
    l = pl.program_id(1)
    nl = pl.num_programs(1)
    n = x_ref.shape[0]

    @pl.when(l == 0)
    def _():
        acc_ref[...] = jnp.zeros_like(acc_ref)
        pet_ref[...] = jnp.zeros_like(pet_ref)

    wblk = w_ref[0]
    for h in range(N_KV):
        acc_ref[h] += _dot(x_ref[:, h * HD:(h + 1) * HD], wblk)
    pet_ref[...] += _dot(pe_ref[0, 0].astype(bf16), wblk)

    @pl.when(l == nl - 1)
    def _():
        r = pet_ref[...]
        peterm = r[0:1, :CMP_HIDDEN] + r[1:2, CMP_HIDDEN:]
        for h in range(N_KV):
            ab = acc_ref[h]
            hid = ab[:, :CMP_HIDDEN] + pltpu.roll(ab[:, CMP_HIDDEN:], n - 1, 0) + peterm
            gl = jax.nn.gelu(hid).astype(bf16)
            on_ref[0, h] = _dot(gl, w2_ref[0]).astype(bf16)
            ot_ref[0, h] = _dot_nt(w2t_ref[0], gl).astype(bf16)


def _compress(projn, pe2, w1ab, w2, w2t):
    t = projn.shape[0]
    n = t // CMP_STRIDE
    xr = projn.reshape(n, CMP_STRIDE * _N_NAT)
    per_l = _N_NAT // (N_KV * HD)
    c0 = _C_KC // N_KV
    return pl.pallas_call(
        _compress_kernel,
        grid=(2, CMP_STRIDE),
        in_specs=[pl.BlockSpec((n, N_KV * HD), lambda kv, l: (0, l * per_l + c0 + kv)),
                  pl.BlockSpec((1, HD, 2 * CMP_HIDDEN), lambda kv, l: (kv, l, 0)),
                  pl.BlockSpec((1, 1, 8, HD), lambda kv, l: (kv, l, 0, 0)),
                  pl.BlockSpec((1, CMP_HIDDEN, HD), lambda kv, l: (kv, 0, 0)),
                  pl.BlockSpec((1, HD, CMP_HIDDEN), lambda kv, l: (kv, 0, 0))],
        out_specs=[pl.BlockSpec((1, N_KV, n, HD), lambda kv, l: (kv, 0, 0, 0)),
                   pl.BlockSpec((1, N_KV, HD, n), lambda kv, l: (kv, 0, 0, 0))],
        out_shape=[jax.ShapeDtypeStruct((2, N_KV, n, HD), bf16),
                   jax.ShapeDtypeStruct((2, N_KV, HD, n), bf16)],
        scratch_shapes=[pltpu.VMEM((N_KV, n, 2 * CMP_HIDDEN), f32),
                        pltpu.VMEM((8, 2 * CMP_HIDDEN), f32)],
        compiler_params=_params(("arbitrary", "arbitrary")),
        name="compress",
    )(xr, w1ab, pe2, w2, w2t)


def _rel_bucket(dist):
    n = jnp.maximum(dist, 0)
    nf = jnp.maximum(n, 1).astype(f32)
    large = REL_EXACT + (jnp.log(nf / REL_EXACT) / math.log(REL_MAX_DIST / REL_EXACT)
                         * (REL_BUCKETS - REL_EXACT)).astype(i32)
    large = jnp.minimum(large, REL_BUCKETS - 1)
    return jnp.where(n < REL_EXACT, n, large)


def _bias_kernel(tbl_ref, o_ref, *, kstride, koff):
    hk = pl.program_id(0)
    o = pl.program_id(1)
    ki = lax.broadcasted_iota(i32, (LANES, LANES), 0)
    qi = lax.broadcasted_iota(i32, (LANES, LANES), 1)
    bucket = _rel_bucket(o * LANES + qi - kstride * ki - koff)
    for g in range(GROUP):
        h = hk * GROUP + g
        acc = jnp.zeros((LANES, LANES), f32)
        for k in range(REL_BUCKETS):
            acc = jnp.where(bucket == k, tbl_ref[k, h], acc)
        o_ref[0, 0, :, g * LANES:(g + 1) * LANES] = acc * LOG2E


def _bias_tiles(rel_bias, n_tiles, kstride, koff, name):
    return pl.pallas_call(
        functools.partial(_bias_kernel, kstride=kstride, koff=koff),
        grid=(N_KV, n_tiles),
        in_specs=[pl.BlockSpec(memory_space=pltpu.SMEM)],
        out_specs=pl.BlockSpec((1, 1, LANES, GROUP * LANES), lambda hk, o: (hk, o, 0, 0)),
        out_shape=jax.ShapeDtypeStruct((N_KV, n_tiles, LANES, GROUP * LANES), f32),
        compiler_params=_params(("parallel", "arbitrary")),
        name=name,
    )(rel_bias)


def _query_t(q_ref):
    qb = q_ref[0]
    return jnp.concatenate([qb[g * HD:(g + 1) * HD, :] for g in range(GROUP)], axis=1)


def _cmp_kernel(q_ref, kc_ref, vct_ref, bc_ref, ov_ref, g_ref, o_ref, ns_ref, *, n_sel):
    b = pl.program_id(1)
    ncp = kc_ref.shape[2]
    nslc = ov_ref.shape[0]
    gl = GROUP * LANES
    qt = _query_t(q_ref)
    s = _dot(kc_ref[0, 0], qt)

    qi = lax.broadcasted_iota(i32, (LANES, gl), 1) & (LANES - 1)
    ci = lax.broadcasted_iota(i32, (LANES, gl), 0)
    parts = []
    for c in range(ncp // LANES):
        o = b - (LANES // (QB // CMP_STRIDE)) * c
        bias = bc_ref[0, jnp.clip(o, 0, N_CMP_BIAS - 1)]
        dist = o * QB + qi - CMP_STRIDE * ci - (CMP_LEN - 1)
        parts.append(jnp.where(dist >= 0, s[c * LANES:(c + 1) * LANES] + bias, NEG))
    s = jnp.concatenate(parts, axis=0) if len(parts) > 1 else parts[0]

    m = jnp.max(s, axis=0, keepdims=True)
    e = jnp.exp2(s - m)
    l = jnp.sum(e, axis=0, keepdims=True)
    p = jnp.where(s > 0.5 * NEG, e / l, 0.0)

    oct_ = _dot(vct_ref[0, 0], p.astype(bf16))

    pg = p[:, 0:LANES] + p[:, LANES:2 * LANES] + p[:, 2 * LANES:3 * LANES] + p[:, 3 * LANES:4 * LANES]
    hi = pg.astype(bf16)
    lo = (pg - hi.astype(f32)).astype(bf16)
    imp = _dot(ov_ref[...], hi) + _dot(ov_ref[...], lo)

    j = lax.broadcasted_iota(i32, (nslc, LANES), 0)
    tq = b * QB + lax.broadcasted_iota(i32, (nslc, LANES), 1)
    jt = jnp.right_shift(tq, SLC_LEN.bit_length() - 1)
    forced = (j == 0) | (j == jt) | (j == jt - 1)
    future = j * SLC_LEN > tq
    work = jnp.where(forced, FORCE_SCORE, jnp.where(future, -1.0, imp))
    jf = j.astype(f32)
    sel = jnp.zeros((nslc, LANES), f32)
    for _ in range(n_sel):
        mx = jnp.max(work, axis=0, keepdims=True)
        jm = jnp.min(jnp.where(work == mx, jf, 1e9), axis=0, keepdims=True)
        pick = jf == jm
        sel = jnp.where(pick, 1.0, sel)
        work = jnp.where(pick, -3e38, work)
    nsel = jnp.where(future, 1.0, 1.0 - sel).astype(bf16)
    nsr = ns_ref.shape[1]
    if nsr > nslc:
        nsel = jnp.concatenate([nsel, jnp.ones((nsr - nslc, LANES), bf16)], axis=0)
    ns_ref[0] = nsel

    gate = jax.nn.sigmoid(g_ref[0].astype(f32))
    for g in range(GROUP):
        sl = slice(g * LANES, (g + 1) * LANES)
        og = oct_[:, sl] * gate[g * 3:g * 3 + 1, :]
        o_ref[:, sl] = og.T.astype(o_ref.dtype)


def _cmp_attention(projt, cn, ct, bias_c, ov_t, t, nsr):
    nb = t // QB
    ncp = t // CMP_STRIDE
    nslc = t // SLC_LEN
    per = KT // QB
    n_sel = min(SLC_TOPN, nslc)
    return pl.pallas_call(
        functools.partial(_cmp_kernel, n_sel=n_sel),
        grid=(N_KV, nb),
        in_specs=[pl.BlockSpec((1, GROUP * HD, QB), lambda hk, b: (b // per, hk, b % per)),
                  pl.BlockSpec((1, 1, ncp, HD), lambda hk, b: (0, hk, 0, 0)),
                  pl.BlockSpec((1, 1, HD, ncp), lambda hk, b: (1, hk, 0, 0)),
                  pl.BlockSpec((1, N_CMP_BIAS, LANES, GROUP * LANES), lambda hk, b: (hk, 0, 0, 0)),
                  pl.BlockSpec((nslc, ncp), lambda hk, b: (0, 0)),
                  pl.BlockSpec((1, 16, QB), lambda hk, b: (b // per, _R_G * 8 + hk, b % per))],
        out_specs=[pl.BlockSpec((QB, GROUP * HD), lambda hk, b: (b, hk)),
                   pl.BlockSpec((1, nsr, QB), lambda hk, b: (hk, 0, b))],
        out_shape=[jax.ShapeDtypeStruct((t, N_HEADS * HD), f32),
                   jax.ShapeDtypeStruct((N_KV, nsr, t), bf16)],
        compiler_params=_params(("parallel", "arbitrary")),
        name="cmp_attention",
    )(projt, cn, ct, bias_c, ov_t, projt)


def _slc_kernel(q_ref, ks_ref, vst_ref, kw0, kw1, kw2, kw3, kw4, vw0, vw1, vw2, vw3, vw4,
                ns_ref, bt_ref, ind_ref, g_ref, o1_ref, o_ref, qa_ref, acc_ref, m_ref, l_ref):
    b = pl.program_id(1)
    gl = GROUP * LANES
    qt = _query_t(q_ref)
    qi = lax.broadcasted_iota(i32, (LANES, gl), 1) & (LANES - 1)
    ki = lax.broadcasted_iota(i32, (LANES, gl), 0)
    rel = qi - ki

    acc_ref[...] = jnp.zeros_like(acc_ref)
    m_ref[...] = jnp.full_like(m_ref, NEG)
    l_ref[...] = jnp.zeros_like(l_ref)

    nk = b // (KT // QB) + 1
    per_chunk = SEL_CHUNK * SLC_LEN // KT
    n_chunks = ns_ref.shape[1] // SEL_CHUNK
    sub = KT // LANES
    for jc in range(n_chunks):
        nsj = -ns_ref[0, jc * SEL_CHUNK:(jc + 1) * SEL_CHUNK, :]
        qa_ref[0:HD, :] = qt
        qa_ref[HD:, :] = jnp.concatenate([nsj] * GROUP, axis=1)
        lo = jc * per_chunk

        def body(kt, carry, lo=lo):
            r0 = pl.multiple_of(kt * KT, KT)
            i0 = pl.multiple_of((kt - lo) * KT, KT)
            kaug = jnp.concatenate([ks_ref[pl.ds(r0, KT), :], ind_ref[pl.ds(i0, KT), :]], axis=1)
            s = _dot(kaug, qa_ref[...])
            parts = []
            for jj in range(sub):
                dl = b - (kt * sub + jj)
                bias = bt_ref[0, jnp.clip(dl, 0, N_TOK_BIAS - 1)]
                parts.append(jnp.where(dl * LANES + rel >= 0, s[jj * LANES:(jj + 1) * LANES] + bias, NEG))
            s = jnp.concatenate(parts, axis=0)
            m_old = m_ref[...]
            m_new = jnp.maximum(m_old, jnp.max(s, axis=0, keepdims=True))
            alpha = jnp.exp2(m_old - m_new)
            p = jnp.exp2(s - m_new)
            l_ref[...] = alpha * l_ref[...] + jnp.sum(p, axis=0, keepdims=True)
            acc_ref[...] = alpha * acc_ref[...] + _dot(vst_ref[kt], p.astype(bf16))
            m_ref[...] = m_new
            return carry

        hi = jnp.maximum(jnp.minimum(nk, lo + per_chunk), lo)
        lax.fori_loop(lo, hi, body, 0)

    kws = (kw0, kw1, kw2, kw3, kw4)
    vws = (vw0, vw1, vw2, vw3, vw4)
    nw = len(kws)
    sw = _dot(jnp.concatenate([r[...] for r in kws], axis=0), qt)
    parts = []
    for jw in range(nw):
        dl = nw - 1 - jw
        dist = dl * LANES + rel
        valid = (dist >= 0) & (dist < WINDOW) & ((b - dl) * LANES + ki >= 0)
        parts.append(jnp.where(valid, sw[jw * LANES:(jw + 1) * LANES] + bt_ref[0, dl], NEG))
    sw = jnp.concatenate(parts, axis=0)
    mw = jnp.max(sw, axis=0, keepdims=True)
    pw = jnp.exp2(sw - mw)
    lw = jnp.sum(pw, axis=0, keepdims=True)
    owt = _dot(jnp.concatenate([r[0] for r in vws], axis=1), pw.astype(bf16)) / lw

    ost = acc_ref[...] / l_ref[...]
    gate = jax.nn.sigmoid(g_ref[0].astype(f32))
    for g in range(GROUP):
        sl = slice(g * LANES, (g + 1) * LANES)
        ot = ost[:, sl] * gate[g * 3 + 1:g * 3 + 2, :] + owt[:, sl] * gate[g * 3 + 2:g * 3 + 3, :]
        o_ref[:, sl] = (ot.T + o1_ref[:, sl]).astype(o_ref.dtype)


def _slc_attention(projn, projt, nsel_t, bias_t, ind, o1, t):
    nb = t // QB
    per = KT // QB
    nsr = nsel_t.shape[1]
    nwin = WINDOW // QB + 1

    def kw_spec(jw):
        return pl.BlockSpec((QB, HD), lambda hk, b, jw=jw: (jnp.maximum(b - (nwin - 1) + jw, 0), _C_KW + hk))

    def vw_spec(jw):
        def imap(hk, b, jw=jw):
            bb = jnp.maximum(b - (nwin - 1) + jw, 0)
            return (bb // per, _R_VW + hk, bb % per)
        return pl.BlockSpec((1, HD, QB), imap)

    return pl.pallas_call(
        _slc_kernel,
        grid=(N_KV, nb),
        in_specs=[pl.BlockSpec((1, GROUP * HD, QB), lambda hk, b: (b // per, hk, b % per)),
                  pl.BlockSpec((t, HD), lambda hk, b: (0, _C_KS + hk)),
                  pl.BlockSpec((t // KT, HD, KT), lambda hk, b: (0, _R_VS + hk, 0))]
                 + [kw_spec(jw) for jw in range(nwin)]
                 + [vw_spec(jw) for jw in range(nwin)]
                 + [pl.BlockSpec((1, nsr, QB), lambda hk, b: (hk, 0, b)),
                    pl.BlockSpec((1, N_TOK_BIAS, LANES, GROUP * LANES), lambda hk, b: (hk, 0, 0, 0)),
                    pl.BlockSpec(ind.shape, lambda hk, b: (0, 0)),
                    pl.BlockSpec((1, 16, QB), lambda hk, b: (b // per, _R_G * 8 + hk, b % per)),
                    pl.BlockSpec((QB, GROUP * HD), lambda hk, b: (b, hk))],
        out_specs=pl.BlockSpec((QB, GROUP * HD), lambda hk, b: (b, hk)),
        out_shape=jax.ShapeDtypeStruct((t, N_HEADS * HD), bf16),
        scratch_shapes=[pltpu.VMEM((2 * HD, GROUP * LANES), bf16),
                        pltpu.VMEM((HD, GROUP * LANES), f32),
                        pltpu.VMEM((1, GROUP * LANES), f32),
                        pltpu.VMEM((1, GROUP * LANES), f32)],
        compiler_params=_params(("parallel", "arbitrary")),
        name="slc_attention",
    )(projt, projn, projt, *([projn] * nwin), *([projt] * nwin), nsel_t, bias_t, ind, projt, o1)


def _layer_norm(z, g, b):
    mu = jnp.mean(z, axis=-1, keepdims=True)
    var = jnp.mean(jnp.square(z - mu), axis=-1, keepdims=True)
    return (z - mu) * lax.rsqrt(var + LN_EPS) * g + b


def _post_kernel(o2_ref, ya_ref, mb_ref, x_ref, p_ref, wao_ref, wmix_ref, wple_ref, wpg_ref,
                 g_ref, b_ref, x1_ref, r2_ref, *, alpha):
    yb = _dot(o2_ref[...], wao_ref[...])
    mixed = ya_ref[...] + jax.nn.sigmoid(mb_ref[...].astype(f32)) * yb
    z = alpha * x_ref[...] + _dot(mixed.astype(bf16), wmix_ref[...])
    x1 = _layer_norm(z, g_ref[...], b_ref[...])
    x1b = x1.astype(bf16)
    x1_ref[...] = x1b
    ple = _dot(p_ref[...].astype(bf16), wple_ref[...]) * jax.nn.sigmoid(_dot(x1b, wpg_ref[...]))
    r2_ref[...] = alpha * x1 + ple


def _post_attention(o2, ya, projn, x2d, p2d, wao, wmix, wple, wpg, g1, b1, alpha, tm):
    t = x2d.shape[0]
    row = lambda w: pl.BlockSpec((tm, w), lambda i: (i, 0))
    full = lambda a: pl.BlockSpec(a.shape, lambda i: (0, 0), pipeline_mode=pl.Buffered(1))
    return pl.pallas_call(
        functools.partial(_post_kernel, alpha=alpha),
        grid=(t // tm,),
        in_specs=[row(D_MODEL), row(D_MODEL),
                  pl.BlockSpec((tm, D_MODEL), lambda i: (i, _C_MB // 16)),
                  row(D_MODEL), row(PLE_DIM),
                  full(wao), full(wmix), full(wple), full(wpg), full(g1), full(b1)],
        out_specs=[row(D_MODEL), row(D_MODEL)],
        out_shape=[jax.ShapeDtypeStruct((t, D_MODEL), bf16),
                   jax.ShapeDtypeStruct((t, D_MODEL), f32)],
        compiler_params=_params(("parallel",)),
        name="post_attention",
    )(o2, ya, projn, x2d, p2d, wao, wmix, wple, wpg, g1, b1)


def _mlp_kernel(x1_ref, r2_ref, wu_ref, wd_ref, g_ref, b_ref, o_ref):
    k = pl.program_id(1)

    @pl.when(k == 0)
    def _():
        o_ref[...] = r2_ref[...]

    h = jnp.square(jnp.maximum(_dot(x1_ref[...], wu_ref[...]), 0.0))
    o_ref[...] += _dot(h.astype(bf16), wd_ref[...])

    @pl.when(k == pl.num_programs(1) - 1)
    def _():
        o_ref[...] = _layer_norm(o_ref[...], g_ref[...], b_ref[...])


def _mlp(x1b, r2, wu, wd, g2, b2, tm, fc):
    t = x1b.shape[0]
    return pl.pallas_call(
        _mlp_kernel,
        grid=(t // tm, D_FF // fc),
        in_specs=[pl.BlockSpec((tm, D_MODEL), lambda i, k: (i, 0)),
                  pl.BlockSpec((tm, D_MODEL), lambda i, k: (i, 0)),
                  pl.BlockSpec((D_MODEL, fc), lambda i, k: (0, k)),
                  pl.BlockSpec((fc, D_MODEL), lambda i, k: (k, 0)),
                  pl.BlockSpec((1, D_MODEL), lambda i, k: (0, 0)),
                  pl.BlockSpec((1, D_MODEL), lambda i, k: (0, 0))],
        out_specs=pl.BlockSpec((tm, D_MODEL), lambda i, k: (i, 0)),
        out_shape=jax.ShapeDtypeStruct((t, D_MODEL), f32),
        compiler_params=_params(("parallel", "arbitrary")),
        name="mlp",
    )(x1b, r2, wu, wd, g2, b2)


def _layer(x2d, p2d, w_in, conv_w, pe_k, w1_k, w2_k, pe_v, w1_v, w2_v, w_conv_out, w_attn_out,
           w_mix_out, ln1_g, ln1_b, w_mlp_up, w_mlp_down, w_ple, w_ple_gate, ln2_g, ln2_b,
           rel_bias, alpha):
    t = x2d.shape[0]
    assert t % 1024 == 0 and x2d.shape[1] == D_MODEL
    kvw = N_KV * HD
    o_q = 3 * D_MODEL
    o_kc = o_q + N_HEADS * HD
    o_ng = o_kc + 6 * kvw
    o_ma = o_ng + N_HEADS * 3

    def cols(a, n):
        return w_in[:, a:a + n]

    w_nat = jnp.concatenate(
        [cols(0, 3 * D_MODEL), cols(o_kc, 2 * kvw), cols(o_kc + 2 * kvw, kvw),
         cols(o_kc + 4 * kvw, kvw), cols(o_ma, 2 * D_MODEL)], axis=1).astype(bf16)
    wng = jnp.pad(cols(o_ng, N_HEADS * 3).reshape(D_MODEL, N_KV, GROUP * 3), ((0, 0), (0, 0), (0, 16 - GROUP * 3)))
    w_t = jnp.concatenate(
        [cols(o_q, N_HEADS * HD) * (HD ** -0.5 * LOG2E), cols(o_kc + 3 * kvw, kvw),
         cols(o_kc + 5 * kvw, kvw), wng.reshape(D_MODEL, N_KV * 16),
         jnp.zeros((D_MODEL, _N_T - _R_G * LANES - N_KV * 16), f32)], axis=1).T.astype(bf16)

    xb = x2d.astype(bf16)
    projn = _proj_nat(xb, w_nat, 1024, 1024)
    projt = _proj_t(w_t, xb, 640, KT)

    ya = _mixer_a(projn, conv_w, w_conv_out.astype(bf16), 512)

    half = CMP_STRIDE * HD
    w1ab = jnp.stack([jnp.concatenate([w[:half], w[half:]], axis=1) for w in (w1_k, w1_v)]).astype(bf16)
    pe2 = jnp.stack([jnp.stack([pe[:CMP_STRIDE], pe[CMP_STRIDE:]], axis=1) for pe in (pe_k, pe_v)])
    pe2 = jnp.pad(pe2, ((0, 0), (0, 0), (0, 6), (0, 0)))
    w2 = jnp.stack([w2_k, w2_v]).astype(bf16)
    cn, ct = _compress(projn, pe2, w1ab, w2, jnp.swapaxes(w2, 1, 2))

    bias_t = _bias_tiles(rel_bias, N_TOK_BIAS, 1, 0, "bias_tok")
    bias_c = _bias_tiles(rel_bias, N_CMP_BIAS, CMP_STRIDE, CMP_LEN - 1, "bias_cmp")

    ncp = t // CMP_STRIDE
    nslc = t // SLC_LEN
    nsr = -(-nslc // SEL_CHUNK) * SEL_CHUNK
    ci = jnp.arange(ncp)[None, :] * CMP_STRIDE
    sj = jnp.arange(nslc)[:, None] * SLC_LEN
    ov_t = ((ci < sj + SLC_LEN) & (ci + CMP_LEN > sj)).astype(bf16)
    pos = jnp.arange(min(t, SEL_CHUNK * SLC_LEN))[:, None] // SLC_LEN
    ind = jnp.where(pos == jnp.arange(SEL_CHUNK)[None, :], SEL_BIG, 0.0).astype(bf16)

    o1, nsel_t = _cmp_attention(projt, cn, ct, bias_c, ov_t, t, nsr)
    o2 = _slc_attention(projn, projt, nsel_t, bias_t, ind, o1, t)

    x1b, r2 = _post_attention(o2, ya, projn, x2d, p2d, w_attn_out.astype(bf16), w_mix_out.astype(bf16),
                              w_ple.astype(bf16), w_ple_gate.astype(bf16),
                              ln1_g[None, :], ln1_b[None, :], alpha, 256)
    return _mlp(x1b, r2, w_mlp_up.astype(bf16), w_mlp_down.astype(bf16), ln2_g[None, :], ln2_b[None, :], 512, 512)


def kernel(x, p, w_in, conv_w, cmp_pe_k, cmp_w1_k, cmp_w2_k, cmp_pe_v, cmp_w1_v, cmp_w2_v, w_conv_out, w_attn_out, w_mix_out, ln1_g, ln1_b, w_mlp_up, w_mlp_down, w_ple, w_ple_gate, ln2_g, ln2_b, rel_bias):
    bsz, t, d = x.shape
    assert bsz == 1
    depth = w_in.shape[0]
    alpha = (2 * depth) ** 0.25
    x2d = x[0]
    for i in range(depth):
        x2d = _layer(x2d, p[i, 0], w_in[i], conv_w[i], cmp_pe_k[i], cmp_w1_k[i], cmp_w2_k[i],
                     cmp_pe_v[i], cmp_w1_v[i], cmp_w2_v[i], w_conv_out[i], w_attn_out[i], w_mix_out[i],
                     ln1_g[i], ln1_b[i], w_mlp_up[i], w_mlp_down[i], w_ple[i], w_ple_gate[i],
                     ln2_g[i], ln2_b[i], rel_bias, alpha)
    return x2d[None]
```

```python
import functools
import math

import jax
import jax.numpy as jnp
from jax import lax
from jax.experimental import pallas as pl
from jax.experimental.pallas import tpu as pltpu

f32 = jnp.float32
bf16 = jnp.bfloat16
i32 = jnp.int32

D_MODEL = 2048
N_HEADS = 16
N_KV = 4
GROUP = N_HEADS // N_KV
HD = 128
CMP_LEN = 32
CMP_STRIDE = 16
CMP_HIDDEN = 2 * HD
SLC_LEN = 64
SLC_TOPN = 16
WINDOW = 512
QB = 128
D_FF = 4 * D_MODEL
PLE_DIM = 256
REL_BUCKETS = 32
REL_EXACT = REL_BUCKETS // 2
REL_MAX_DIST = 4096
LN_EPS = 1e-5
NEG = -1e30
FORCE_SCORE = 1e9
LOG2E = 1.4426950408889634

LANES = 128
KT = 512
SEL_BIG = 65536.0
SEL_CHUNK = 128
N_TOK_BIAS = 25
N_CMP_BIAS = 40
VMEM_LIMIT = 56 * 1024 * 1024

_C_BG, _C_CG, _C_HX = 0, 16, 32
_C_MA, _C_MB = 48, 64
_C_KS, _C_KW = 80, 84
_N_NAT = 88 * LANES
_R_Q, _R_VS, _R_VW, _R_G = 0, 16, 20, 24
_N_T = 25 * LANES


def _dot(a, b):
    return jnp.dot(a, b, preferred_element_type=f32)


def _dot_nt(a, b):
    return lax.dot_general(a, b, (((1,), (1,)), ((), ())), preferred_element_type=f32)


def _params(sem, vmem=VMEM_LIMIT):
    return pltpu.CompilerParams(dimension_semantics=sem, vmem_limit_bytes=vmem)


def _mm_kernel(x_ref, w_ref, o_ref):
    o_ref[...] = _dot(x_ref[...], w_ref[...]).astype(o_ref.dtype)


def _proj_nat(xb, w, tm, tn, name):
    m, k = xb.shape
    n = w.shape[1]
    return pl.pallas_call(
        _mm_kernel,
        grid=(m // tm, n // tn),
        in_specs=[pl.BlockSpec((tm, k), lambda i, j: (i, 0)),
                  pl.BlockSpec((k, tn), lambda i, j: (0, j))],
        out_specs=pl.BlockSpec((tm, tn), lambda i, j: (i, j)),
        out_shape=jax.ShapeDtypeStruct((m, n), bf16),
        compiler_params=_params(("parallel", "arbitrary")),
        name=name,
    )(xb, w)


def _mmt_kernel(w_ref, x_ref, o_ref):
    o_ref[0] = _dot_nt(w_ref[...], x_ref[...]).astype(o_ref.dtype)


def _proj_t(wt, xb, tm, tn):
    n, k = wt.shape
    t = xb.shape[0]
    return pl.pallas_call(
        _mmt_kernel,
        grid=(n // tm, t // tn),
        in_specs=[pl.BlockSpec((tm, k), lambda i, j: (i, 0)),
                  pl.BlockSpec((tn, k), lambda i, j: (j, 0))],
        out_specs=pl.BlockSpec((1, tm, tn), lambda i, j: (j, i, 0)),
        out_shape=jax.ShapeDtypeStruct((t // tn, n, tn), bf16),
        compiler_params=_params(("parallel", "arbitrary")),
        name="proj_t",
    )(wt, xb)


def _ya_kernel(bg_ref, cg_ref, hx_ref, cgh_ref, hxh_ref, cw_ref, w_ref, ma_ref, o_ref, a_ref):
    i = pl.program_id(0)
    tm = bg_ref.shape[0]
    cc = 512
    row = lax.broadcasted_iota(i32, (tm, cc), 0)
    for c in range(D_MODEL // cc):
        sl = slice(c * cc, (c + 1) * cc)
        u = cg_ref[:, sl].astype(f32) * hx_ref[:, sl].astype(f32)
        uh = cgh_ref[:, sl].astype(f32) * hxh_ref[:, sl].astype(f32)
        uh = jnp.where(i > 0, uh, 0.0)
        u1 = jnp.where(row == 0, uh[15:16, :], pltpu.roll(u, 1, 0))
        u2 = jnp.where(row == 0, uh[14:15, :], jnp.where(row == 1, uh[15:16, :], pltpu.roll(u, 2, 0)))
        cw = cw_ref[:, sl]
        conv = cw[0:1, :] * u2 + cw[1:2, :] * u1 + cw[2:3, :] * u
        a_ref[:, sl] = (bg_ref[:, sl].astype(f32) * conv).astype(bf16)
    y = _dot(a_ref[...], w_ref[...])
    o_ref[...] = jax.nn.sigmoid(ma_ref[...].astype(f32)) * y


def _mixer_a(projn, conv_w, w_out, tm):
    t = projn.shape[0]
    hb = tm // 16
    wide = lambda c: pl.BlockSpec((tm, D_MODEL), lambda i, c=c: (i, c))
    halo = lambda c: pl.BlockSpec((16, D_MODEL), lambda i, c=c: (jnp.maximum(i * hb - 1, 0), c))
    return pl.pallas_call(
        _ya_kernel,
        grid=(t // tm,),
        in_specs=[wide(0), wide(1), wide(2), halo(1), halo(2),
                  pl.BlockSpec((3, D_MODEL), lambda i: (0, 0)),
                  pl.BlockSpec((D_MODEL, D_MODEL), lambda i: (0, 0)),
                  wide(_C_MA // 16)],
        out_specs=pl.BlockSpec((tm, D_MODEL), lambda i: (i, 0)),
        out_shape=jax.ShapeDtypeStruct((t, D_MODEL), f32),
        scratch_shapes=[pltpu.VMEM((tm, D_MODEL), bf16)],
        compiler_params=_params(("parallel",)),
        name="mixer_a",
    )(projn, projn, projn, projn, projn, conv_w, w_out, projn)


def _compress_kernel(x_ref, w_ref, pe_ref, w2_ref, w2t_ref, on_ref, ot_ref, acc_ref, pet_ref):
    l = pl.program_id(1)
    nl = pl.num_programs(1)
    n = x_ref.shape[0]

    @pl.when(l == 0)
    def _():
        acc_ref[...] = jnp.zeros_like(acc_ref)
        pet_ref[...] = jnp.zeros_like(pet_ref)

    wblk = w_ref[0]
    for h in range(N_KV):
        acc_ref[h] += _dot(x_ref[:, h * HD:(h + 1) * HD], wblk)
    pet_ref[...] += _dot(pe_ref[0, 0].astype(bf16), wblk)

    @pl.when(l == nl - 1)
    def _():
        r = pet_ref[...]
        peterm = r[0:1, :CMP_HIDDEN] + r[1:2, CMP_HIDDEN:]
        for h in range(N_KV):
            ab = acc_ref[h]
            hid = ab[:, :CMP_HIDDEN] + pltpu.roll(ab[:, CMP_HIDDEN:], n - 1, 0) + peterm
            gl = jax.nn.gelu(hid).astype(bf16)
            on_ref[0, h] = _dot(gl, w2_ref[0]).astype(bf16)
            ot_ref[0, h] = _dot_nt(w2t_ref[0], gl).astype(bf16)


def _compress(kvc, pe2, w1ab, w2, w2t):
    t = kvc.shape[0]
    n = t // CMP_STRIDE
    xr = kvc.reshape(n, CMP_STRIDE * 2 * N_KV * HD)
    return pl.pallas_call(
        _compress_kernel,
        grid=(2, CMP_STRIDE),
        in_specs=[pl.BlockSpec((n, N_KV * HD), lambda kv, l: (0, 2 * l + kv)),
                  pl.BlockSpec((1, HD, 2 * CMP_HIDDEN), lambda kv, l: (kv, l, 0)),
                  pl.BlockSpec((1, 1, 8, HD), lambda kv, l: (kv, l, 0, 0)),
                  pl.BlockSpec((1, CMP_HIDDEN, HD), lambda kv, l: (kv, 0, 0)),
                  pl.BlockSpec((1, HD, CMP_HIDDEN), lambda kv, l: (kv, 0, 0))],
        out_specs=[pl.BlockSpec((1, N_KV, n, HD), lambda kv, l: (kv, 0, 0, 0)),
                   pl.BlockSpec((1, N_KV, HD, n), lambda kv, l: (kv, 0, 0, 0))],
        out_shape=[jax.ShapeDtypeStruct((2, N_KV, n, HD), bf16),
                   jax.ShapeDtypeStruct((2, N_KV, HD, n), bf16)],
        scratch_shapes=[pltpu.VMEM((N_KV, n, 2 * CMP_HIDDEN), f32),
                        pltpu.VMEM((8, 2 * CMP_HIDDEN), f32)],
        compiler_params=_params(("arbitrary", "arbitrary")),
        name="compress",
    )(xr, w1ab, pe2, w2, w2t)


def _rel_bucket(dist):
    n = jnp.maximum(dist, 0)
    nf = jnp.maximum(n, 1).astype(f32)
    large = REL_EXACT + (jnp.log(nf / REL_EXACT) / math.log(REL_MAX_DIST / REL_EXACT)
                         * (REL_BUCKETS - REL_EXACT)).astype(i32)
    large = jnp.minimum(large, REL_BUCKETS - 1)
    return jnp.where(n < REL_EXACT, n, large)


def _bias_kernel(tbl_ref, o_ref, *, kstride, koff):
    hk = pl.program_id(0)
    o = pl.program_id(1)
    ki = lax.broadcasted_iota(i32, (LANES, LANES), 0)
    qi = lax.broadcasted_iota(i32, (LANES, LANES), 1)
    dist = o * LANES + qi - kstride * ki - koff
    bucket = _rel_bucket(dist)
    for g in range(GROUP):
        h = hk * GROUP + g
        acc = jnp.zeros((LANES, LANES), f32)
        for k in range(REL_BUCKETS):
            acc = jnp.where(bucket == k, tbl_ref[k, h], acc)
        o_ref[0, 0, :, g * LANES:(g + 1) * LANES] = jnp.where(dist >= 0, acc * LOG2E, NEG)


def _bias_tiles(rel_bias, n_tiles, kstride, koff, name):
    return pl.pallas_call(
        functools.partial(_bias_kernel, kstride=kstride, koff=koff),
        grid=(N_KV, n_tiles),
        in_specs=[pl.BlockSpec(memory_space=pltpu.SMEM)],
        out_specs=pl.BlockSpec((1, 1, LANES, GROUP * LANES), lambda hk, o: (hk, o, 0, 0)),
        out_shape=jax.ShapeDtypeStruct((N_KV, n_tiles, LANES, GROUP * LANES), f32),
        compiler_params=_params(("parallel", "arbitrary")),
        name=name,
    )(rel_bias)


def _query_t(q_ref):
    qb = q_ref[0]
    return jnp.concatenate([qb[g * HD:(g + 1) * HD, :] for g in range(GROUP)], axis=1)


def _cmp_kernel(q_ref, kc_ref, vct_ref, bc_ref, ov_ref, g_ref, o_ref, ns_ref, *, n_sel):
    b = pl.program_id(1)
    ncp = kc_ref.shape[2]
    nslc = ov_ref.shape[0]
    gl = GROUP * LANES
    qt = _query_t(q_ref)
    s = _dot(kc_ref[0, 0], qt)

    qi = lax.broadcasted_iota(i32, (LANES, gl), 1) & (LANES - 1)
    ci = lax.broadcasted_iota(i32, (LANES, gl), 0)
    parts = []
    for c in range(ncp // LANES):
        o = b - (LANES // (QB // CMP_STRIDE)) * c
        bias = bc_ref[0, jnp.clip(o, 0, N_CMP_BIAS - 1)]
        dist = o * QB + qi - CMP_STRIDE * ci - (CMP_LEN - 1)
        parts.append(jnp.where(dist >= 0, s[c * LANES:(c + 1) * LANES] + bias, NEG))
    s = jnp.concatenate(parts, axis=0) if len(parts) > 1 else parts[0]

    m = jnp.max(s, axis=0, keepdims=True)
    e = jnp.exp2(s - m)
    l = jnp.sum(e, axis=0, keepdims=True)
    p = jnp.where(s > 0.5 * NEG, e / l, 0.0)

    oct_ = _dot(vct_ref[0, 0], p.astype(bf16))

    pg = p[:, 0:LANES] + p[:, LANES:2 * LANES] + p[:, 2 * LANES:3 * LANES] + p[:, 3 * LANES:4 * LANES]
    hi = pg.astype(bf16)
    lo = (pg - hi.astype(f32)).astype(bf16)
    imp = _dot(ov_ref[...], hi) + _dot(ov_ref[...], lo)

    j = lax.broadcasted_iota(i32, (nslc, LANES), 0)
    tq = b * QB + lax.broadcasted_iota(i32, (nslc, LANES), 1)
    jt = jnp.right_shift(tq, SLC_LEN.bit_length() - 1)
    forced = (j == 0) | (j == jt) | (j == jt - 1)
    future = j * SLC_LEN > tq
    work = jnp.where(forced, FORCE_SCORE, jnp.where(future, -1.0, imp))
    jf = j.astype(f32)
    sel = jnp.zeros((nslc, LANES), f32)
    for _ in range(n_sel):
        mx = jnp.max(work, axis=0, keepdims=True)
        jm = jnp.min(jnp.where(work == mx, jf, 1e9), axis=0, keepdims=True)
        pick = jf == jm
        sel = jnp.where(pick, 1.0, sel)
        work = jnp.where(pick, -3e38, work)
    nsel = jnp.where(future, 1.0, 1.0 - sel).astype(bf16)
    nsr = ns_ref.shape[1]
    if nsr > nslc:
        nsel = jnp.concatenate([nsel, jnp.ones((nsr - nslc, LANES), bf16)], axis=0)
    ns_ref[0] = nsel

    gate = jax.nn.sigmoid(g_ref[0].astype(f32))
    for g in range(GROUP):
        sl = slice(g * LANES, (g + 1) * LANES)
        og = oct_[:, sl] * gate[g * 3:g * 3 + 1, :]
        o_ref[:, sl] = og.T.astype(o_ref.dtype)


def _cmp_attention(projt, cn, ct, bias_c, ov_t, t, nsr):
    nb = t // QB
    ncp = t // CMP_STRIDE
    nslc = t // SLC_LEN
    per = KT // QB
    n_sel = min(SLC_TOPN, nslc)
    return pl.pallas_call(
        functools.partial(_cmp_kernel, n_sel=n_sel),
        grid=(N_KV, nb),
        in_specs=[pl.BlockSpec((1, GROUP * HD, QB), lambda hk, b: (b // per, hk, b % per)),
                  pl.BlockSpec((1, 1, ncp, HD), lambda hk, b: (0, hk, 0, 0)),
                  pl.BlockSpec((1, 1, HD, ncp), lambda hk, b: (1, hk, 0, 0)),
                  pl.BlockSpec((1, N_CMP_BIAS, LANES, GROUP * LANES), lambda hk, b: (hk, 0, 0, 0)),
                  pl.BlockSpec((nslc, ncp), lambda hk, b: (0, 0)),
                  pl.BlockSpec((1, 16, QB), lambda hk, b: (b // per, _R_G * 8 + hk, b % per))],
        out_specs=[pl.BlockSpec((QB, GROUP * HD), lambda hk, b: (b, hk)),
                   pl.BlockSpec((1, nsr, QB), lambda hk, b: (hk, 0, b))],
        out_shape=[jax.ShapeDtypeStruct((t, N_HEADS * HD), f32),
                   jax.ShapeDtypeStruct((N_KV, nsr, t), bf16)],
        compiler_params=_params(("parallel", "arbitrary")),
        name="cmp_attention",
    )(projt, cn, ct, bias_c, ov_t, projt)


def _slc_kernel(q_ref, ks_ref, vst_ref, kw0, kw1, kw2, kw3, kw4, vw0, vw1, vw2, vw3, vw4,
                ns_ref, bt_ref, ind_ref, g_ref, o1_ref, o_ref,
                qa_ref, s0_ref, s1_ref, p0_ref, p1_ref, acc_ref, m_ref, al_ref, mx_ref, off_ref):
    b = pl.program_id(1)
    gl = GROUP * LANES
    qt = _query_t(q_ref)

    n_chunks = qa_ref.shape[0]
    for jc in range(n_chunks):
        nsj = -ns_ref[0, jc * SEL_CHUNK:(jc + 1) * SEL_CHUNK, :]
        qa_ref[jc, 0:HD, :] = qt
        qa_ref[jc, HD:, :] = jnp.concatenate([nsj] * GROUP, axis=1)

    n_tiles = vst_ref.shape[0]
    per_chunk = SEL_CHUNK * SLC_LEN // KT
    sub = KT // LANES

    far_bias = bt_ref[0, N_TOK_BIAS - 1, 0:1, :]
    ones_rows = jnp.ones((8, KT), bf16)

    def scores(kt, s_ref, far):
        jc = kt // per_chunk
        r0 = pl.multiple_of(kt * KT, KT)
        i0 = pl.multiple_of((kt - jc * per_chunk) * KT, KT)
        kaug = jnp.concatenate([ks_ref[pl.ds(r0, KT), :], ind_ref[pl.ds(i0, KT), :]], axis=1)
        s = _dot(kaug, qa_ref[jc])
        if far:
            s_ref[...] = s
            mx_ref[...] = jnp.max(s, axis=0, keepdims=True) + far_bias
            off_ref[...] = far_bias
            return
        mx = None
        for jj in range(sub):
            rows = slice(jj * LANES, (jj + 1) * LANES)
            dl = b - (kt * sub + jj)
            sj = s[rows] + bt_ref[0, jnp.clip(dl, 0, N_TOK_BIAS - 1)]
            s_ref[rows, :] = sj
            mj = jnp.max(sj, axis=0, keepdims=True)
            mx = mj if mx is None else jnp.maximum(mx, mj)
        mx_ref[...] = mx
        off_ref[...] = jnp.zeros_like(off_ref)

    def values(kt):
        return jnp.concatenate([vst_ref[kt], ones_rows], axis=0)

    def step(kt, s_cur, p_cur, s_nxt, p_prev, far):
        m_old = m_ref[...]
        m_new = jnp.maximum(m_old, mx_ref[...])
        shift = m_new - off_ref[...]
        al_old = al_ref[...]
        scores(jnp.minimum(kt + 1, n_tiles - 1), s_nxt, far)
        acc_ref[...] = al_old * acc_ref[...] + _dot(values(jnp.maximum(kt - 1, 0)), p_prev[...])
        p_cur[...] = jnp.exp2(s_cur[...] - shift).astype(bf16)
        m_ref[...] = m_new
        al_ref[...] = jnp.exp2(m_old - m_new)

    acc_ref[...] = jnp.zeros_like(acc_ref)
    m_ref[...] = jnp.full_like(m_ref, NEG)
    al_ref[...] = jnp.ones_like(al_ref)
    p1_ref[...] = jnp.zeros_like(p1_ref)
    scores(0, s0_ref, False)

    n_pairs = b // (2 * KT // QB) + 1
    n_far = jnp.maximum((b - (N_TOK_BIAS - 2)) // sub, 0)
    n_far_pairs = jnp.maximum((n_far - 1) // 2, 0)

    def pair(far):
        def body(i, carry):
            step(2 * i, s0_ref, p0_ref, s1_ref, p1_ref, far)
            step(2 * i + 1, s1_ref, p1_ref, s0_ref, p0_ref, far)
            return carry
        return body

    lax.fori_loop(0, n_far_pairs, pair(True), 0)
    lax.fori_loop(n_far_pairs, n_pairs, pair(False), 0)
    acc = al_ref[...] * acc_ref[...] + _dot(values(2 * n_pairs - 1), p1_ref[...])

    qi = lax.broadcasted_iota(i32, (LANES, gl), 1) & (LANES - 1)
    ki = lax.broadcasted_iota(i32, (LANES, gl), 0)
    rel = qi - ki
    kws = (kw0, kw1, kw2, kw3, kw4)
    vws = (vw0, vw1, vw2, vw3, vw4)
    nw = len(kws)
    sw = _dot(jnp.concatenate([r[...] for r in kws], axis=0), qt)
    parts = []
    for jw in range(nw):
        dl = nw - 1 - jw
        dist = dl * LANES + rel
        valid = (dist >= 0) & (dist < WINDOW) & ((b - dl) * LANES + ki >= 0)
        parts.append(jnp.where(valid, sw[jw * LANES:(jw + 1) * LANES] + bt_ref[0, dl], NEG))
    sw = jnp.concatenate(parts, axis=0)
    mw = jnp.max(sw, axis=0, keepdims=True)
    pw = jnp.exp2(sw - mw)
    lw = jnp.sum(pw, axis=0, keepdims=True)
    owt = _dot(jnp.concatenate([r[0] for r in vws], axis=1), pw.astype(bf16)) / lw

    ost = acc[0:HD] / acc[HD:HD + 1]
    gate = jax.nn.sigmoid(g_ref[0].astype(f32))
    for g in range(GROUP):
        sl = slice(g * LANES, (g + 1) * LANES)
        ot = ost[:, sl] * gate[g * 3 + 1:g * 3 + 2, :] + owt[:, sl] * gate[g * 3 + 2:g * 3 + 3, :]
        o_ref[:, sl] = (ot.T + o1_ref[:, sl]).astype(o_ref.dtype)


def _slc_attention(projn, projt, nsel_t, bias_t, ind, o1, t):
    nb = t // QB
    per = KT // QB
    nsr = nsel_t.shape[1]
    nwin = WINDOW // QB + 1

    def kw_spec(jw):
        return pl.BlockSpec((QB, HD), lambda hk, b, jw=jw: (jnp.maximum(b - (nwin - 1) + jw, 0), _C_KW + hk))

    def vw_spec(jw):
        def imap(hk, b, jw=jw):
            bb = jnp.maximum(b - (nwin - 1) + jw, 0)
            return (bb // per, _R_VW + hk, bb % per)
        return pl.BlockSpec((1, HD, QB), imap)

    return pl.pallas_call(
        _slc_kernel,
        grid=(N_KV, nb),
        in_specs=[pl.BlockSpec((1, GROUP * HD, QB), lambda hk, b: (b // per, hk, b % per)),
                  pl.BlockSpec((t, HD), lambda hk, b: (0, _C_KS + hk)),
                  pl.BlockSpec((t // KT, HD, KT), lambda hk, b: (0, _R_VS + hk, 0))]
                 + [kw_spec(jw) for jw in range(nwin)]
                 + [vw_spec(jw) for jw in range(nwin)]
                 + [pl.BlockSpec((1, nsr, QB), lambda hk, b: (hk, 0, b)),
                    pl.BlockSpec((1, N_TOK_BIAS, LANES, GROUP * LANES), lambda hk, b: (hk, 0, 0, 0)),
                    pl.BlockSpec(ind.shape, lambda hk, b: (0, 0)),
                    pl.BlockSpec((1, 16, QB), lambda hk, b: (b // per, _R_G * 8 + hk, b % per)),
                    pl.BlockSpec((QB, GROUP * HD), lambda hk, b: (b, hk))],
        out_specs=pl.BlockSpec((QB, GROUP * HD), lambda hk, b: (b, hk)),
        out_shape=jax.ShapeDtypeStruct((t, N_HEADS * HD), bf16),
        scratch_shapes=[pltpu.VMEM((nsr // SEL_CHUNK, 2 * HD, GROUP * LANES), bf16),
                        pltpu.VMEM((KT, GROUP * LANES), f32),
                        pltpu.VMEM((KT, GROUP * LANES), f32),
                        pltpu.VMEM((KT, GROUP * LANES), bf16),
                        pltpu.VMEM((KT, GROUP * LANES), bf16),
                        pltpu.VMEM((HD + 8, GROUP * LANES), f32)]
                       + [pltpu.VMEM((1, GROUP * LANES), f32)] * 4,
        compiler_params=_params(("parallel", "arbitrary")),
        name="slc_attention",
    )(projt, projn, projt, *([projn] * nwin), *([projt] * nwin), nsel_t, bias_t, ind, projt, o1)


def _layer_norm(z, g, b):
    mu = jnp.mean(z, axis=-1, keepdims=True)
    var = jnp.mean(jnp.square(z - mu), axis=-1, keepdims=True)
    return (z - mu) * lax.rsqrt(var + LN_EPS) * g + b


def _post_kernel(o2_ref, ya_ref, mb_ref, x_ref, p_ref, wao_ref, wmix_ref, wple_ref, wpg_ref,
                 g_ref, b_ref, x1_ref, r2_ref, *, alpha):
    yb = _dot(o2_ref[...], wao_ref[...])
    mixed = ya_ref[...] + jax.nn.sigmoid(mb_ref[...].astype(f32)) * yb
    z = alpha * x_ref[...] + _dot(mixed.astype(bf16), wmix_ref[...])
    x1 = _layer_norm(z, g_ref[...], b_ref[...])
    x1b = x1.astype(bf16)
    x1_ref[...] = x1b
    ple = _dot(p_ref[...].astype(bf16), wple_ref[...]) * jax.nn.sigmoid(_dot(x1b, wpg_ref[...]))
    r2_ref[...] = alpha * x1 + ple


def _post_attention(o2, ya, projn, x2d, p2d, wao, wmix, wple, wpg, g1, b1, alpha, tm):
    t = x2d.shape[0]
    row = lambda w: pl.BlockSpec((tm, w), lambda i: (i, 0))
    full = lambda a: pl.BlockSpec(a.shape, lambda i: (0, 0), pipeline_mode=pl.Buffered(1))
    return pl.pallas_call(
        functools.partial(_post_kernel, alpha=alpha),
        grid=(t // tm,),
        in_specs=[row(D_MODEL), row(D_MODEL),
                  pl.BlockSpec((tm, D_MODEL), lambda i: (i, _C_MB // 16)),
                  row(D_MODEL), row(PLE_DIM),
                  full(wao), full(wmix), full(wple), full(wpg), full(g1), full(b1)],
        out_specs=[row(D_MODEL), row(D_MODEL)],
        out_shape=[jax.ShapeDtypeStruct((t, D_MODEL), bf16),
                   jax.ShapeDtypeStruct((t, D_MODEL), f32)],
        compiler_params=_params(("parallel",)),
        name="post_attention",
    )(o2, ya, projn, x2d, p2d, wao, wmix, wple, wpg, g1, b1)


def _mlp_kernel(x1_ref, r2_ref, wu_ref, wd_ref, g_ref, b_ref, o_ref):
    k = pl.program_id(1)

    @pl.when(k == 0)
    def _():
        o_ref[...] = r2_ref[...]

    h = jnp.square(jnp.maximum(_dot(x1_ref[...], wu_ref[...]), 0.0))
    o_ref[...] += _dot(h.astype(bf16), wd_ref[...])

    @pl.when(k == pl.num_programs(1) - 1)
    def _():
        o_ref[...] = _layer_norm(o_ref[...], g_ref[...], b_ref[...])


def _mlp(x1b, r2, wu, wd, g2, b2, tm, fc):
    t = x1b.shape[0]
    return pl.pallas_call(
        _mlp_kernel,
        grid=(t // tm, D_FF // fc),
        in_specs=[pl.BlockSpec((tm, D_MODEL), lambda i, k: (i, 0)),
                  pl.BlockSpec((tm, D_MODEL), lambda i, k: (i, 0)),
                  pl.BlockSpec((D_MODEL, fc), lambda i, k: (0, k)),
                  pl.BlockSpec((fc, D_MODEL), lambda i, k: (k, 0)),
                  pl.BlockSpec((1, D_MODEL), lambda i, k: (0, 0)),
                  pl.BlockSpec((1, D_MODEL), lambda i, k: (0, 0))],
        out_specs=pl.BlockSpec((tm, D_MODEL), lambda i, k: (i, 0)),
        out_shape=jax.ShapeDtypeStruct((t, D_MODEL), f32),
        compiler_params=_params(("parallel", "arbitrary")),
        name="mlp",
    )(x1b, r2, wu, wd, g2, b2)


def _layer(x2d, p2d, w_in, conv_w, pe_k, w1_k, w2_k, pe_v, w1_v, w2_v, w_conv_out, w_attn_out,
           w_mix_out, ln1_g, ln1_b, w_mlp_up, w_mlp_down, w_ple, w_ple_gate, ln2_g, ln2_b,
           rel_bias, alpha):
    t = x2d.shape[0]
    assert t % 1024 == 0 and x2d.shape[1] == D_MODEL
    kvw = N_KV * HD
    o_q = 3 * D_MODEL
    o_kc = o_q + N_HEADS * HD
    o_ng = o_kc + 6 * kvw
    o_ma = o_ng + N_HEADS * 3

    def cols(a, n):
        return w_in[:, a:a + n]

    w_nat = jnp.concatenate(
        [cols(0, 3 * D_MODEL), cols(o_ma, 2 * D_MODEL), cols(o_kc + 2 * kvw, kvw),
         cols(o_kc + 4 * kvw, kvw)], axis=1).astype(bf16)
    wng = jnp.pad(cols(o_ng, N_HEADS * 3).reshape(D_MODEL, N_KV, GROUP * 3), ((0, 0), (0, 0), (0, 16 - GROUP * 3)))
    w_t = jnp.concatenate(
        [cols(o_q, N_HEADS * HD) * (HD ** -0.5 * LOG2E), cols(o_kc + 3 * kvw, kvw),
         cols(o_kc + 5 * kvw, kvw), wng.reshape(D_MODEL, N_KV * 16),
         jnp.zeros((D_MODEL, _N_T - _R_G * LANES - N_KV * 16), f32)], axis=1).T.astype(bf16)

    xb = x2d.astype(bf16)
    projn = _proj_nat(xb, w_nat, 1024, 1024, "proj_nat")
    kvc = _proj_nat(xb, cols(o_kc, 2 * kvw).astype(bf16), 1024, 1024, "proj_cmp")
    projt = _proj_t(w_t, xb, 640, KT)

    ya = _mixer_a(projn, conv_w, w_conv_out.astype(bf16), 512)

    half = CMP_STRIDE * HD
    w1ab = jnp.stack([jnp.concatenate([w[:half], w[half:]], axis=1) for w in (w1_k, w1_v)]).astype(bf16)
    pe2 = jnp.stack([jnp.stack([pe[:CMP_STRIDE], pe[CMP_STRIDE:]], axis=1) for pe in (pe_k, pe_v)])
    pe2 = jnp.pad(pe2, ((0, 0), (0, 0), (0, 6), (0, 0)))
    w2 = jnp.stack([w2_k, w2_v]).astype(bf16)
    cn, ct = _compress(kvc, pe2, w1ab, w2, jnp.swapaxes(w2, 1, 2))

    bias_t = _bias_tiles(rel_bias, N_TOK_BIAS, 1, 0, "bias_tok")
    bias_c = _bias_tiles(rel_bias, N_CMP_BIAS, CMP_STRIDE, CMP_LEN - 1, "bias_cmp")

    ncp = t // CMP_STRIDE
    nslc = t // SLC_LEN
    nsr = -(-nslc // SEL_CHUNK) * SEL_CHUNK
    ci = jnp.arange(ncp)[None, :] * CMP_STRIDE
    sj = jnp.arange(nslc)[:, None] * SLC_LEN
    ov_t = ((ci < sj + SLC_LEN) & (ci + CMP_LEN > sj)).astype(bf16)
    pos = jnp.arange(min(t, SEL_CHUNK * SLC_LEN))[:, None] // SLC_LEN
    ind = jnp.where(pos == jnp.arange(SEL_CHUNK)[None, :], SEL_BIG, 0.0).astype(bf16)

    o1, nsel_t = _cmp_attention(projt, cn, ct, bias_c, ov_t, t, nsr)
    o2 = _slc_attention(projn, projt, nsel_t, bias_t, ind, o1, t)

    x1b, r2 = _post_attention(o2, ya, projn, x2d, p2d, w_attn_out.astype(bf16), w_mix_out.astype(bf16),
                              w_ple.astype(bf16), w_ple_gate.astype(bf16),
                              ln1_g[None, :], ln1_b[None, :], alpha, 256)
    return _mlp(x1b, r2, w_mlp_up.astype(bf16), w_mlp_down.astype(bf16), ln2_g[None, :], ln2_b[None, :], 512, 512)


def kernel(x, p, w_in, conv_w, cmp_pe_k, cmp_w1_k, cmp_w2_k, cmp_pe_v, cmp_w1_v, cmp_w2_v, w_conv_out, w_attn_out, w_mix_out, ln1_g, ln1_b, w_mlp_up, w_mlp_down, w_ple, w_ple_gate, ln2_g, ln2_b, rel_bias):
    bsz, t, d = x.shape
    assert bsz == 1
    depth = w_in.shape[0]
    alpha = (2 * depth) ** 0.25
    x2d = x[0]
    for i in range(depth):
        x2d = _layer(x2d, p[i, 0], w_in[i], conv_w[i], cmp_pe_k[i], cmp_w1_k[i], cmp_w2_k[i],
                     cmp_pe_v[i], cmp_w1_v[i], cmp_w2_v[i], w_conv_out[i], w_attn_out[i], w_mix_out[i],
                     ln1_g[i], ln1_b[i], w_mlp_up[i], w_mlp_down[i], w_ple[i], w_ple_gate[i],
                     ln2_g[i], ln2_b[i], rel_bias, alpha)
    return x2d[None]
```

```python
import functools
import math

import jax
import jax.numpy as jnp
from jax import lax
from jax.experimental import pallas as pl
from jax.experimental.pallas import tpu as pltpu

f32 = jnp.float32
bf16 = jnp.bfloat16
i32 = jnp.int32

D_MODEL = 2048
N_HEADS = 16
N_KV = 4
GROUP = N_HEADS // N_KV
HD = 128
CMP_LEN = 32
CMP_STRIDE = 16
CMP_HIDDEN = 2 * HD
SLC_LEN = 64
SLC_TOPN = 16
WINDOW = 512
QB = 128
D_FF = 4 * D_MODEL
PLE_DIM = 256
REL_BUCKETS = 32
REL_EXACT = REL_BUCKETS // 2
REL_MAX_DIST = 4096
LN_EPS = 1e-5
NEG = -1e30
FORCE_SCORE = 1e9
LOG2E = 1.4426950408889634

LANES = 128
KT = 512
SEL_BIG = 65536.0
SEL_CHUNK = 128
N_TOK_BIAS = 25
N_CMP_BIAS = 40
VMEM_LIMIT = 56 * 1024 * 1024

_C_BG, _C_CG, _C_HX = 0, 16, 32
_C_MA, _C_MB = 48, 64
_C_KS, _C_KW = 80, 84
_N_NAT = 88 * LANES
_R_Q, _R_VS, _R_VW, _R_G = 0, 16, 20, 24
_N_T = 25 * LANES


def _dot(a, b):
    return jnp.dot(a, b, preferred_element_type=f32)


def _dot_nt(a, b):
    return lax.dot_general(a, b, (((1,), (1,)), ((), ())), preferred_element_type=f32)


def _params(sem, vmem=VMEM_LIMIT):
    return pltpu.CompilerParams(dimension_semantics=sem, vmem_limit_bytes=vmem)


def _mm_kernel(x_ref, w_ref, o_ref):
    o_ref[...] = _dot(x_ref[...], w_ref[...]).astype(o_ref.dtype)


def _proj_nat(xb, w, tm, tn, name):
    m, k = xb.shape
    n = w.shape[1]
    return pl.pallas_call(
        _mm_kernel,
        grid=(m // tm, n // tn),
        in_specs=[pl.BlockSpec((tm, k), lambda i, j: (i, 0)),
                  pl.BlockSpec((k, tn), lambda i, j: (0, j))],
        out_specs=pl.BlockSpec((tm, tn), lambda i, j: (i, j)),
        out_shape=jax.ShapeDtypeStruct((m, n), bf16),
        compiler_params=_params(("parallel", "arbitrary")),
        name=name,
    )(xb, w)


def _mmt_kernel(w_ref, x_ref, o_ref):
    o_ref[0] = _dot_nt(w_ref[...], x_ref[...]).astype(o_ref.dtype)


def _proj_t(wt, xb, tm, tn):
    n, k = wt.shape
    t = xb.shape[0]
    return pl.pallas_call(
        _mmt_kernel,
        grid=(n // tm, t // tn),
        in_specs=[pl.BlockSpec((tm, k), lambda i, j: (i, 0)),
                  pl.BlockSpec((tn, k), lambda i, j: (j, 0))],
        out_specs=pl.BlockSpec((1, tm, tn), lambda i, j: (j, i, 0)),
        out_shape=jax.ShapeDtypeStruct((t // tn, n, tn), bf16),
        compiler_params=_params(("parallel", "arbitrary")),
        name="proj_t",
    )(wt, xb)


def _ya_kernel(bg_ref, cg_ref, hx_ref, cgh_ref, hxh_ref, cw_ref, w_ref, ma_ref, o_ref, a_ref):
    i = pl.program_id(0)
    tm = bg_ref.shape[0]
    cc = 512
    row = lax.broadcasted_iota(i32, (tm, cc), 0)
    for c in range(D_MODEL // cc):
        sl = slice(c * cc, (c + 1) * cc)
        u = cg_ref[:, sl].astype(f32) * hx_ref[:, sl].astype(f32)
        uh = cgh_ref[:, sl].astype(f32) * hxh_ref[:, sl].astype(f32)
        uh = jnp.where(i > 0, uh, 0.0)
        u1 = jnp.where(row == 0, uh[15:16, :], pltpu.roll(u, 1, 0))
        u2 = jnp.where(row == 0, uh[14:15, :], jnp.where(row == 1, uh[15:16, :], pltpu.roll(u, 2, 0)))
        cw = cw_ref[:, sl]
        conv = cw[0:1, :] * u2 + cw[1:2, :] * u1 + cw[2:3, :] * u
        a_ref[:, sl] = (bg_ref[:, sl].astype(f32) * conv).astype(bf16)
    y = _dot(a_ref[...], w_ref[...])
    o_ref[...] = jax.nn.sigmoid(ma_ref[...].astype(f32)) * y


def _mixer_a(projn, conv_w, w_out, tm):
    t = projn.shape[0]
    hb = tm // 16
    wide = lambda c: pl.BlockSpec((tm, D_MODEL), lambda i, c=c: (i, c))
    halo = lambda c: pl.BlockSpec((16, D_MODEL), lambda i, c=c: (jnp.maximum(i * hb - 1, 0), c))
    return pl.pallas_call(
        _ya_kernel,
        grid=(t // tm,),
        in_specs=[wide(0), wide(1), wide(2), halo(1), halo(2),
                  pl.BlockSpec((3, D_MODEL), lambda i: (0, 0)),
                  pl.BlockSpec((D_MODEL, D_MODEL), lambda i: (0, 0)),
                  wide(_C_MA // 16)],
        out_specs=pl.BlockSpec((tm, D_MODEL), lambda i: (i, 0)),
        out_shape=jax.ShapeDtypeStruct((t, D_MODEL), f32),
        scratch_shapes=[pltpu.VMEM((tm, D_MODEL), bf16)],
        compiler_params=_params(("parallel",)),
        name="mixer_a",
    )(projn, projn, projn, projn, projn, conv_w, w_out, projn)


def _compress_kernel(x_ref, w_ref, pe_ref, w2_ref, w2t_ref, on_ref, ot_ref, acc_ref, pet_ref):
    l = pl.program_id(1)
    nl = pl.num_programs(1)
    n = x_ref.shape[0]

    @pl.when(l == 0)
    def _():
        acc_ref[...] = jnp.zeros_like(acc_ref)
        pet_ref[...] = jnp.zeros_like(pet_ref)

    wblk = w_ref[0]
    for h in range(N_KV):
        acc_ref[h] += _dot(x_ref[:, h * HD:(h + 1) * HD], wblk)
    pet_ref[...] += _dot(pe_ref[0, 0].astype(bf16), wblk)

    @pl.when(l == nl - 1)
    def _():
        r = pet_ref[...]
        peterm = r[0:1, :CMP_HIDDEN] + r[1:2, CMP_HIDDEN:]
        for h in range(N_KV):
            ab = acc_ref[h]
            hid = ab[:, :CMP_HIDDEN] + pltpu.roll(ab[:, CMP_HIDDEN:], n - 1, 0) + peterm
            gl = jax.nn.gelu(hid).astype(bf16)
            on_ref[0, h] = _dot(gl, w2_ref[0]).astype(bf16)
            ot_ref[0, h] = _dot_nt(w2t_ref[0], gl).astype(bf16)


def _compress(kvc, pe2, w1ab, w2, w2t):
    t = kvc.shape[0]
    n = t // CMP_STRIDE
    xr = kvc.reshape(n, CMP_STRIDE * 2 * N_KV * HD)
    return pl.pallas_call(
        _compress_kernel,
        grid=(2, CMP_STRIDE),
        in_specs=[pl.BlockSpec((n, N_KV * HD), lambda kv, l: (0, 2 * l + kv)),
                  pl.BlockSpec((1, HD, 2 * CMP_HIDDEN), lambda kv, l: (kv, l, 0)),
                  pl.BlockSpec((1, 1, 8, HD), lambda kv, l: (kv, l, 0, 0)),
                  pl.BlockSpec((1, CMP_HIDDEN, HD), lambda kv, l: (kv, 0, 0)),
                  pl.BlockSpec((1, HD, CMP_HIDDEN), lambda kv, l: (kv, 0, 0))],
        out_specs=[pl.BlockSpec((1, N_KV, n, HD), lambda kv, l: (kv, 0, 0, 0)),
                   pl.BlockSpec((1, N_KV, HD, n), lambda kv, l: (kv, 0, 0, 0))],
        out_shape=[jax.ShapeDtypeStruct((2, N_KV, n, HD), bf16),
                   jax.ShapeDtypeStruct((2, N_KV, HD, n), bf16)],
        scratch_shapes=[pltpu.VMEM((N_KV, n, 2 * CMP_HIDDEN), f32),
                        pltpu.VMEM((8, 2 * CMP_HIDDEN), f32)],
        compiler_params=_params(("arbitrary", "arbitrary")),
        name="compress",
    )(xr, w1ab, pe2, w2, w2t)


def _rel_bucket(dist):
    n = jnp.maximum(dist, 0)
    nf = jnp.maximum(n, 1).astype(f32)
    large = REL_EXACT + (jnp.log(nf / REL_EXACT) / math.log(REL_MAX_DIST / REL_EXACT)
                         * (REL_BUCKETS - REL_EXACT)).astype(i32)
    large = jnp.minimum(large, REL_BUCKETS - 1)
    return jnp.where(n < REL_EXACT, n, large)


def _bias_kernel(tbl_ref, o_ref, *, kstride, koff, n_reg, edge_o, edge_max):
    hk = pl.program_id(0)
    o = pl.program_id(1)
    is_edge = o == n_reg
    ki = lax.broadcasted_iota(i32, (LANES, LANES), 0)
    qi = lax.broadcasted_iota(i32, (LANES, LANES), 1)
    dist = jnp.where(is_edge, edge_o, o) * LANES + qi - kstride * ki - koff
    valid = (dist >= 0) & (dist <= jnp.where(is_edge, edge_max, 2 ** 30))
    bucket = _rel_bucket(dist)
    for g in range(GROUP):
        h = hk * GROUP + g
        acc = jnp.zeros((LANES, LANES), f32)
        for k in range(REL_BUCKETS):
            acc = jnp.where(bucket == k, tbl_ref[k, h], acc)
        o_ref[0, 0, :, g * LANES:(g + 1) * LANES] = jnp.where(valid, acc * LOG2E, NEG)


def _bias_tiles(rel_bias, n_reg, kstride, koff, edge_o, edge_max, name):
    return pl.pallas_call(
        functools.partial(_bias_kernel, kstride=kstride, koff=koff, n_reg=n_reg, edge_o=edge_o,
                          edge_max=edge_max),
        grid=(N_KV, n_reg + 1),
        in_specs=[pl.BlockSpec(memory_space=pltpu.SMEM)],
        out_specs=pl.BlockSpec((1, 1, LANES, GROUP * LANES), lambda hk, o: (hk, o, 0, 0)),
        out_shape=jax.ShapeDtypeStruct((N_KV, n_reg + 1, LANES, GROUP * LANES), f32),
        compiler_params=_params(("parallel", "arbitrary")),
        name=name,
    )(rel_bias)


def _query_t(q_ref):
    qb = q_ref[0]
    return jnp.concatenate([qb[g * HD:(g + 1) * HD, :] for g in range(GROUP)], axis=1)


def _cmp_kernel(q_ref, kc_ref, vct_ref, bc_ref, ov_ref, g_ref, o_ref, ns_ref, *, n_sel, nb, n_branch):
    b = pl.program_id(1)
    nsr = ns_ref.shape[1]
    qt = _query_t(q_ref)
    gate = jax.nn.sigmoid(g_ref[0].astype(f32))
    cmp_per_q = QB // CMP_STRIDE
    q_per_tile = LANES // cmp_per_q

    def core(nt, rows):
        keys = nt * LANES
        s = _dot(kc_ref[0, 0, 0:keys, :], qt)
        parts = []
        for c in range(nt):
            o = b - q_per_tile * c
            idx = jnp.where(o < 0, N_CMP_BIAS, jnp.minimum(o, N_CMP_BIAS - 1))
            parts.append(s[c * LANES:(c + 1) * LANES] + bc_ref[0, idx])
        s = jnp.concatenate(parts, axis=0) if nt > 1 else parts[0]
        m = jnp.max(s, axis=0, keepdims=True)
        e = jnp.exp2(s - m)
        vals = jnp.concatenate([vct_ref[0, 0, :, 0:keys], jnp.ones((8, keys), bf16)], axis=0)
        acc = _dot(vals, e.astype(bf16))
        inv = jnp.where(m > 0.5 * NEG, 1.0 / acc[HD:HD + 1], 0.0)
        oct_ = acc[0:HD] * inv
        for g in range(GROUP):
            sl = slice(g * LANES, (g + 1) * LANES)
            o_ref[:, sl] = (oct_[:, sl] * gate[g * 3:g * 3 + 1, :]).T.astype(o_ref.dtype)

        pg = None
        for g in range(GROUP):
            sl = slice(g * LANES, (g + 1) * LANES)
            pgg = e[:, sl] * inv[:, sl]
            pg = pgg if pg is None else pg + pgg
        hi = pg.astype(bf16)
        lo = (pg - hi.astype(f32)).astype(bf16)
        ov = ov_ref[0:rows, 0:keys]
        imp = _dot(ov, hi) + _dot(ov, lo)

        j = lax.broadcasted_iota(i32, (rows, LANES), 0)
        tq = b * QB + lax.broadcasted_iota(i32, (rows, LANES), 1)
        jt = jnp.right_shift(tq, SLC_LEN.bit_length() - 1)
        forced = (j == 0) | (j == jt) | (j == jt - 1)
        future = j * SLC_LEN > tq
        work = jnp.where(forced, -3e38, jnp.where(future, -1.0, imp))
        jf = j.astype(f32)
        sel = jnp.where(forced, 1.0, 0.0)
        for _ in range(n_sel - 3):
            mx = jnp.max(work, axis=0, keepdims=True)
            jm = jnp.min(jnp.where(work == mx, jf, 1e9), axis=0, keepdims=True)
            pick = jf == jm
            sel = jnp.where(pick, 1.0, sel)
            work = jnp.where(pick, -3e38, work)
        ns_ref[0, 0:rows, :] = jnp.where(future, 1.0, 1.0 - sel).astype(bf16)
        if nsr > rows:
            ns_ref[0, rows:, :] = jnp.ones((nsr - rows, LANES), bf16)

    per_branch = nb // n_branch
    for k in range(n_branch):
        last_b = (k + 1) * per_branch - 1
        nt = last_b // q_per_tile + 1
        rows = (last_b + 1) * (QB // SLC_LEN)
        pl.when(b // per_branch == k)(functools.partial(core, nt, rows))


def _cmp_attention(projt, cn, ct, bias_c, ov_t, t, nsr):
    nb = t // QB
    ncp = t // CMP_STRIDE
    nslc = t // SLC_LEN
    per = KT // QB
    n_sel = min(SLC_TOPN, nslc)
    assert n_sel > 3
    n_branch = 4 if nb % 64 == 0 else 1
    return pl.pallas_call(
        functools.partial(_cmp_kernel, n_sel=n_sel, nb=nb, n_branch=n_branch),
        grid=(N_KV, nb),
        in_specs=[pl.BlockSpec((1, GROUP * HD, QB), lambda hk, b: (b // per, hk, b % per)),
                  pl.BlockSpec((1, 1, ncp, HD), lambda hk, b: (0, hk, 0, 0)),
                  pl.BlockSpec((1, 1, HD, ncp), lambda hk, b: (1, hk, 0, 0)),
                  pl.BlockSpec((1, N_CMP_BIAS + 1, LANES, GROUP * LANES), lambda hk, b: (hk, 0, 0, 0)),
                  pl.BlockSpec((nslc, ncp), lambda hk, b: (0, 0)),
                  pl.BlockSpec((1, 16, QB), lambda hk, b: (b // per, _R_G * 8 + hk, b % per))],
        out_specs=[pl.BlockSpec((QB, GROUP * HD), lambda hk, b: (b, hk)),
                   pl.BlockSpec((1, nsr, QB), lambda hk, b: (hk, 0, b))],
        out_shape=[jax.ShapeDtypeStruct((t, N_HEADS * HD), f32),
                   jax.ShapeDtypeStruct((N_KV, nsr, t), bf16)],
        compiler_params=_params(("parallel", "arbitrary")),
        name="cmp_attention",
    )(projt, cn, ct, bias_c, ov_t, projt)


def _slc_kernel(q_ref, ks_ref, vst_ref, kw0, kw1, kw2, kw3, kw4, vw0, vw1, vw2, vw3, vw4,
                ns_ref, bt_ref, ind_ref, g_ref, o1_ref, o_ref,
                qa_ref, s0_ref, s1_ref, p0_ref, p1_ref, acc_ref, ow_ref, m_ref, al_ref, mx_ref, off_ref):
    b = pl.program_id(1)
    qt = _query_t(q_ref)

    n_chunks = qa_ref.shape[0]
    for jc in range(n_chunks):
        nsj = -ns_ref[0, jc * SEL_CHUNK:(jc + 1) * SEL_CHUNK, :]
        qa_ref[jc, 0:HD, :] = qt
        qa_ref[jc, HD:, :] = jnp.concatenate([nsj] * GROUP, axis=1)

    n_tiles = vst_ref.shape[0]
    per_chunk = SEL_CHUNK * SLC_LEN // KT
    sub = KT // LANES

    far_bias = bt_ref[0, N_TOK_BIAS - 1, 0:1, :]
    ones_rows = jnp.ones((8, KT), bf16)

    def scores(kt, s_ref, far):
        jc = kt // per_chunk
        r0 = pl.multiple_of(kt * KT, KT)
        i0 = pl.multiple_of((kt - jc * per_chunk) * KT, KT)
        kaug = jnp.concatenate([ks_ref[pl.ds(r0, KT), :], ind_ref[pl.ds(i0, KT), :]], axis=1)
        s = _dot(kaug, qa_ref[jc])
        if far:
            s_ref[...] = s
            mx_ref[...] = jnp.max(s, axis=0, keepdims=True) + far_bias
            off_ref[...] = far_bias
            return
        mx = None
        for jj in range(sub):
            rows = slice(jj * LANES, (jj + 1) * LANES)
            dl = b - (kt * sub + jj)
            sj = s[rows] + bt_ref[0, jnp.clip(dl, 0, N_TOK_BIAS - 1)]
            s_ref[rows, :] = sj
            mj = jnp.max(sj, axis=0, keepdims=True)
            mx = mj if mx is None else jnp.maximum(mx, mj)
        mx_ref[...] = mx
        off_ref[...] = jnp.zeros_like(off_ref)

    def values(kt):
        return jnp.concatenate([vst_ref[kt], ones_rows], axis=0)

    def step(kt, s_cur, p_cur, s_nxt, p_prev, far):
        m_old = m_ref[...]
        m_new = jnp.maximum(m_old, mx_ref[...])
        shift = m_new - off_ref[...]
        al_old = al_ref[...]
        scores(jnp.minimum(kt + 1, n_tiles - 1), s_nxt, far)
        acc_ref[...] = al_old * acc_ref[...] + _dot(values(jnp.maximum(kt - 1, 0)), p_prev[...])
        p_cur[...] = jnp.exp2(s_cur[...] - shift).astype(bf16)
        m_ref[...] = m_new
        al_ref[...] = jnp.exp2(m_old - m_new)

    gate = jax.nn.sigmoid(g_ref[0].astype(f32))
    kws = (kw0, kw1, kw2, kw3, kw4)
    vws = (vw0, vw1, vw2, vw3, vw4)
    nw = len(kws)
    sw = _dot(jnp.concatenate([r[...] for r in kws], axis=0), qt)
    parts = []
    for jw in range(nw):
        dl = nw - 1 - jw
        tile = N_TOK_BIAS if dl == nw - 1 else dl
        before_start = jnp.where(b - dl >= 0, 0.0, NEG)
        parts.append(sw[jw * LANES:(jw + 1) * LANES] + bt_ref[0, tile] + before_start)
    sw = jnp.concatenate(parts, axis=0)
    pw = jnp.exp2(sw - jnp.max(sw, axis=0, keepdims=True)).astype(bf16)
    vw = jnp.concatenate([r[0] for r in vws], axis=1)
    accw = _dot(jnp.concatenate([vw, jnp.ones((8, nw * LANES), bf16)], axis=0), pw)
    owt = accw[0:HD] / accw[HD:HD + 1]
    for g in range(GROUP):
        sl = slice(g * LANES, (g + 1) * LANES)
        ow_ref[:, sl] = (owt[:, sl] * gate[g * 3 + 2:g * 3 + 3, :]).T + o1_ref[:, sl]

    acc_ref[...] = jnp.zeros_like(acc_ref)
    m_ref[...] = jnp.full_like(m_ref, NEG)
    al_ref[...] = jnp.ones_like(al_ref)
    p1_ref[...] = jnp.zeros_like(p1_ref)
    scores(0, s0_ref, False)

    n_pairs = b // (2 * KT // QB) + 1
    n_far = jnp.maximum((b - (N_TOK_BIAS - 2)) // sub, 0)
    n_far_pairs = jnp.maximum((n_far - 1) // 2, 0)

    def pair(far):
        def body(i, carry):
            step(2 * i, s0_ref, p0_ref, s1_ref, p1_ref, far)
            step(2 * i + 1, s1_ref, p1_ref, s0_ref, p0_ref, far)
            return carry
        return body

    lax.fori_loop(0, n_far_pairs, pair(True), 0)
    lax.fori_loop(n_far_pairs, n_pairs, pair(False), 0)
    acc = al_ref[...] * acc_ref[...] + _dot(values(2 * n_pairs - 1), p1_ref[...])

    ost = acc[0:HD] / acc[HD:HD + 1]
    for g in range(GROUP):
        sl = slice(g * LANES, (g + 1) * LANES)
        o_ref[:, sl] = ((ost[:, sl] * gate[g * 3 + 1:g * 3 + 2, :]).T + ow_ref[:, sl]).astype(o_ref.dtype)


def _slc_attention(projn, projt, nsel_t, bias_t, ind, o1, t):
    nb = t // QB
    per = KT // QB
    nsr = nsel_t.shape[1]
    nwin = WINDOW // QB + 1

    def kw_spec(jw):
        return pl.BlockSpec((QB, HD), lambda hk, b, jw=jw: (jnp.maximum(b - (nwin - 1) + jw, 0), _C_KW + hk))

    def vw_spec(jw):
        def imap(hk, b, jw=jw):
            bb = jnp.maximum(b - (nwin - 1) + jw, 0)
            return (bb // per, _R_VW + hk, bb % per)
        return pl.BlockSpec((1, HD, QB), imap)

    return pl.pallas_call(
        _slc_kernel,
        grid=(N_KV, nb),
        in_specs=[pl.BlockSpec((1, GROUP * HD, QB), lambda hk, b: (b // per, hk, b % per)),
                  pl.BlockSpec((t, HD), lambda hk, b: (0, _C_KS + hk)),
                  pl.BlockSpec((t // KT, HD, KT), lambda hk, b: (0, _R_VS + hk, 0))]
                 + [kw_spec(jw) for jw in range(nwin)]
                 + [vw_spec(jw) for jw in range(nwin)]
                 + [pl.BlockSpec((1, nsr, QB), lambda hk, b: (hk, 0, b)),
                    pl.BlockSpec((1, N_TOK_BIAS + 1, LANES, GROUP * LANES), lambda hk, b: (hk, 0, 0, 0)),
                    pl.BlockSpec(ind.shape, lambda hk, b: (0, 0)),
                    pl.BlockSpec((1, 16, QB), lambda hk, b: (b // per, _R_G * 8 + hk, b % per)),
                    pl.BlockSpec((QB, GROUP * HD), lambda hk, b: (b, hk))],
        out_specs=pl.BlockSpec((QB, GROUP * HD), lambda hk, b: (b, hk)),
        out_shape=jax.ShapeDtypeStruct((t, N_HEADS * HD), bf16),
        scratch_shapes=[pltpu.VMEM((nsr // SEL_CHUNK, 2 * HD, GROUP * LANES), bf16),
                        pltpu.VMEM((KT, GROUP * LANES), f32),
                        pltpu.VMEM((KT, GROUP * LANES), f32),
                        pltpu.VMEM((KT, GROUP * LANES), bf16),
                        pltpu.VMEM((KT, GROUP * LANES), bf16),
                        pltpu.VMEM((HD + 8, GROUP * LANES), f32),
                        pltpu.VMEM((QB, GROUP * HD), f32)]
                       + [pltpu.VMEM((1, GROUP * LANES), f32)] * 4,
        compiler_params=_params(("parallel", "arbitrary")),
        name="slc_attention",
    )(projt, projn, projt, *([projn] * nwin), *([projt] * nwin), nsel_t, bias_t, ind, projt, o1)


def _layer_norm(z, g, b):
    mu = jnp.mean(z, axis=-1, keepdims=True)
    var = jnp.mean(jnp.square(z - mu), axis=-1, keepdims=True)
    return (z - mu) * lax.rsqrt(var + LN_EPS) * g + b


def _post_kernel(o2_ref, ya_ref, mb_ref, x_ref, p_ref, wao_ref, wmix_ref, wple_ref, wpg_ref,
                 g_ref, b_ref, x1_ref, r2_ref, *, alpha):
    yb = _dot(o2_ref[...], wao_ref[...])
    mixed = ya_ref[...] + jax.nn.sigmoid(mb_ref[...].astype(f32)) * yb
    z = alpha * x_ref[...] + _dot(mixed.astype(bf16), wmix_ref[...])
    x1 = _layer_norm(z, g_ref[...], b_ref[...])
    x1b = x1.astype(bf16)
    x1_ref[...] = x1b
    ple = _dot(p_ref[...].astype(bf16), wple_ref[...]) * jax.nn.sigmoid(_dot(x1b, wpg_ref[...]))
    r2_ref[...] = alpha * x1 + ple


def _post_attention(o2, ya, projn, x2d, p2d, wao, wmix, wple, wpg, g1, b1, alpha, tm):
    t = x2d.shape[0]
    row = lambda w: pl.BlockSpec((tm, w), lambda i: (i, 0))
    full = lambda a: pl.BlockSpec(a.shape, lambda i: (0, 0), pipeline_mode=pl.Buffered(1))
    return pl.pallas_call(
        functools.partial(_post_kernel, alpha=alpha),
        grid=(t // tm,),
        in_specs=[row(D_MODEL), row(D_MODEL),
                  pl.BlockSpec((tm, D_MODEL), lambda i: (i, _C_MB // 16)),
                  row(D_MODEL), row(PLE_DIM),
                  full(wao), full(wmix), full(wple), full(wpg), full(g1), full(b1)],
        out_specs=[row(D_MODEL), row(D_MODEL)],
        out_shape=[jax.ShapeDtypeStruct((t, D_MODEL), bf16),
                   jax.ShapeDtypeStruct((t, D_MODEL), f32)],
        compiler_params=_params(("parallel",)),
        name="post_attention",
    )(o2, ya, projn, x2d, p2d, wao, wmix, wple, wpg, g1, b1)


def _mlp_kernel(x1_ref, r2_ref, wu_ref, wd_ref, g_ref, b_ref, o_ref):
    k = pl.program_id(1)

    @pl.when(k == 0)
    def _():
        o_ref[...] = r2_ref[...]

    h = jnp.square(jnp.maximum(_dot(x1_ref[...], wu_ref[...]), 0.0))
    o_ref[...] += _dot(h.astype(bf16), wd_ref[...])

    @pl.when(k == pl.num_programs(1) - 1)
    def _():
        o_ref[...] = _layer_norm(o_ref[...], g_ref[...], b_ref[...])


def _mlp(x1b, r2, wu, wd, g2, b2, tm, fc):
    t = x1b.shape[0]
    return pl.pallas_call(
        _mlp_kernel,
        grid=(t // tm, D_FF // fc),
        in_specs=[pl.BlockSpec((tm, D_MODEL), lambda i, k: (i, 0)),
                  pl.BlockSpec((tm, D_MODEL), lambda i, k: (i, 0)),
                  pl.BlockSpec((D_MODEL, fc), lambda i, k: (0, k)),
                  pl.BlockSpec((fc, D_MODEL), lambda i, k: (k, 0)),
                  pl.BlockSpec((1, D_MODEL), lambda i, k: (0, 0)),
                  pl.BlockSpec((1, D_MODEL), lambda i, k: (0, 0))],
        out_specs=pl.BlockSpec((tm, D_MODEL), lambda i, k: (i, 0)),
        out_shape=jax.ShapeDtypeStruct((t, D_MODEL), f32),
        compiler_params=_params(("parallel", "arbitrary")),
        name="mlp",
    )(x1b, r2, wu, wd, g2, b2)


def _layer(x2d, p2d, w_in, conv_w, pe_k, w1_k, w2_k, pe_v, w1_v, w2_v, w_conv_out, w_attn_out,
           w_mix_out, ln1_g, ln1_b, w_mlp_up, w_mlp_down, w_ple, w_ple_gate, ln2_g, ln2_b,
           rel_bias, alpha):
    t = x2d.shape[0]
    assert t % 1024 == 0 and x2d.shape[1] == D_MODEL
    kvw = N_KV * HD
    o_q = 3 * D_MODEL
    o_kc = o_q + N_HEADS * HD
    o_ng = o_kc + 6 * kvw
    o_ma = o_ng + N_HEADS * 3

    def cols(a, n):
        return w_in[:, a:a + n]

    w_nat = jnp.concatenate(
        [cols(0, 3 * D_MODEL), cols(o_ma, 2 * D_MODEL), cols(o_kc + 2 * kvw, kvw),
         cols(o_kc + 4 * kvw, kvw)], axis=1).astype(bf16)
    wng = jnp.pad(cols(o_ng, N_HEADS * 3).reshape(D_MODEL, N_KV, GROUP * 3), ((0, 0), (0, 0), (0, 16 - GROUP * 3)))
    w_t = jnp.concatenate(
        [cols(o_q, N_HEADS * HD) * (HD ** -0.5 * LOG2E), cols(o_kc + 3 * kvw, kvw),
         cols(o_kc + 5 * kvw, kvw), wng.reshape(D_MODEL, N_KV * 16),
         jnp.zeros((D_MODEL, _N_T - _R_G * LANES - N_KV * 16), f32)], axis=1).T.astype(bf16)

    xb = x2d.astype(bf16)
    projn = _proj_nat(xb, w_nat, 1024, 1024, "proj_nat")
    kvc = _proj_nat(xb, cols(o_kc, 2 * kvw).astype(bf16), 1024, 1024, "proj_cmp")
    projt = _proj_t(w_t, xb, 640, KT)

    ya = _mixer_a(projn, conv_w, w_conv_out.astype(bf16), 512)

    half = CMP_STRIDE * HD
    w1ab = jnp.stack([jnp.concatenate([w[:half], w[half:]], axis=1) for w in (w1_k, w1_v)]).astype(bf16)
    pe2 = jnp.stack([jnp.stack([pe[:CMP_STRIDE], pe[CMP_STRIDE:]], axis=1) for pe in (pe_k, pe_v)])
    pe2 = jnp.pad(pe2, ((0, 0), (0, 0), (0, 6), (0, 0)))
    w2 = jnp.stack([w2_k, w2_v]).astype(bf16)
    cn, ct = _compress(kvc, pe2, w1ab, w2, jnp.swapaxes(w2, 1, 2))

    bias_t = _bias_tiles(rel_bias, N_TOK_BIAS, 1, 0, WINDOW // QB, WINDOW - 1, "bias_tok")
    bias_c = _bias_tiles(rel_bias, N_CMP_BIAS, CMP_STRIDE, CMP_LEN - 1, 0, -1, "bias_cmp")

    ncp = t // CMP_STRIDE
    nslc = t // SLC_LEN
    nsr = -(-nslc // SEL_CHUNK) * SEL_CHUNK
    ci = jnp.arange(ncp)[None, :] * CMP_STRIDE
    sj = jnp.arange(nslc)[:, None] * SLC_LEN
    ov_t = ((ci < sj + SLC_LEN) & (ci + CMP_LEN > sj)).astype(bf16)
    pos = jnp.arange(min(t, SEL_CHUNK * SLC_LEN))[:, None] // SLC_LEN
    ind = jnp.where(pos == jnp.arange(SEL_CHUNK)[None, :], SEL_BIG, 0.0).astype(bf16)

    o1, nsel_t = _cmp_attention(projt, cn, ct, bias_c, ov_t, t, nsr)
    o2 = _slc_attention(projn, projt, nsel_t, bias_t, ind, o1, t)

    x1b, r2 = _post_attention(o2, ya, projn, x2d, p2d, w_attn_out.astype(bf16), w_mix_out.astype(bf16),
                              w_ple.astype(bf16), w_ple_gate.astype(bf16),
                              ln1_g[None, :], ln1_b[None, :], alpha, 256)
    return _mlp(x1b, r2, w_mlp_up.astype(bf16), w_mlp_down.astype(bf16), ln2_g[None, :], ln2_b[None, :], 512, 512)


def kernel(x, p, w_in, conv_w, cmp_pe_k, cmp_w1_k, cmp_w2_k, cmp_pe_v, cmp_w1_v, cmp_w2_v, w_conv_out, w_attn_out, w_mix_out, ln1_g, ln1_b, w_mlp_up, w_mlp_down, w_ple, w_ple_gate, ln2_g, ln2_b, rel_bias):
    bsz, t, d = x.shape
    assert bsz == 1
    depth = w_in.shape[0]
    alpha = (2 * depth) ** 0.25
    x2d = x[0]
    for i in range(depth):
        x2d = _layer(x2d, p[i, 0], w_in[i], conv_w[i], cmp_pe_k[i], cmp_w1_k[i], cmp_w2_k[i],
                     cmp_pe_v[i], cmp_w1_v[i], cmp_w2_v[i], w_conv_out[i], w_attn_out[i], w_mix_out[i],
                     ln1_g[i], ln1_b[i], w_mlp_up[i], w_mlp_down[i], w_ple[i], w_ple_gate[i],
                     ln2_g[i], ln2_b[i], rel_bias, alpha)
    return x2d[None]
```

```python
import functools
import math

import jax
import jax.numpy as jnp
from jax import lax
from jax.experimental import pallas as pl
from jax.experimental.pallas import tpu as pltpu

f32 = jnp.float32
bf16 = jnp.bfloat16
i32 = jnp.int32

D_MODEL = 2048
N_HEADS = 16
N_KV = 4
GROUP = N_HEADS // N_KV
HD = 128
CMP_LEN = 32
CMP_STRIDE = 16
CMP_HIDDEN = 2 * HD
SLC_LEN = 64
SLC_TOPN = 16
WINDOW = 512
QB = 128
D_FF = 4 * D_MODEL
PLE_DIM = 256
REL_BUCKETS = 32
REL_EXACT = REL_BUCKETS // 2
REL_MAX_DIST = 4096
LN_EPS = 1e-5
NEG = -1e30
FORCE_SCORE = 1e9
LOG2E = 1.4426950408889634

LANES = 128
KT = 512
SEL_BIG = 65536.0
SEL_CHUNK = 128
SLC_NQ = 1
N_TOK_BIAS = 25
N_CMP_BIAS = 40
VMEM_LIMIT = 56 * 1024 * 1024

_C_BG, _C_CG, _C_HX = 0, 16, 32
_C_KS, _C_KW = 48, 52
_N_NAT = 56 * LANES
_R_Q, _R_VS, _R_VW, _R_G = 0, 16, 20, 24
_N_T = 25 * LANES


def _dot(a, b):
    return jnp.dot(a, b, preferred_element_type=f32)


def _dot_nt(a, b):
    return lax.dot_general(a, b, (((1,), (1,)), ((), ())), preferred_element_type=f32)


def _params(sem, vmem=VMEM_LIMIT):
    return pltpu.CompilerParams(dimension_semantics=sem, vmem_limit_bytes=vmem)


def _mm_kernel(x_ref, w_ref, o_ref):
    o_ref[...] = _dot(x_ref[...], w_ref[...]).astype(o_ref.dtype)


def _proj_nat(xb, w, tm, tn, name, n=None, col_tile=lambda j: j):
    m, k = xb.shape
    n = w.shape[1] if n is None else n
    return pl.pallas_call(
        _mm_kernel,
        grid=(m // tm, n // tn),
        in_specs=[pl.BlockSpec((tm, k), lambda i, j: (i, 0)),
                  pl.BlockSpec((k, tn), lambda i, j: (0, col_tile(j)))],
        out_specs=pl.BlockSpec((tm, tn), lambda i, j: (i, j)),
        out_shape=jax.ShapeDtypeStruct((m, n), bf16),
        compiler_params=_params(("parallel", "arbitrary")),
        name=name,
    )(xb, w)


def _mmt_kernel(w_ref, x_ref, o_ref):
    o_ref[0] = _dot_nt(w_ref[...], x_ref[...]).astype(o_ref.dtype)


def _proj_t(wt, xb, tm, tn):
    n, k = wt.shape
    t = xb.shape[0]
    return pl.pallas_call(
        _mmt_kernel,
        grid=(n // tm, t // tn),
        in_specs=[pl.BlockSpec((tm, k), lambda i, j: (i, 0)),
                  pl.BlockSpec((tn, k), lambda i, j: (j, 0))],
        out_specs=pl.BlockSpec((1, tm, tn), lambda i, j: (j, i, 0)),
        out_shape=jax.ShapeDtypeStruct((t // tn, n, tn), bf16),
        compiler_params=_params(("parallel", "arbitrary")),
        name="proj_t",
    )(wt, xb)


def _ya_kernel(bg_ref, cg_ref, hx_ref, cgh_ref, hxh_ref, cw_ref, w_ref, ma_ref, o_ref, a_ref):
    i = pl.program_id(0)
    tm = bg_ref.shape[0]
    cc = 512
    row = lax.broadcasted_iota(i32, (tm, cc), 0)
    for c in range(D_MODEL // cc):
        sl = slice(c * cc, (c + 1) * cc)
        u = cg_ref[:, sl].astype(f32) * hx_ref[:, sl].astype(f32)
        uh = cgh_ref[:, sl].astype(f32) * hxh_ref[:, sl].astype(f32)
        uh = jnp.where(i > 0, uh, 0.0)
        u1 = jnp.where(row == 0, uh[15:16, :], pltpu.roll(u, 1, 0))
        u2 = jnp.where(row == 0, uh[14:15, :], jnp.where(row == 1, uh[15:16, :], pltpu.roll(u, 2, 0)))
        cw = cw_ref[:, sl]
        conv = cw[0:1, :] * u2 + cw[1:2, :] * u1 + cw[2:3, :] * u
        a_ref[:, sl] = (bg_ref[:, sl].astype(f32) * conv).astype(bf16)
    y = _dot(a_ref[...], w_ref[...])
    o_ref[...] = jax.nn.sigmoid(ma_ref[...].astype(f32)) * y


def _mixer_a(projn, mab, conv_w, w_out, tm):
    t = projn.shape[0]
    hb = tm // 16
    wide = lambda c: pl.BlockSpec((tm, D_MODEL), lambda i, c=c: (i, c))
    halo = lambda c: pl.BlockSpec((16, D_MODEL), lambda i, c=c: (jnp.maximum(i * hb - 1, 0), c))
    return pl.pallas_call(
        _ya_kernel,
        grid=(t // tm,),
        in_specs=[wide(0), wide(1), wide(2), halo(1), halo(2),
                  pl.BlockSpec((3, D_MODEL), lambda i: (0, 0)),
                  pl.BlockSpec((D_MODEL, D_MODEL), lambda i: (0, 0)),
                  wide(0)],
        out_specs=pl.BlockSpec((tm, D_MODEL), lambda i: (i, 0)),
        out_shape=jax.ShapeDtypeStruct((t, D_MODEL), f32),
        scratch_shapes=[pltpu.VMEM((tm, D_MODEL), bf16)],
        compiler_params=_params(("parallel",)),
        name="mixer_a",
    )(projn, projn, projn, projn, projn, conv_w, w_out, mab)


def _compress_kernel(x_ref, w_ref, pe_ref, w2_ref, w2t_ref, on_ref, ot_ref, acc_ref, pet_ref):
    l = pl.program_id(1)
    nl = pl.num_programs(1)
    n = x_ref.shape[0]

    @pl.when(l == 0)
    def _():
        acc_ref[...] = jnp.zeros_like(acc_ref)
        pet_ref[...] = jnp.zeros_like(pet_ref)

    wblk = w_ref[0]
    for h in range(N_KV):
        acc_ref[h] += _dot(x_ref[:, h * HD:(h + 1) * HD], wblk)
    pet_ref[...] += _dot(pe_ref[0, 0].astype(bf16), wblk)

    @pl.when(l == nl - 1)
    def _():
        r = pet_ref[...]
        peterm = r[0:1, :CMP_HIDDEN] + r[1:2, CMP_HIDDEN:]
        for h in range(N_KV):
            ab = acc_ref[h]
            hid = ab[:, :CMP_HIDDEN] + pltpu.roll(ab[:, CMP_HIDDEN:], n - 1, 0) + peterm
            gl = jax.nn.gelu(hid).astype(bf16)
            on_ref[0, h] = _dot(gl, w2_ref[0]).astype(bf16)
            ot_ref[0, h] = _dot_nt(w2t_ref[0], gl).astype(bf16)


def _compress(kvc, pe2, w1ab, w2, w2t):
    t = kvc.shape[0]
    n = t // CMP_STRIDE
    xr = kvc.reshape(n, CMP_STRIDE * 2 * N_KV * HD)
    return pl.pallas_call(
        _compress_kernel,
        grid=(2, CMP_STRIDE),
        in_specs=[pl.BlockSpec((n, N_KV * HD), lambda kv, l: (0, 2 * l + kv)),
                  pl.BlockSpec((1, HD, 2 * CMP_HIDDEN), lambda kv, l: (kv, l, 0)),
                  pl.BlockSpec((1, 1, 8, HD), lambda kv, l: (kv, l, 0, 0)),
                  pl.BlockSpec((1, CMP_HIDDEN, HD), lambda kv, l: (kv, 0, 0)),
                  pl.BlockSpec((1, HD, CMP_HIDDEN), lambda kv, l: (kv, 0, 0))],
        out_specs=[pl.BlockSpec((1, N_KV, n, HD), lambda kv, l: (kv, 0, 0, 0)),
                   pl.BlockSpec((1, N_KV, HD, n), lambda kv, l: (kv, 0, 0, 0))],
        out_shape=[jax.ShapeDtypeStruct((2, N_KV, n, HD), bf16),
                   jax.ShapeDtypeStruct((2, N_KV, HD, n), bf16)],
        scratch_shapes=[pltpu.VMEM((N_KV, n, 2 * CMP_HIDDEN), f32),
                        pltpu.VMEM((8, 2 * CMP_HIDDEN), f32)],
        compiler_params=_params(("arbitrary", "arbitrary")),
        name="compress",
    )(xr, w1ab, pe2, w2, w2t)


def _rel_bucket(dist):
    n = jnp.maximum(dist, 0)
    nf = jnp.maximum(n, 1).astype(f32)
    large = REL_EXACT + (jnp.log(nf / REL_EXACT) / math.log(REL_MAX_DIST / REL_EXACT)
                         * (REL_BUCKETS - REL_EXACT)).astype(i32)
    large = jnp.minimum(large, REL_BUCKETS - 1)
    return jnp.where(n < REL_EXACT, n, large)


def _bias_kernel(tbl_ref, o_ref, *, kstride, koff, n_reg, edge_o, edge_max):
    hk = pl.program_id(0)
    o = pl.program_id(1)
    is_edge = o == n_reg
    ki = lax.broadcasted_iota(i32, (LANES, LANES), 0)
    qi = lax.broadcasted_iota(i32, (LANES, LANES), 1)
    dist = jnp.where(is_edge, edge_o, o) * LANES + qi - kstride * ki - koff
    valid = (dist >= 0) & (dist <= jnp.where(is_edge, edge_max, 2 ** 30))
    bucket = _rel_bucket(dist)
    for g in range(GROUP):
        h = hk * GROUP + g
        acc = jnp.zeros((LANES, LANES), f32)
        for k in range(REL_BUCKETS):
            acc = jnp.where(bucket == k, tbl_ref[k, h], acc)
        o_ref[0, 0, :, g * LANES:(g + 1) * LANES] = jnp.where(valid, acc * LOG2E, NEG)


def _bias_tiles(rel_bias, n_reg, kstride, koff, edge_o, edge_max, name):
    return pl.pallas_call(
        functools.partial(_bias_kernel, kstride=kstride, koff=koff, n_reg=n_reg, edge_o=edge_o,
                          edge_max=edge_max),
        grid=(N_KV, n_reg + 1),
        in_specs=[pl.BlockSpec(memory_space=pltpu.SMEM)],
        out_specs=pl.BlockSpec((1, 1, LANES, GROUP * LANES), lambda hk, o: (hk, o, 0, 0)),
        out_shape=jax.ShapeDtypeStruct((N_KV, n_reg + 1, LANES, GROUP * LANES), f32),
        compiler_params=_params(("parallel", "arbitrary")),
        name=name,
    )(rel_bias)


def _query_t(q_ref):
    qb = q_ref[0]
    return jnp.concatenate([qb[g * HD:(g + 1) * HD, :] for g in range(GROUP)], axis=1)


def _cmp_kernel(q_ref, kc_ref, vct_ref, bc_ref, ov_ref, g_ref, *rest, n_sel, nb, n_branch):
    nw = WINDOW // QB + 1
    kws, vws = rest[:nw], rest[nw:2 * nw]
    bt_ref, o_ref, ns_ref = rest[2 * nw:]
    b = pl.program_id(1)
    nsr = ns_ref.shape[1]
    qt = _query_t(q_ref)
    gate = jax.nn.sigmoid(g_ref[0].astype(f32))
    cmp_per_q = QB // CMP_STRIDE
    q_per_tile = LANES // cmp_per_q

    def core(nt, rows):
        keys = nt * LANES
        s = _dot(kc_ref[0, 0, 0:keys, :], qt)
        parts = []
        for c in range(nt):
            o = b - q_per_tile * c
            idx = jnp.where(o < 0, N_CMP_BIAS, jnp.minimum(o, N_CMP_BIAS - 1))
            parts.append(s[c * LANES:(c + 1) * LANES] + bc_ref[0, idx])
        s = jnp.concatenate(parts, axis=0) if nt > 1 else parts[0]
        m = jnp.max(s, axis=0, keepdims=True)
        e = jnp.exp2(s - m)
        vals = jnp.concatenate([vct_ref[0, 0, :, 0:keys], jnp.ones((8, keys), bf16)], axis=0)
        acc = _dot(vals, e.astype(bf16))
        inv = jnp.where(m > 0.5 * NEG, 1.0 / acc[HD:HD + 1], 0.0)
        oct_ = acc[0:HD] * inv

        sw = _dot(jnp.concatenate([r[...] for r in kws], axis=0), qt)
        parts = []
        for jw in range(nw):
            dl = nw - 1 - jw
            tile = N_TOK_BIAS if dl == nw - 1 else dl
            before_start = jnp.where(b - dl >= 0, 0.0, NEG)
            parts.append(sw[jw * LANES:(jw + 1) * LANES] + bt_ref[0, tile] + before_start)
        sw = jnp.concatenate(parts, axis=0)
        pw = jnp.exp2(sw - jnp.max(sw, axis=0, keepdims=True)).astype(bf16)
        vw = jnp.concatenate([r[0] for r in vws], axis=1)
        accw = _dot(jnp.concatenate([vw, jnp.ones((8, nw * LANES), bf16)], axis=0), pw)
        owt = accw[0:HD] / accw[HD:HD + 1]
        for g in range(GROUP):
            sl = slice(g * LANES, (g + 1) * LANES)
            both = oct_[:, sl] * gate[g * 3:g * 3 + 1, :] + owt[:, sl] * gate[g * 3 + 2:g * 3 + 3, :]
            o_ref[:, sl] = both.T.astype(o_ref.dtype)

        pg = None
        for g in range(GROUP):
            sl = slice(g * LANES, (g + 1) * LANES)
            pgg = e[:, sl] * inv[:, sl]
            pg = pgg if pg is None else pg + pgg
        hi = pg.astype(bf16)
        lo = (pg - hi.astype(f32)).astype(bf16)
        ov = ov_ref[0:rows, 0:keys]
        imp = _dot(ov, hi) + _dot(ov, lo)

        j = lax.broadcasted_iota(i32, (rows, LANES), 0)
        tq = b * QB + lax.broadcasted_iota(i32, (rows, LANES), 1)
        jt = jnp.right_shift(tq, SLC_LEN.bit_length() - 1)
        forced = (j == 0) | (j == jt) | (j == jt - 1)
        future = j * SLC_LEN > tq
        work = jnp.where(forced, -3e38, jnp.where(future, -1.0, imp))
        jf = j.astype(f32)
        sel = jnp.where(forced, 1.0, 0.0)
        for _ in range(n_sel - 3):
            mx = jnp.max(work, axis=0, keepdims=True)
            jm = jnp.min(jnp.where(work == mx, jf, 1e9), axis=0, keepdims=True)
            pick = jf == jm
            sel = jnp.where(pick, 1.0, sel)
            work = jnp.where(pick, -3e38, work)
        ns_ref[0, 0:rows, :] = jnp.where(future, 1.0, 1.0 - sel).astype(bf16)
        if nsr > rows:
            ns_ref[0, rows:, :] = jnp.ones((nsr - rows, LANES), bf16)

    per_branch = nb // n_branch
    for k in range(n_branch):
        last_b = (k + 1) * per_branch - 1
        nt = last_b // q_per_tile + 1
        rows = (last_b + 1) * (QB // SLC_LEN)
        pl.when(b // per_branch == k)(functools.partial(core, nt, rows))


def _cmp_attention(projn, projt, cn, ct, bias_c, bias_t, ov_t, t, nsr):
    nb = t // QB
    ncp = t // CMP_STRIDE
    nslc = t // SLC_LEN
    per = KT // QB
    nw = WINDOW // QB + 1
    n_sel = min(SLC_TOPN, nslc)
    assert n_sel > 3
    n_branch = 4 if nb % 64 == 0 else 1
    resident = dict(pipeline_mode=pl.Buffered(1))

    def kw_spec(jw):
        return pl.BlockSpec((QB, HD), lambda hk, b, jw=jw: (jnp.maximum(b - (nw - 1) + jw, 0), _C_KW + hk))

    def vw_spec(jw):
        def imap(hk, b, jw=jw):
            bb = jnp.maximum(b - (nw - 1) + jw, 0)
            return (bb // per, _R_VW + hk, bb % per)
        return pl.BlockSpec((1, HD, QB), imap)

    return pl.pallas_call(
        functools.partial(_cmp_kernel, n_sel=n_sel, nb=nb, n_branch=n_branch),
        grid=(N_KV, nb),
        in_specs=[pl.BlockSpec((1, GROUP * HD, QB), lambda hk, b: (b // per, hk, b % per)),
                  pl.BlockSpec((1, 1, ncp, HD), lambda hk, b: (0, hk, 0, 0), **resident),
                  pl.BlockSpec((1, 1, HD, ncp), lambda hk, b: (1, hk, 0, 0), **resident),
                  pl.BlockSpec((1, N_CMP_BIAS + 1, LANES, GROUP * LANES), lambda hk, b: (hk, 0, 0, 0), **resident),
                  pl.BlockSpec((nslc, ncp), lambda hk, b: (0, 0), **resident),
                  pl.BlockSpec((1, 16, QB), lambda hk, b: (b // per, _R_G * 8 + hk, b % per))]
                 + [kw_spec(jw) for jw in range(nw)]
                 + [vw_spec(jw) for jw in range(nw)]
                 + [pl.BlockSpec((1, N_TOK_BIAS + 1, LANES, GROUP * LANES), lambda hk, b: (hk, 0, 0, 0), **resident)],
        out_specs=[pl.BlockSpec((QB, GROUP * HD), lambda hk, b: (b, hk)),
                   pl.BlockSpec((1, nsr, QB), lambda hk, b: (hk, 0, b))],
        out_shape=[jax.ShapeDtypeStruct((t, N_HEADS * HD), f32),
                   jax.ShapeDtypeStruct((N_KV, nsr, t), bf16)],
        compiler_params=_params(("parallel", "arbitrary")),
        name="cmp_attention",
    )(projt, cn, ct, bias_c, ov_t, projt, *([projn] * nw), *([projt] * nw), bias_t)


def _old_slc_kernel(q_ref, ks_ref, vst_ref, kw0, kw1, kw2, kw3, kw4, vw0, vw1, vw2, vw3, vw4,
                ns_ref, bt_ref, ind_ref, g_ref, o1_ref, o_ref,
                qa_ref, s0_ref, s1_ref, p0_ref, p1_ref, acc_ref, ow_ref, m_ref, al_ref, mx_ref, off_ref):
    b = pl.program_id(1)
    qt = _query_t(q_ref)

    n_chunks = qa_ref.shape[0]
    for jc in range(n_chunks):
        nsj = -ns_ref[0, jc * SEL_CHUNK:(jc + 1) * SEL_CHUNK, :]
        qa_ref[jc, 0:HD, :] = qt
        qa_ref[jc, HD:, :] = jnp.concatenate([nsj] * GROUP, axis=1)

    n_tiles = vst_ref.shape[0]
    per_chunk = SEL_CHUNK * SLC_LEN // KT
    sub = KT // LANES

    far_bias = bt_ref[0, N_TOK_BIAS - 1, 0:1, :]
    ones_rows = jnp.ones((8, KT), bf16)

    def scores(kt, s_ref, far):
        jc = kt // per_chunk
        r0 = pl.multiple_of(kt * KT, KT)
        i0 = pl.multiple_of((kt - jc * per_chunk) * KT, KT)
        kaug = jnp.concatenate([ks_ref[pl.ds(r0, KT), :], ind_ref[pl.ds(i0, KT), :]], axis=1)
        s = _dot(kaug, qa_ref[jc])
        if far:
            s_ref[...] = s
            mx_ref[...] = jnp.max(s, axis=0, keepdims=True) + far_bias
            off_ref[...] = far_bias
            return
        mx = None
        for jj in range(sub):
            rows = slice(jj * LANES, (jj + 1) * LANES)
            dl = b - (kt * sub + jj)
            sj = s[rows] + bt_ref[0, jnp.clip(dl, 0, N_TOK_BIAS - 1)]
            s_ref[rows, :] = sj
            mj = jnp.max(sj, axis=0, keepdims=True)
            mx = mj if mx is None else jnp.maximum(mx, mj)
        mx_ref[...] = mx
        off_ref[...] = jnp.zeros_like(off_ref)

    def values(kt):
        return jnp.concatenate([vst_ref[kt], ones_rows], axis=0)

    def step(kt, s_cur, p_cur, s_nxt, p_prev, far):
        m_old = m_ref[...]
        m_new = jnp.maximum(m_old, mx_ref[...])
        shift = m_new - off_ref[...]
        al_old = al_ref[...]
        scores(jnp.minimum(kt + 1, n_tiles - 1), s_nxt, far)
        acc_ref[...] = al_old * acc_ref[...] + _dot(values(jnp.maximum(kt - 1, 0)), p_prev[...])
        p_cur[...] = jnp.exp2(s_cur[...] - shift).astype(bf16)
        m_ref[...] = m_new
        al_ref[...] = jnp.exp2(m_old - m_new)

    gate = jax.nn.sigmoid(g_ref[0].astype(f32))
    kws = (kw0, kw1, kw2, kw3, kw4)
    vws = (vw0, vw1, vw2, vw3, vw4)
    nw = len(kws)
    sw = _dot(jnp.concatenate([r[...] for r in kws], axis=0), qt)
    parts = []
    for jw in range(nw):
        dl = nw - 1 - jw
        tile = N_TOK_BIAS if dl == nw - 1 else dl
        before_start = jnp.where(b - dl >= 0, 0.0, NEG)
        parts.append(sw[jw * LANES:(jw + 1) * LANES] + bt_ref[0, tile] + before_start)
    sw = jnp.concatenate(parts, axis=0)
    pw = jnp.exp2(sw - jnp.max(sw, axis=0, keepdims=True)).astype(bf16)
    vw = jnp.concatenate([r[0] for r in vws], axis=1)
    accw = _dot(jnp.concatenate([vw, jnp.ones((8, nw * LANES), bf16)], axis=0), pw)
    owt = accw[0:HD] / accw[HD:HD + 1]
    for g in range(GROUP):
        sl = slice(g * LANES, (g + 1) * LANES)
        ow_ref[:, sl] = (owt[:, sl] * gate[g * 3 + 2:g * 3 + 3, :]).T + o1_ref[:, sl]

    acc_ref[...] = jnp.zeros_like(acc_ref)
    m_ref[...] = jnp.full_like(m_ref, NEG)
    al_ref[...] = jnp.ones_like(al_ref)
    p1_ref[...] = jnp.zeros_like(p1_ref)
    scores(0, s0_ref, False)

    n_pairs = b // (2 * KT // QB) + 1
    n_far = jnp.maximum((b - (N_TOK_BIAS - 2)) // sub, 0)
    n_far_pairs = jnp.maximum((n_far - 1) // 2, 0)

    def pair(far):
        def body(i, carry):
            step(2 * i, s0_ref, p0_ref, s1_ref, p1_ref, far)
            step(2 * i + 1, s1_ref, p1_ref, s0_ref, p0_ref, far)
            return carry
        return body

    lax.fori_loop(0, n_far_pairs, pair(True), 0)
    lax.fori_loop(n_far_pairs, n_pairs, pair(False), 0)
    acc = al_ref[...] * acc_ref[...] + _dot(values(2 * n_pairs - 1), p1_ref[...])

    ost = acc[0:HD] / acc[HD:HD + 1]
    for g in range(GROUP):
        sl = slice(g * LANES, (g + 1) * LANES)
        o_ref[:, sl] = ((ost[:, sl] * gate[g * 3 + 1:g * 3 + 2, :]).T + ow_ref[:, sl]).astype(o_ref.dtype)


def _old_slc_attention(projn, projt, nsel_t, bias_t, ind, o1, t):
    nb = t // QB
    per = KT // QB
    nsr = nsel_t.shape[1]
    nwin = WINDOW // QB + 1

    def kw_spec(jw):
        return pl.BlockSpec((QB, HD), lambda hk, b, jw=jw: (jnp.maximum(b - (nwin - 1) + jw, 0), _C_KW + hk))

    def vw_spec(jw):
        def imap(hk, b, jw=jw):
            bb = jnp.maximum(b - (nwin - 1) + jw, 0)
            return (bb // per, _R_VW + hk, bb % per)
        return pl.BlockSpec((1, HD, QB), imap)

    return pl.pallas_call(
        _slc_kernel,
        grid=(N_KV, nb),
        in_specs=[pl.BlockSpec((1, GROUP * HD, QB), lambda hk, b: (b // per, hk, b % per)),
                  pl.BlockSpec((t, HD), lambda hk, b: (0, _C_KS + hk)),
                  pl.BlockSpec((t // KT, HD, KT), lambda hk, b: (0, _R_VS + hk, 0))]
                 + [kw_spec(jw) for jw in range(nwin)]
                 + [vw_spec(jw) for jw in range(nwin)]
                 + [pl.BlockSpec((1, nsr, QB), lambda hk, b: (hk, 0, b)),
                    pl.BlockSpec((1, N_TOK_BIAS + 1, LANES, GROUP * LANES), lambda hk, b: (hk, 0, 0, 0)),
                    pl.BlockSpec(ind.shape, lambda hk, b: (0, 0)),
                    pl.BlockSpec((1, 16, QB), lambda hk, b: (b // per, _R_G * 8 + hk, b % per)),
                    pl.BlockSpec((QB, GROUP * HD), lambda hk, b: (b, hk))],
        out_specs=pl.BlockSpec((QB, GROUP * HD), lambda hk, b: (b, hk)),
        out_shape=jax.ShapeDtypeStruct((t, N_HEADS * HD), bf16),
        scratch_shapes=[pltpu.VMEM((nsr // SEL_CHUNK, 2 * HD, GROUP * LANES), bf16),
                        pltpu.VMEM((KT, GROUP * LANES), f32),
                        pltpu.VMEM((KT, GROUP * LANES), f32),
                        pltpu.VMEM((KT, GROUP * LANES), bf16),
                        pltpu.VMEM((KT, GROUP * LANES), bf16),
                        pltpu.VMEM((HD + 8, GROUP * LANES), f32),
                        pltpu.VMEM((QB, GROUP * HD), f32)]
                       + [pltpu.VMEM((1, GROUP * LANES), f32)] * 4,
        compiler_params=_params(("parallel", "arbitrary")),
        name="slc_attention",
    )(projt, projn, projt, *([projn] * nwin), *([projt] * nwin), nsel_t, bias_t, ind, projt, o1)


def _slc_kernel(q_ref, ks_ref, vst_ref, ns_ref, bt_ref, ind_ref, g_ref, o1_ref, o_ref,
                qa_ref, s0_ref, s1_ref, p0_ref, p1_ref, acc_ref, m_ref, al_ref, mx_ref, off_ref, *, nq):
    b0 = pl.program_id(1) * nq
    gl = GROUP * LANES
    qb = q_ref[0]
    qt = jnp.concatenate([qb[g * HD:(g + 1) * HD, h * QB:(h + 1) * QB]
                          for h in range(nq) for g in range(GROUP)], axis=1)
    gate = jax.nn.sigmoid(g_ref[0].astype(f32))

    n_chunks = qa_ref.shape[0]
    for jc in range(n_chunks):
        nsj = -ns_ref[0, jc * SEL_CHUNK:(jc + 1) * SEL_CHUNK, :]
        qa_ref[jc, 0:HD, :] = qt
        qa_ref[jc, HD:, :] = jnp.concatenate([nsj[:, h * QB:(h + 1) * QB]
                                              for h in range(nq) for _ in range(GROUP)], axis=1)

    n_tiles = vst_ref.shape[0]
    per_chunk = SEL_CHUNK * SLC_LEN // KT
    sub = KT // LANES

    far_bias = jnp.concatenate([bt_ref[0, N_TOK_BIAS - 1, 0:1, :]] * nq, axis=1)
    ones_rows = jnp.ones((8, KT), bf16)

    def scores(kt, s_ref, far):
        jc = kt // per_chunk
        r0 = pl.multiple_of(kt * KT, KT)
        i0 = pl.multiple_of((kt - jc * per_chunk) * KT, KT)
        kaug = jnp.concatenate([ks_ref[pl.ds(r0, KT), :], ind_ref[pl.ds(i0, KT), :]], axis=1)
        s = _dot(kaug, qa_ref[jc])
        if far:
            s_ref[...] = s
            mx_ref[...] = jnp.max(s, axis=0, keepdims=True) + far_bias
            off_ref[...] = far_bias
            return
        mxs = []
        for h in range(nq):
            cols = slice(h * gl, (h + 1) * gl)
            mx = None
            for jj in range(sub):
                rows = slice(jj * LANES, (jj + 1) * LANES)
                dl = b0 + h - (kt * sub + jj)
                sj = s[rows, cols] + bt_ref[0, jnp.clip(dl, 0, N_TOK_BIAS - 1)]
                s_ref[rows, cols] = sj
                mj = jnp.max(sj, axis=0, keepdims=True)
                mx = mj if mx is None else jnp.maximum(mx, mj)
            mxs.append(mx)
        mx_ref[...] = jnp.concatenate(mxs, axis=1)
        off_ref[...] = jnp.zeros_like(off_ref)

    def values(kt):
        return jnp.concatenate([vst_ref[kt], ones_rows], axis=0)

    def step(kt, s_cur, p_cur, s_nxt, p_prev, far):
        m_old = m_ref[...]
        m_new = jnp.maximum(m_old, mx_ref[...])
        shift = m_new - off_ref[...]
        al_old = al_ref[...]
        scores(jnp.minimum(kt + 1, n_tiles - 1), s_nxt, far)
        acc_ref[...] = al_old * acc_ref[...] + _dot(values(jnp.maximum(kt - 1, 0)), p_prev[...])
        p_cur[...] = jnp.exp2(s_cur[...] - shift).astype(bf16)
        m_ref[...] = m_new
        al_ref[...] = jnp.exp2(m_old - m_new)

    acc_ref[...] = jnp.zeros_like(acc_ref)
    m_ref[...] = jnp.full_like(m_ref, NEG)
    al_ref[...] = jnp.ones_like(al_ref)
    p1_ref[...] = jnp.zeros_like(p1_ref)
    scores(0, s0_ref, False)

    n_pairs = (b0 + nq - 1) // (2 * KT // QB) + 1
    n_far = jnp.maximum((b0 - (N_TOK_BIAS - 2)) // sub, 0)
    n_far_pairs = jnp.maximum((n_far - 1) // 2, 0)

    def pair(far):
        def body(i, carry):
            step(2 * i, s0_ref, p0_ref, s1_ref, p1_ref, far)
            step(2 * i + 1, s1_ref, p1_ref, s0_ref, p0_ref, far)
            return carry
        return body

    lax.fori_loop(0, n_far_pairs, pair(True), 0)
    lax.fori_loop(n_far_pairs, n_pairs, pair(False), 0)
    acc = al_ref[...] * acc_ref[...] + _dot(values(2 * n_pairs - 1), p1_ref[...])

    ost = acc[0:HD] / acc[HD:HD + 1]
    for h in range(nq):
        qs = slice(h * QB, (h + 1) * QB)
        for g in range(GROUP):
            ls = slice(h * gl + g * LANES, h * gl + (g + 1) * LANES)
            ds = slice(g * HD, (g + 1) * HD)
            o_ref[qs, ds] = ((ost[:, ls] * gate[g * 3 + 1:g * 3 + 2, qs]).T + o1_ref[qs, ds]).astype(o_ref.dtype)


def _slc_attention(projn, projt, nsel_t, bias_t, ind, o1, t, nq):
    nb = t // QB
    per = KT // QB // nq
    nsr = nsel_t.shape[1]
    lanes = nq * GROUP * LANES
    resident = dict(pipeline_mode=pl.Buffered(1))
    return pl.pallas_call(
        functools.partial(_slc_kernel, nq=nq),
        grid=(N_KV, nb // nq),
        in_specs=[pl.BlockSpec((1, GROUP * HD, nq * QB), lambda hk, bq: (bq // per, hk, bq % per)),
                  pl.BlockSpec((t, HD), lambda hk, bq: (0, _C_KS + hk), **resident),
                  pl.BlockSpec((t // KT, HD, KT), lambda hk, bq: (0, _R_VS + hk, 0), **resident),
                  pl.BlockSpec((1, nsr, nq * QB), lambda hk, bq: (hk, 0, bq)),
                  pl.BlockSpec((1, N_TOK_BIAS + 1, LANES, GROUP * LANES), lambda hk, bq: (hk, 0, 0, 0), **resident),
                  pl.BlockSpec(ind.shape, lambda hk, bq: (0, 0), **resident),
                  pl.BlockSpec((1, 16, nq * QB), lambda hk, bq: (bq // per, _R_G * 8 + hk, bq % per)),
                  pl.BlockSpec((nq * QB, GROUP * HD), lambda hk, bq: (bq, hk))],
        out_specs=pl.BlockSpec((nq * QB, GROUP * HD), lambda hk, bq: (bq, hk)),
        out_shape=jax.ShapeDtypeStruct((t, N_HEADS * HD), bf16),
        scratch_shapes=[pltpu.VMEM((nsr // SEL_CHUNK, 2 * HD, lanes), bf16),
                        pltpu.VMEM((KT, lanes), f32),
                        pltpu.VMEM((KT, lanes), f32),
                        pltpu.VMEM((KT, lanes), bf16),
                        pltpu.VMEM((KT, lanes), bf16),
                        pltpu.VMEM((HD + 8, lanes), f32)]
                       + [pltpu.VMEM((1, lanes), f32)] * 4,
        compiler_params=_params(("parallel", "arbitrary")),
        name="slc_attention",
    )(projt, projn, projt, nsel_t, bias_t, ind, projt, o1)


def _layer_norm(z, g, b):
    mu = jnp.mean(z, axis=-1, keepdims=True)
    var = jnp.mean(jnp.square(z - mu), axis=-1, keepdims=True)
    return (z - mu) * lax.rsqrt(var + LN_EPS) * g + b


def _post_kernel(o2_ref, ya_ref, mb_ref, x_ref, p_ref, wao_ref, wmix_ref, wple_ref, wpg_ref,
                 g_ref, b_ref, x1_ref, r2_ref, *, alpha):
    yb = _dot(o2_ref[...], wao_ref[...])
    mixed = ya_ref[...] + jax.nn.sigmoid(mb_ref[...].astype(f32)) * yb
    z = alpha * x_ref[...] + _dot(mixed.astype(bf16), wmix_ref[...])
    x1 = _layer_norm(z, g_ref[...], b_ref[...])
    x1b = x1.astype(bf16)
    x1_ref[...] = x1b
    ple = _dot(p_ref[...].astype(bf16), wple_ref[...]) * jax.nn.sigmoid(_dot(x1b, wpg_ref[...]))
    r2_ref[...] = alpha * x1 + ple


def _post_attention(o2, ya, mab, x2d, p2d, wao, wmix, wple, wpg, g1, b1, alpha, tm):
    t = x2d.shape[0]
    row = lambda w: pl.BlockSpec((tm, w), lambda i: (i, 0))
    full = lambda a: pl.BlockSpec(a.shape, lambda i: (0, 0), pipeline_mode=pl.Buffered(1))
    return pl.pallas_call(
        functools.partial(_post_kernel, alpha=alpha),
        grid=(t // tm,),
        in_specs=[row(D_MODEL), row(D_MODEL),
                  pl.BlockSpec((tm, D_MODEL), lambda i: (i, 1)),
                  row(D_MODEL), row(PLE_DIM),
                  full(wao), full(wmix), full(wple), full(wpg), full(g1), full(b1)],
        out_specs=[row(D_MODEL), row(D_MODEL)],
        out_shape=[jax.ShapeDtypeStruct((t, D_MODEL), bf16),
                   jax.ShapeDtypeStruct((t, D_MODEL), f32)],
        compiler_params=_params(("parallel",)),
        name="post_attention",
    )(o2, ya, mab, x2d, p2d, wao, wmix, wple, wpg, g1, b1)


def _mlp_kernel(x1_ref, r2_ref, wu_ref, wd_ref, g_ref, b_ref, o_ref):
    k = pl.program_id(1)

    @pl.when(k == 0)
    def _():
        o_ref[...] = r2_ref[...]

    h = jnp.square(jnp.maximum(_dot(x1_ref[...], wu_ref[...]), 0.0))
    o_ref[...] += _dot(h.astype(bf16), wd_ref[...])

    @pl.when(k == pl.num_programs(1) - 1)
    def _():
        o_ref[...] = _layer_norm(o_ref[...], g_ref[...], b_ref[...])


def _mlp(x1b, r2, wu, wd, g2, b2, tm, fc):
    t = x1b.shape[0]
    return pl.pallas_call(
        _mlp_kernel,
        grid=(t // tm, D_FF // fc),
        in_specs=[pl.BlockSpec((tm, D_MODEL), lambda i, k: (i, 0)),
                  pl.BlockSpec((tm, D_MODEL), lambda i, k: (i, 0)),
                  pl.BlockSpec((D_MODEL, fc), lambda i, k: (0, k)),
                  pl.BlockSpec((fc, D_MODEL), lambda i, k: (k, 0)),
                  pl.BlockSpec((1, D_MODEL), lambda i, k: (0, 0)),
                  pl.BlockSpec((1, D_MODEL), lambda i, k: (0, 0))],
        out_specs=pl.BlockSpec((tm, D_MODEL), lambda i, k: (i, 0)),
        out_shape=jax.ShapeDtypeStruct((t, D_MODEL), f32),
        compiler_params=_params(("parallel", "arbitrary")),
        name="mlp",
    )(x1b, r2, wu, wd, g2, b2)


def _layer(x2d, p2d, w_in, conv_w, pe_k, w1_k, w2_k, pe_v, w1_v, w2_v, w_conv_out, w_attn_out,
           w_mix_out, ln1_g, ln1_b, w_mlp_up, w_mlp_down, w_ple, w_ple_gate, ln2_g, ln2_b,
           rel_bias, alpha):
    t = x2d.shape[0]
    assert t % 1024 == 0 and x2d.shape[1] == D_MODEL
    kvw = N_KV * HD
    o_q = 3 * D_MODEL
    o_kc = o_q + N_HEADS * HD
    o_ng = o_kc + 6 * kvw
    o_ma = o_ng + N_HEADS * 3

    def cols(a, n):
        return w_in[:, a:a + n]

    w_bf = w_in.astype(bf16)
    tn = kvw
    conv_tiles = 3 * D_MODEL // tn
    ks_tile, kw_tile = (o_kc + 2 * kvw) // tn, (o_kc + 4 * kvw) // tn
    assert _C_KS * LANES == conv_tiles * tn and _C_KW * LANES == (conv_tiles + 1) * tn

    def nat_tile(j):
        return jnp.where(j < conv_tiles, j, jnp.where(j == conv_tiles, ks_tile, kw_tile))

    wng = jnp.pad(cols(o_ng, N_HEADS * 3).reshape(D_MODEL, N_KV, GROUP * 3), ((0, 0), (0, 0), (0, 16 - GROUP * 3)))
    w_t = jnp.concatenate(
        [cols(o_q, N_HEADS * HD) * (HD ** -0.5 * LOG2E), cols(o_kc + 3 * kvw, kvw),
         cols(o_kc + 5 * kvw, kvw), wng.reshape(D_MODEL, N_KV * 16),
         jnp.zeros((D_MODEL, _N_T - _R_G * LANES - N_KV * 16), f32)], axis=1).T.astype(bf16)

    xb = x2d.astype(bf16)
    projn = _proj_nat(xb, w_bf, 1024, tn, "proj_nat", _N_NAT, nat_tile)
    mab = _proj_nat(xb, w_bf[:, o_ma:o_ma + 2 * D_MODEL], 1024, 1024, "proj_gate")
    kvc = _proj_nat(xb, w_bf, 1024, 2 * kvw, "proj_cmp", 2 * kvw, lambda j: o_kc // (2 * kvw))
    projt = _proj_t(w_t, xb, 640, KT)

    ya = _mixer_a(projn, mab, conv_w, w_conv_out.astype(bf16), 512)

    half = CMP_STRIDE * HD
    w1ab = jnp.stack([jnp.concatenate([w[:half], w[half:]], axis=1) for w in (w1_k, w1_v)]).astype(bf16)
    pe2 = jnp.stack([jnp.stack([pe[:CMP_STRIDE], pe[CMP_STRIDE:]], axis=1) for pe in (pe_k, pe_v)])
    pe2 = jnp.pad(pe2, ((0, 0), (0, 0), (0, 6), (0, 0)))
    w2 = jnp.stack([w2_k, w2_v]).astype(bf16)
    cn, ct = _compress(kvc, pe2, w1ab, w2, jnp.swapaxes(w2, 1, 2))

    bias_t = _bias_tiles(rel_bias, N_TOK_BIAS, 1, 0, WINDOW // QB, WINDOW - 1, "bias_tok")
    bias_c = _bias_tiles(rel_bias, N_CMP_BIAS, CMP_STRIDE, CMP_LEN - 1, 0, -1, "bias_cmp")

    ncp = t // CMP_STRIDE
    nslc = t // SLC_LEN
    nsr = -(-nslc // SEL_CHUNK) * SEL_CHUNK
    ci = jnp.arange(ncp)[None, :] * CMP_STRIDE
    sj = jnp.arange(nslc)[:, None] * SLC_LEN
    ov_t = ((ci < sj + SLC_LEN) & (ci + CMP_LEN > sj)).astype(bf16)
    pos = jnp.arange(min(t, SEL_CHUNK * SLC_LEN))[:, None] // SLC_LEN
    ind = jnp.where(pos == jnp.arange(SEL_CHUNK)[None, :], SEL_BIG, 0.0).astype(bf16)

    o1, nsel_t = _cmp_attention(projn, projt, cn, ct, bias_c, bias_t, ov_t, t, nsr)
    o2 = _slc_attention(projn, projt, nsel_t, bias_t, ind, o1, t, SLC_NQ)

    x1b, r2 = _post_attention(o2, ya, mab, x2d, p2d, w_attn_out.astype(bf16), w_mix_out.astype(bf16),
                              w_ple.astype(bf16), w_ple_gate.astype(bf16),
                              ln1_g[None, :], ln1_b[None, :], alpha, 256)
    return _mlp(x1b, r2, w_mlp_up.astype(bf16), w_mlp_down.astype(bf16), ln2_g[None, :], ln2_b[None, :], 512, 512)


def kernel(x, p, w_in, conv_w, cmp_pe_k, cmp_w1_k, cmp_w2_k, cmp_pe_v, cmp_w1_v, cmp_w2_v, w_conv_out, w_attn_out, w_mix_out, ln1_g, ln1_b, w_mlp_up, w_mlp_down, w_ple, w_ple_gate, ln2_g, ln2_b, rel_bias):
    bsz, t, d = x.shape
    assert bsz == 1
    depth = w_in.shape[0]
    alpha = (2 * depth) ** 0.25
    x2d = x[0]
    for i in range(depth):
        x2d = _layer(x2d, p[i, 0], w_in[i], conv_w[i], cmp_pe_k[i], cmp_w1_k[i], cmp_w2_k[i],
                     cmp_pe_v[i], cmp_w1_v[i], cmp_w2_v[i], w_conv_out[i], w_attn_out[i], w_mix_out[i],
                     ln1_g[i], ln1_b[i], w_mlp_up[i], w_mlp_down[i], w_ple[i], w_ple_gate[i],
                     ln2_g[i], ln2_b[i], rel_bias, alpha)
    return x2d[None]
```

```python
import functools
import math

import jax
import jax.numpy as jnp
from jax import lax
from jax.experimental import pallas as pl
from jax.experimental.pallas import tpu as pltpu

f32 = jnp.float32
bf16 = jnp.bfloat16
i32 = jnp.int32

D_MODEL = 2048
N_HEADS = 16
N_KV = 4
GROUP = N_HEADS // N_KV
HD = 128
CMP_LEN = 32
CMP_STRIDE = 16
CMP_HIDDEN = 2 * HD
SLC_LEN = 64
SLC_TOPN = 16
WINDOW = 512
QB = 128
D_FF = 4 * D_MODEL
PLE_DIM = 256
REL_BUCKETS = 32
REL_EXACT = REL_BUCKETS // 2
REL_MAX_DIST = 4096
LN_EPS = 1e-5
NEG = -1e30
FORCE_SCORE = 1e9
LOG2E = 1.4426950408889634

LANES = 128
KT = 512
SEL_BIG = 65536.0
SEL_CHUNK = 128
SLC_NQ = 1
CMP_NQ = 2
N_TOK_BIAS = 25
N_CMP_BIAS = 40
VMEM_LIMIT = 56 * 1024 * 1024

_C_BG, _C_CG, _C_HX = 0, 16, 32
_C_KS, _C_KW = 48, 52
_N_NAT = 56 * LANES
_R_Q, _R_VS, _R_VW, _R_G = 0, 16, 20, 24
_N_T = 25 * LANES


def _dot(a, b):
    return jnp.dot(a, b, preferred_element_type=f32)


def _dot_nt(a, b):
    return lax.dot_general(a, b, (((1,), (1,)), ((), ())), preferred_element_type=f32)


def _params(sem, vmem=VMEM_LIMIT):
    return pltpu.CompilerParams(dimension_semantics=sem, vmem_limit_bytes=vmem)


def _mm_kernel(x_ref, w_ref, o_ref):
    o_ref[...] = _dot(x_ref[...], w_ref[...]).astype(o_ref.dtype)


def _proj_nat(xb, w, tm, tn, name, n=None, col_tile=lambda j: j):
    m, k = xb.shape
    n = w.shape[1] if n is None else n
    return pl.pallas_call(
        _mm_kernel,
        grid=(m // tm, n // tn),
        in_specs=[pl.BlockSpec((tm, k), lambda i, j: (i, 0)),
                  pl.BlockSpec((k, tn), lambda i, j: (0, col_tile(j)))],
        out_specs=pl.BlockSpec((tm, tn), lambda i, j: (i, j)),
        out_shape=jax.ShapeDtypeStruct((m, n), bf16),
        compiler_params=_params(("parallel", "arbitrary")),
        name=name,
    )(xb, w)


def _mmt_kernel(w_ref, x_ref, o_ref):
    o_ref[0] = lax.dot_general(w_ref[...], x_ref[...], (((0,), (1,)), ((), ())),
                               preferred_element_type=f32).astype(o_ref.dtype)


def _proj_t(w, xb, tm, tn):
    k, n = w.shape
    t = xb.shape[0]
    return pl.pallas_call(
        _mmt_kernel,
        grid=(n // tm, t // tn),
        in_specs=[pl.BlockSpec((k, tm), lambda i, j: (0, i)),
                  pl.BlockSpec((tn, k), lambda i, j: (j, 0))],
        out_specs=pl.BlockSpec((1, tm, tn), lambda i, j: (j, i, 0)),
        out_shape=jax.ShapeDtypeStruct((t // tn, n, tn), bf16),
        compiler_params=_params(("parallel", "arbitrary")),
        name="proj_t",
    )(w, xb)


def _ya_kernel(bg_ref, cg_ref, hx_ref, cgh_ref, hxh_ref, cw_ref, w_ref, ma_ref, o_ref, a_ref):
    i = pl.program_id(0)
    tm = bg_ref.shape[0]
    cc = 512
    row = lax.broadcasted_iota(i32, (tm, cc), 0)
    for c in range(D_MODEL // cc):
        sl = slice(c * cc, (c + 1) * cc)
        u = cg_ref[:, sl].astype(f32) * hx_ref[:, sl].astype(f32)
        uh = cgh_ref[:, sl].astype(f32) * hxh_ref[:, sl].astype(f32)
        uh = jnp.where(i > 0, uh, 0.0)
        u1 = jnp.where(row == 0, uh[15:16, :], pltpu.roll(u, 1, 0))
        u2 = jnp.where(row == 0, uh[14:15, :], jnp.where(row == 1, uh[15:16, :], pltpu.roll(u, 2, 0)))
        cw = cw_ref[:, sl]
        conv = cw[0:1, :] * u2 + cw[1:2, :] * u1 + cw[2:3, :] * u
        a_ref[:, sl] = (bg_ref[:, sl].astype(f32) * conv).astype(bf16)
    y = _dot(a_ref[...], w_ref[...])
    o_ref[...] = jax.nn.sigmoid(ma_ref[...].astype(f32)) * y


def _mixer_a(projn, mab, conv_w, w_out, tm):
    t = projn.shape[0]
    hb = tm // 16
    wide = lambda c: pl.BlockSpec((tm, D_MODEL), lambda i, c=c: (i, c))
    halo = lambda c: pl.BlockSpec((16, D_MODEL), lambda i, c=c: (jnp.maximum(i * hb - 1, 0), c))
    return pl.pallas_call(
        _ya_kernel,
        grid=(t // tm,),
        in_specs=[wide(0), wide(1), wide(2), halo(1), halo(2),
                  pl.BlockSpec((3, D_MODEL), lambda i: (0, 0)),
                  pl.BlockSpec((D_MODEL, D_MODEL), lambda i: (0, 0)),
                  wide(0)],
        out_specs=pl.BlockSpec((tm, D_MODEL), lambda i: (i, 0)),
        out_shape=jax.ShapeDtypeStruct((t, D_MODEL), f32),
        scratch_shapes=[pltpu.VMEM((tm, D_MODEL), bf16)],
        compiler_params=_params(("parallel",)),
        name="mixer_a",
    )(projn, projn, projn, projn, projn, conv_w, w_out, mab)


def _compress_kernel(x_ref, w_ref, pe_ref, w2_ref, w2t_ref, on_ref, ot_ref, acc_ref, pet_ref):
    l = pl.program_id(1)
    nl = pl.num_programs(1)
    n = x_ref.shape[0]

    @pl.when(l == 0)
    def _():
        acc_ref[...] = jnp.zeros_like(acc_ref)
        pet_ref[...] = jnp.zeros_like(pet_ref)

    wblk = w_ref[0]
    for h in range(N_KV):
        acc_ref[h] += _dot(x_ref[:, h * HD:(h + 1) * HD], wblk)
    pet_ref[...] += _dot(pe_ref[0, 0].astype(bf16), wblk)

    @pl.when(l == nl - 1)
    def _():
        r = pet_ref[...]
        peterm = r[0:1, :CMP_HIDDEN] + r[1:2, CMP_HIDDEN:]
        for h in range(N_KV):
            ab = acc_ref[h]
            hid = ab[:, :CMP_HIDDEN] + pltpu.roll(ab[:, CMP_HIDDEN:], n - 1, 0) + peterm
            gl = jax.nn.gelu(hid).astype(bf16)
            on_ref[0, h] = _dot(gl, w2_ref[0]).astype(bf16)
            ot_ref[0, h] = _dot_nt(w2t_ref[0], gl).astype(bf16)


def _compress(kvc, pe2, w1ab, w2, w2t):
    t = kvc.shape[0]
    n = t // CMP_STRIDE
    xr = kvc.reshape(n, CMP_STRIDE * 2 * N_KV * HD)
    return pl.pallas_call(
        _compress_kernel,
        grid=(2, CMP_STRIDE),
        in_specs=[pl.BlockSpec((n, N_KV * HD), lambda kv, l: (0, 2 * l + kv)),
                  pl.BlockSpec((1, HD, 2 * CMP_HIDDEN), lambda kv, l: (kv, l, 0)),
                  pl.BlockSpec((1, 1, 8, HD), lambda kv, l: (kv, l, 0, 0)),
                  pl.BlockSpec((1, CMP_HIDDEN, HD), lambda kv, l: (kv, 0, 0)),
                  pl.BlockSpec((1, HD, CMP_HIDDEN), lambda kv, l: (kv, 0, 0))],
        out_specs=[pl.BlockSpec((1, N_KV, n, HD), lambda kv, l: (kv, 0, 0, 0)),
                   pl.BlockSpec((1, N_KV, HD, n), lambda kv, l: (kv, 0, 0, 0))],
        out_shape=[jax.ShapeDtypeStruct((2, N_KV, n, HD), bf16),
                   jax.ShapeDtypeStruct((2, N_KV, HD, n), bf16)],
        scratch_shapes=[pltpu.VMEM((N_KV, n, 2 * CMP_HIDDEN), f32),
                        pltpu.VMEM((8, 2 * CMP_HIDDEN), f32)],
        compiler_params=_params(("arbitrary", "arbitrary")),
        name="compress",
    )(xr, w1ab, pe2, w2, w2t)


def _rel_bucket(dist):
    n = jnp.maximum(dist, 0)
    nf = jnp.maximum(n, 1).astype(f32)
    large = REL_EXACT + (jnp.log(nf / REL_EXACT) / math.log(REL_MAX_DIST / REL_EXACT)
                         * (REL_BUCKETS - REL_EXACT)).astype(i32)
    large = jnp.minimum(large, REL_BUCKETS - 1)
    return jnp.where(n < REL_EXACT, n, large)


def _bias_kernel(tbl_ref, o_ref, *, kstride, koff, n_reg, edge_o, edge_max):
    hk = pl.program_id(0)
    o = pl.program_id(1)
    is_edge = o == n_reg
    ki = lax.broadcasted_iota(i32, (LANES, LANES), 0)
    qi = lax.broadcasted_iota(i32, (LANES, LANES), 1)
    dist = jnp.where(is_edge, edge_o, o) * LANES + qi - kstride * ki - koff
    valid = (dist >= 0) & (dist <= jnp.where(is_edge, edge_max, 2 ** 30))
    bucket = _rel_bucket(dist)
    for g in range(GROUP):
        h = hk * GROUP + g
        acc = jnp.zeros((LANES, LANES), f32)
        for k in range(REL_BUCKETS):
            acc = jnp.where(bucket == k, tbl_ref[k, h], acc)
        o_ref[0, 0, :, g * LANES:(g + 1) * LANES] = jnp.where(valid, acc * LOG2E, NEG)


def _bias_tiles(rel_bias, n_reg, kstride, koff, edge_o, edge_max, name):
    return pl.pallas_call(
        functools.partial(_bias_kernel, kstride=kstride, koff=koff, n_reg=n_reg, edge_o=edge_o,
                          edge_max=edge_max),
        grid=(N_KV, n_reg + 1),
        in_specs=[pl.BlockSpec(memory_space=pltpu.SMEM)],
        out_specs=pl.BlockSpec((1, 1, LANES, GROUP * LANES), lambda hk, o: (hk, o, 0, 0)),
        out_shape=jax.ShapeDtypeStruct((N_KV, n_reg + 1, LANES, GROUP * LANES), f32),
        compiler_params=_params(("parallel", "arbitrary")),
        name=name,
    )(rel_bias)


def _query_lanes(q_ref, nq):
    qb = q_ref[0]
    return jnp.concatenate([qb[g * HD:(g + 1) * HD, h * QB:(h + 1) * QB]
                            for h in range(nq) for g in range(GROUP)], axis=1)


def _cmp_kernel(q_ref, kc_ref, vct_ref, bc_ref, ov_ref, g_ref, *rest, n_sel, n_steps, n_branch, nq):
    nw = nq + WINDOW // QB
    kws, vws = rest[:nw], rest[nw:2 * nw]
    bt_ref, o_ref, ns_ref = rest[2 * nw:]
    step_id = pl.program_id(1)
    b0 = step_id * nq
    nsr = ns_ref.shape[1]
    gl = GROUP * LANES
    qt = _query_lanes(q_ref, nq)
    gate = jax.nn.sigmoid(g_ref[0].astype(f32))
    cmp_per_q = QB // CMP_STRIDE
    q_per_tile = LANES // cmp_per_q

    def core(nt, rows):
        keys = nt * LANES
        s = _dot(kc_ref[0, 0, 0:keys, :], qt)
        col_parts = []
        for h in range(nq):
            parts = []
            for c in range(nt):
                o = b0 + h - q_per_tile * c
                idx = jnp.where(o < 0, N_CMP_BIAS, jnp.minimum(o, N_CMP_BIAS - 1))
                parts.append(s[c * LANES:(c + 1) * LANES, h * gl:(h + 1) * gl] + bc_ref[0, idx])
            col_parts.append(jnp.concatenate(parts, axis=0) if nt > 1 else parts[0])
        s = jnp.concatenate(col_parts, axis=1) if nq > 1 else col_parts[0]
        m = jnp.max(s, axis=0, keepdims=True)
        e = jnp.exp2(s - m)
        vals = jnp.concatenate([vct_ref[0, 0, :, 0:keys], jnp.ones((8, keys), bf16)], axis=0)
        acc = _dot(vals, e.astype(bf16))
        inv = jnp.where(m > 0.5 * NEG, 1.0 / acc[HD:HD + 1], 0.0)
        oct_ = acc[0:HD] * inv

        sw = _dot(jnp.concatenate([r[...] for r in kws], axis=0), qt)
        col_parts = []
        for h in range(nq):
            parts = []
            for jw in range(nw):
                dl = h + WINDOW // QB - jw
                if dl < 0 or dl > WINDOW // QB:
                    parts.append(jnp.full((LANES, gl), NEG, f32))
                    continue
                tile = N_TOK_BIAS if dl == WINDOW // QB else dl
                before_start = jnp.where(b0 + h - dl >= 0, 0.0, NEG)
                parts.append(sw[jw * LANES:(jw + 1) * LANES, h * gl:(h + 1) * gl] + bt_ref[0, tile] + before_start)
            col_parts.append(jnp.concatenate(parts, axis=0))
        sw = jnp.concatenate(col_parts, axis=1) if nq > 1 else col_parts[0]
        pw = jnp.exp2(sw - jnp.max(sw, axis=0, keepdims=True)).astype(bf16)
        vw = jnp.concatenate([r[0] for r in vws], axis=1)
        accw = _dot(jnp.concatenate([vw, jnp.ones((8, nw * LANES), bf16)], axis=0), pw)
        owt = accw[0:HD] / accw[HD:HD + 1]
        for h in range(nq):
            qs = slice(h * QB, (h + 1) * QB)
            for g in range(GROUP):
                ls = slice(h * gl + g * LANES, h * gl + (g + 1) * LANES)
                both = oct_[:, ls] * gate[g * 3:g * 3 + 1, qs] + owt[:, ls] * gate[g * 3 + 2:g * 3 + 3, qs]
                o_ref[qs, g * HD:(g + 1) * HD] = both.T.astype(o_ref.dtype)

        pgs = []
        for h in range(nq):
            pg = None
            for g in range(GROUP):
                ls = slice(h * gl + g * LANES, h * gl + (g + 1) * LANES)
                pgg = e[:, ls] * inv[:, ls]
                pg = pgg if pg is None else pg + pgg
            pgs.append(pg)
        pg = jnp.concatenate(pgs, axis=1) if nq > 1 else pgs[0]
        hi = pg.astype(bf16)
        lo = (pg - hi.astype(f32)).astype(bf16)
        ov = ov_ref[0:rows, 0:keys]
        imp = _dot(ov, hi) + _dot(ov, lo)

        j = lax.broadcasted_iota(i32, (rows, nq * QB), 0)
        tq = b0 * QB + lax.broadcasted_iota(i32, (rows, nq * QB), 1)
        jt = jnp.right_shift(tq, SLC_LEN.bit_length() - 1)
        forced = (j == 0) | (j == jt) | (j == jt - 1)
        future = j * SLC_LEN > tq
        work = jnp.where(forced, -3e38, jnp.where(future, -1.0, imp))
        jf = j.astype(f32)
        sel = jnp.where(forced, 1.0, 0.0)
        for _ in range(n_sel - 3):
            mx = jnp.max(work, axis=0, keepdims=True)
            jm = jnp.min(jnp.where(work == mx, jf, 1e9), axis=0, keepdims=True)
            pick = jf == jm
            sel = jnp.where(pick, 1.0, sel)
            work = jnp.where(pick, -3e38, work)
        ns_ref[0, 0:rows, :] = jnp.where(future, 1.0, 1.0 - sel).astype(bf16)
        if nsr > rows:
            ns_ref[0, rows:, :] = jnp.ones((nsr - rows, nq * QB), bf16)

    per_branch = n_steps // n_branch
    for k in range(n_branch):
        last_b = (k + 1) * per_branch * nq - 1
        nt = last_b // q_per_tile + 1
        rows = (last_b + 1) * (QB // SLC_LEN)
        pl.when(step_id // per_branch == k)(functools.partial(core, nt, rows))


def _cmp_attention(projn, projt, cn, ct, bias_c, bias_t, ov_t, t, nsr, nq):
    nb = t // QB
    ncp = t // CMP_STRIDE
    nslc = t // SLC_LEN
    per_kt = KT // QB
    per = per_kt // nq
    nw = nq + WINDOW // QB
    n_sel = min(SLC_TOPN, nslc)
    assert n_sel > 3
    n_steps = nb // nq
    n_branch = 4 if nb % 64 == 0 else 1
    resident = dict(pipeline_mode=pl.Buffered(1))

    def kw_spec(jw):
        return pl.BlockSpec((QB, HD), lambda hk, s, jw=jw: (jnp.maximum(s * nq - WINDOW // QB + jw, 0), _C_KW + hk))

    def vw_spec(jw):
        def imap(hk, s, jw=jw):
            bb = jnp.maximum(s * nq - WINDOW // QB + jw, 0)
            return (bb // per_kt, _R_VW + hk, bb % per_kt)
        return pl.BlockSpec((1, HD, QB), imap)

    return pl.pallas_call(
        functools.partial(_cmp_kernel, n_sel=n_sel, n_steps=n_steps, n_branch=n_branch, nq=nq),
        grid=(N_KV, n_steps),
        in_specs=[pl.BlockSpec((1, GROUP * HD, nq * QB), lambda hk, s: (s // per, hk, s % per)),
                  pl.BlockSpec((1, 1, ncp, HD), lambda hk, s: (0, hk, 0, 0), **resident),
                  pl.BlockSpec((1, 1, HD, ncp), lambda hk, s: (1, hk, 0, 0), **resident),
                  pl.BlockSpec((1, N_CMP_BIAS + 1, LANES, GROUP * LANES), lambda hk, s: (hk, 0, 0, 0), **resident),
                  pl.BlockSpec((nslc, ncp), lambda hk, s: (0, 0), **resident),
                  pl.BlockSpec((1, 16, nq * QB), lambda hk, s: (s // per, _R_G * 8 + hk, s % per))]
                 + [kw_spec(jw) for jw in range(nw)]
                 + [vw_spec(jw) for jw in range(nw)]
                 + [pl.BlockSpec((1, N_TOK_BIAS + 1, LANES, GROUP * LANES), lambda hk, s: (hk, 0, 0, 0), **resident)],
        out_specs=[pl.BlockSpec((nq * QB, GROUP * HD), lambda hk, s: (s, hk)),
                   pl.BlockSpec((1, nsr, nq * QB), lambda hk, s: (hk, 0, s))],
        out_shape=[jax.ShapeDtypeStruct((t, N_HEADS * HD), f32),
                   jax.ShapeDtypeStruct((N_KV, nsr, t), bf16)],
        compiler_params=_params(("parallel", "arbitrary")),
        name="cmp_attention",
    )(projt, cn, ct, bias_c, ov_t, projt, *([projn] * nw), *([projt] * nw), bias_t)


def _slc_kernel(q_ref, ks_ref, vst_ref, ns_ref, bt_ref, ind_ref, g_ref, o1_ref, o_ref,
                qa_ref, s0_ref, s1_ref, p0_ref, p1_ref, acc_ref, m_ref, al_ref, mx_ref, off_ref, *, nq):
    b0 = pl.program_id(1) * nq
    gl = GROUP * LANES
    qt = _query_lanes(q_ref, nq)
    gate = jax.nn.sigmoid(g_ref[0].astype(f32))

    n_chunks = qa_ref.shape[0]
    for jc in range(n_chunks):
        nsj = -ns_ref[0, jc * SEL_CHUNK:(jc + 1) * SEL_CHUNK, :]
        qa_ref[jc, 0:HD, :] = qt
        qa_ref[jc, HD:, :] = jnp.concatenate([nsj[:, h * QB:(h + 1) * QB]
                                              for h in range(nq) for _ in range(GROUP)], axis=1)

    n_tiles = vst_ref.shape[0]
    per_chunk = SEL_CHUNK * SLC_LEN // KT
    sub = KT // LANES

    far_bias = jnp.concatenate([bt_ref[0, N_TOK_BIAS - 1, 0:1, :]] * nq, axis=1)
    ones_rows = jnp.ones((8, KT), bf16)

    def scores(kt, s_ref, far):
        jc = kt // per_chunk
        r0 = pl.multiple_of(kt * KT, KT)
        i0 = pl.multiple_of((kt - jc * per_chunk) * KT, KT)
        kaug = jnp.concatenate([ks_ref[pl.ds(r0, KT), :], ind_ref[pl.ds(i0, KT), :]], axis=1)
        s = _dot(kaug, qa_ref[jc])
        if far:
            s_ref[...] = s
            mx_ref[...] = jnp.max(s, axis=0, keepdims=True) + far_bias
            off_ref[...] = far_bias
            return
        mxs = []
        for h in range(nq):
            cols = slice(h * gl, (h + 1) * gl)
            mx = None
            for jj in range(sub):
                rows = slice(jj * LANES, (jj + 1) * LANES)
                dl = b0 + h - (kt * sub + jj)
                sj = s[rows, cols] + bt_ref[0, jnp.clip(dl, 0, N_TOK_BIAS - 1)]
                s_ref[rows, cols] = sj
                mj = jnp.max(sj, axis=0, keepdims=True)
                mx = mj if mx is None else jnp.maximum(mx, mj)
            mxs.append(mx)
        mx_ref[...] = jnp.concatenate(mxs, axis=1) if nq > 1 else mxs[0]
        off_ref[...] = jnp.zeros_like(off_ref)

    def values(kt):
        return jnp.concatenate([vst_ref[kt], ones_rows], axis=0)

    def step(kt, s_cur, p_cur, s_nxt, p_prev, far):
        m_old = m_ref[...]
        m_new = jnp.maximum(m_old, mx_ref[...])
        shift = m_new - off_ref[...]
        al_old = al_ref[...]
        scores(jnp.minimum(kt + 1, n_tiles - 1), s_nxt, far)
        acc_ref[...] = al_old * acc_ref[...] + _dot(values(jnp.maximum(kt - 1, 0)), p_prev[...])
        p_cur[...] = jnp.exp2(s_cur[...] - shift).astype(bf16)
        m_ref[...] = m_new
        al_ref[...] = jnp.exp2(m_old - m_new)

    acc_ref[...] = jnp.zeros_like(acc_ref)
    m_ref[...] = jnp.full_like(m_ref, NEG)
    al_ref[...] = jnp.ones_like(al_ref)
    p1_ref[...] = jnp.zeros_like(p1_ref)
    scores(0, s0_ref, False)

    n_pairs = (b0 + nq - 1) // (2 * KT // QB) + 1
    n_far = jnp.maximum((b0 - (N_TOK_BIAS - 2)) // sub, 0)
    n_far_pairs = jnp.maximum((n_far - 1) // 2, 0)

    def pair(far):
        def body(i, carry):
            step(2 * i, s0_ref, p0_ref, s1_ref, p1_ref, far)
            step(2 * i + 1, s1_ref, p1_ref, s0_ref, p0_ref, far)
            return carry
        return body

    lax.fori_loop(0, n_far_pairs, pair(True), 0)
    lax.fori_loop(n_far_pairs, n_pairs, pair(False), 0)
    acc = al_ref[...] * acc_ref[...] + _dot(values(2 * n_pairs - 1), p1_ref[...])

    ost = acc[0:HD] / acc[HD:HD + 1]
    for h in range(nq):
        qs = slice(h * QB, (h + 1) * QB)
        for g in range(GROUP):
            ls = slice(h * gl + g * LANES, h * gl + (g + 1) * LANES)
            ds = slice(g * HD, (g + 1) * HD)
            o_ref[qs, ds] = ((ost[:, ls] * gate[g * 3 + 1:g * 3 + 2, qs]).T + o1_ref[qs, ds]).astype(o_ref.dtype)


def _slc_attention(projn, projt, nsel_t, bias_t, ind, o1, t, nq):
    nb = t // QB
    per = KT // QB // nq
    nsr = nsel_t.shape[1]
    lanes = nq * GROUP * LANES
    resident = dict(pipeline_mode=pl.Buffered(1))
    return pl.pallas_call(
        functools.partial(_slc_kernel, nq=nq),
        grid=(N_KV, nb // nq),
        in_specs=[pl.BlockSpec((1, GROUP * HD, nq * QB), lambda hk, bq: (bq // per, hk, bq % per)),
                  pl.BlockSpec((t, HD), lambda hk, bq: (0, _C_KS + hk), **resident),
                  pl.BlockSpec((t // KT, HD, KT), lambda hk, bq: (0, _R_VS + hk, 0), **resident),
                  pl.BlockSpec((1, nsr, nq * QB), lambda hk, bq: (hk, 0, bq)),
                  pl.BlockSpec((1, N_TOK_BIAS + 1, LANES, GROUP * LANES), lambda hk, bq: (hk, 0, 0, 0), **resident),
                  pl.BlockSpec(ind.shape, lambda hk, bq: (0, 0), **resident),
                  pl.BlockSpec((1, 16, nq * QB), lambda hk, bq: (bq // per, _R_G * 8 + hk, bq % per)),
                  pl.BlockSpec((nq * QB, GROUP * HD), lambda hk, bq: (bq, hk))],
        out_specs=pl.BlockSpec((nq * QB, GROUP * HD), lambda hk, bq: (bq, hk)),
        out_shape=jax.ShapeDtypeStruct((t, N_HEADS * HD), bf16),
        scratch_shapes=[pltpu.VMEM((nsr // SEL_CHUNK, 2 * HD, lanes), bf16),
                        pltpu.VMEM((KT, lanes), f32),
                        pltpu.VMEM((KT, lanes), f32),
                        pltpu.VMEM((KT, lanes), bf16),
                        pltpu.VMEM((KT, lanes), bf16),
                        pltpu.VMEM((HD + 8, lanes), f32)]
                       + [pltpu.VMEM((1, lanes), f32)] * 4,
        compiler_params=_params(("parallel", "arbitrary")),
        name="slc_attention",
    )(projt, projn, projt, nsel_t, bias_t, ind, projt, o1)


def _layer_norm(z, g, b):
    mu = jnp.mean(z, axis=-1, keepdims=True)
    var = jnp.mean(jnp.square(z - mu), axis=-1, keepdims=True)
    return (z - mu) * lax.rsqrt(var + LN_EPS) * g + b


def _post_kernel(o2_ref, ya_ref, mb_ref, x_ref, p_ref, wao_ref, wmix_ref, wple_ref, wpg_ref,
                 g_ref, b_ref, x1_ref, r2_ref, *, alpha):
    yb = _dot(o2_ref[...], wao_ref[...])
    mixed = ya_ref[...] + jax.nn.sigmoid(mb_ref[...].astype(f32)) * yb
    z = alpha * x_ref[...] + _dot(mixed.astype(bf16), wmix_ref[...])
    x1 = _layer_norm(z, g_ref[...], b_ref[...])
    x1b = x1.astype(bf16)
    x1_ref[...] = x1b
    ple = _dot(p_ref[...].astype(bf16), wple_ref[...]) * jax.nn.sigmoid(_dot(x1b, wpg_ref[...]))
    r2_ref[...] = alpha * x1 + ple


def _post_attention(o2, ya, mab, x2d, p2d, wao, wmix, wple, wpg, g1, b1, alpha, tm):
    t = x2d.shape[0]
    row = lambda w: pl.BlockSpec((tm, w), lambda i: (i, 0))
    full = lambda a: pl.BlockSpec(a.shape, lambda i: (0, 0), pipeline_mode=pl.Buffered(1))
    return pl.pallas_call(
        functools.partial(_post_kernel, alpha=alpha),
        grid=(t // tm,),
        in_specs=[row(D_MODEL), row(D_MODEL),
                  pl.BlockSpec((tm, D_MODEL), lambda i: (i, 1)),
                  row(D_MODEL), row(PLE_DIM),
                  full(wao), full(wmix), full(wple), full(wpg), full(g1), full(b1)],
        out_specs=[row(D_MODEL), row(D_MODEL)],
        out_shape=[jax.ShapeDtypeStruct((t, D_MODEL), bf16),
                   jax.ShapeDtypeStruct((t, D_MODEL), f32)],
        compiler_params=_params(("parallel",)),
        name="post_attention",
    )(o2, ya, mab, x2d, p2d, wao, wmix, wple, wpg, g1, b1)


def _mlp_kernel(x1_ref, r2_ref, wu_ref, wd_ref, g_ref, b_ref, o_ref):
    k = pl.program_id(1)

    @pl.when(k == 0)
    def _():
        o_ref[...] = r2_ref[...]

    h = jnp.square(jnp.maximum(_dot(x1_ref[...], wu_ref[...]), 0.0))
    o_ref[...] += _dot(h.astype(bf16), wd_ref[...])

    @pl.when(k == pl.num_programs(1) - 1)
    def _():
        o_ref[...] = _layer_norm(o_ref[...], g_ref[...], b_ref[...])


def _mlp(x1b, r2, wu, wd, g2, b2, tm, fc):
    t = x1b.shape[0]
    return pl.pallas_call(
        _mlp_kernel,
        grid=(t // tm, D_FF // fc),
        in_specs=[pl.BlockSpec((tm, D_MODEL), lambda i, k: (i, 0)),
                  pl.BlockSpec((tm, D_MODEL), lambda i, k: (i, 0)),
                  pl.BlockSpec((D_MODEL, fc), lambda i, k: (0, k)),
                  pl.BlockSpec((fc, D_MODEL), lambda i, k: (k, 0)),
                  pl.BlockSpec((1, D_MODEL), lambda i, k: (0, 0)),
                  pl.BlockSpec((1, D_MODEL), lambda i, k: (0, 0))],
        out_specs=pl.BlockSpec((tm, D_MODEL), lambda i, k: (i, 0)),
        out_shape=jax.ShapeDtypeStruct((t, D_MODEL), f32),
        compiler_params=_params(("parallel", "arbitrary")),
        name="mlp",
    )(x1b, r2, wu, wd, g2, b2)


def _layer(x2d, p2d, w_in, conv_w, pe_k, w1_k, w2_k, pe_v, w1_v, w2_v, w_conv_out, w_attn_out,
           w_mix_out, ln1_g, ln1_b, w_mlp_up, w_mlp_down, w_ple, w_ple_gate, ln2_g, ln2_b,
           rel_bias, alpha):
    t = x2d.shape[0]
    assert t % 1024 == 0 and x2d.shape[1] == D_MODEL
    kvw = N_KV * HD
    o_q = 3 * D_MODEL
    o_kc = o_q + N_HEADS * HD
    o_ng = o_kc + 6 * kvw
    o_ma = o_ng + N_HEADS * 3

    def cols(a, n):
        return w_in[:, a:a + n]

    w_bf = w_in.astype(bf16)
    tn = kvw
    conv_tiles = 3 * D_MODEL // tn
    ks_tile, kw_tile = (o_kc + 2 * kvw) // tn, (o_kc + 4 * kvw) // tn
    assert _C_KS * LANES == conv_tiles * tn and _C_KW * LANES == (conv_tiles + 1) * tn

    def nat_tile(j):
        return jnp.where(j < conv_tiles, j, jnp.where(j == conv_tiles, ks_tile, kw_tile))

    wng = jnp.pad(cols(o_ng, N_HEADS * 3).reshape(D_MODEL, N_KV, GROUP * 3), ((0, 0), (0, 0), (0, 16 - GROUP * 3)))
    w_t = jnp.concatenate(
        [cols(o_q, N_HEADS * HD) * (HD ** -0.5 * LOG2E), cols(o_kc + 3 * kvw, kvw),
         cols(o_kc + 5 * kvw, kvw), wng.reshape(D_MODEL, N_KV * 16),
         jnp.zeros((D_MODEL, _N_T - _R_G * LANES - N_KV * 16), f32)], axis=1).astype(bf16)

    xb = x2d.astype(bf16)
    projn = _proj_nat(xb, w_bf, 1024, tn, "proj_nat", _N_NAT, nat_tile)
    mab = _proj_nat(xb, w_bf[:, o_ma:o_ma + 2 * D_MODEL], 1024, 1024, "proj_gate")
    kvc = _proj_nat(xb, w_bf, 1024, 2 * kvw, "proj_cmp", 2 * kvw, lambda j: o_kc // (2 * kvw))
    projt = _proj_t(w_t, xb, 640, KT)

    ya = _mixer_a(projn, mab, conv_w, w_conv_out.astype(bf16), 512)

    half = CMP_STRIDE * HD
    w1ab = jnp.stack([jnp.concatenate([w[:half], w[half:]], axis=1) for w in (w1_k, w1_v)]).astype(bf16)
    pe2 = jnp.stack([jnp.stack([pe[:CMP_STRIDE], pe[CMP_STRIDE:]], axis=1) for pe in (pe_k, pe_v)])
    pe2 = jnp.pad(pe2, ((0, 0), (0, 0), (0, 6), (0, 0)))
    w2 = jnp.stack([w2_k, w2_v]).astype(bf16)
    cn, ct = _compress(kvc, pe2, w1ab, w2, jnp.swapaxes(w2, 1, 2))

    bias_t = _bias_tiles(rel_bias, N_TOK_BIAS, 1, 0, WINDOW // QB, WINDOW - 1, "bias_tok")
    bias_c = _bias_tiles(rel_bias, N_CMP_BIAS, CMP_STRIDE, CMP_LEN - 1, 0, -1, "bias_cmp")

    ncp = t // CMP_STRIDE
    nslc = t // SLC_LEN
    nsr = -(-nslc // SEL_CHUNK) * SEL_CHUNK
    ci = jnp.arange(ncp)[None, :] * CMP_STRIDE
    sj = jnp.arange(nslc)[:, None] * SLC_LEN
    ov_t = ((ci < sj + SLC_LEN) & (ci + CMP_LEN > sj)).astype(bf16)
    pos = jnp.arange(min(t, SEL_CHUNK * SLC_LEN))[:, None] // SLC_LEN
    ind = jnp.where(pos == jnp.arange(SEL_CHUNK)[None, :], SEL_BIG, 0.0).astype(bf16)

    o1, nsel_t = _cmp_attention(projn, projt, cn, ct, bias_c, bias_t, ov_t, t, nsr, CMP_NQ)
    o2 = _slc_attention(projn, projt, nsel_t, bias_t, ind, o1, t, SLC_NQ)

    x1b, r2 = _post_attention(o2, ya, mab, x2d, p2d, w_attn_out.astype(bf16), w_mix_out.astype(bf16),
                              w_ple.astype(bf16), w_ple_gate.astype(bf16),
                              ln1_g[None, :], ln1_b[None, :], alpha, 256)
    return _mlp(x1b, r2, w_mlp_up.astype(bf16), w_mlp_down.astype(bf16), ln2_g[None, :], ln2_b[None, :], 512, 512)


def kernel(x, p, w_in, conv_w, cmp_pe_k, cmp_w1_k, cmp_w2_k, cmp_pe_v, cmp_w1_v, cmp_w2_v, w_conv_out, w_attn_out, w_mix_out, ln1_g, ln1_b, w_mlp_up, w_mlp_down, w_ple, w_ple_gate, ln2_g, ln2_b, rel_bias):
    bsz, t, d = x.shape
    assert bsz == 1
    depth = w_in.shape[0]
    alpha = (2 * depth) ** 0.25
    x2d = x[0]
    for i in range(depth):
        x2d = _layer(x2d, p[i, 0], w_in[i], conv_w[i], cmp_pe_k[i], cmp_w1_k[i], cmp_w2_k[i],
                     cmp_pe_v[i], cmp_w1_v[i], cmp_w2_v[i], w_conv_out[i], w_attn_out[i], w_mix_out[i],
                     ln1_g[i], ln1_b[i], w_mlp_up[i], w_mlp_down[i], w_ple[i], w_ple_gate[i],
                     ln2_g[i], ln2_b[i], rel_bias, alpha)
    return x2d[None]
```

```python
import functools
import math

import jax
import jax.numpy as jnp
from jax import lax
from jax.experimental import pallas as pl
from jax.experimental.pallas import tpu as pltpu

f32 = jnp.float32
bf16 = jnp.bfloat16
i32 = jnp.int32

D_MODEL = 2048
N_HEADS = 16
N_KV = 4
GROUP = N_HEADS // N_KV
HD = 128
CMP_LEN = 32
CMP_STRIDE = 16
CMP_HIDDEN = 2 * HD
SLC_LEN = 64
SLC_TOPN = 16
WINDOW = 512
QB = 128
D_FF = 4 * D_MODEL
PLE_DIM = 256
REL_BUCKETS = 32
REL_EXACT = REL_BUCKETS // 2
REL_MAX_DIST = 4096
LN_EPS = 1e-5
NEG = -1e30
FORCE_SCORE = 1e9
LOG2E = 1.4426950408889634

LANES = 128
KT = 512
SEL_BIG = 65536.0
SEL_CHUNK = 128
SLC_NQ = 1
CMP_NQ = 2
N_TOK_BIAS = 25
N_CMP_BIAS = 40
VMEM_LIMIT = 56 * 1024 * 1024

_C_BG, _C_CG, _C_HX = 0, 16, 32
_C_KS, _C_KW = 48, 52
_N_NAT = 56 * LANES
_R_Q, _R_VS, _R_VW, _R_G = 0, 16, 20, 24
_N_T = 25 * LANES


def _dot(a, b):
    return jnp.dot(a, b, preferred_element_type=f32)


def _dot_nt(a, b):
    return lax.dot_general(a, b, (((1,), (1,)), ((), ())), preferred_element_type=f32)


def _params(sem, vmem=VMEM_LIMIT):
    return pltpu.CompilerParams(dimension_semantics=sem, vmem_limit_bytes=vmem)


def _mm_kernel(x_ref, w_ref, o_ref):
    o_ref[...] = _dot(x_ref[...], w_ref[...]).astype(o_ref.dtype)


def _proj_nat(xb, w, tm, tn, name, n=None, col_tile=lambda j: j):
    m, k = xb.shape
    n = w.shape[1] if n is None else n
    return pl.pallas_call(
        _mm_kernel,
        grid=(m // tm, n // tn),
        in_specs=[pl.BlockSpec((tm, k), lambda i, j: (i, 0)),
                  pl.BlockSpec((k, tn), lambda i, j: (0, col_tile(j)))],
        out_specs=pl.BlockSpec((tm, tn), lambda i, j: (i, j)),
        out_shape=jax.ShapeDtypeStruct((m, n), bf16),
        compiler_params=_params(("parallel", "arbitrary")),
        name=name,
    )(xb, w)


def _mmt_kernel(w_ref, x_ref, s_ref, o_ref):
    acc = lax.dot_general(w_ref[...], x_ref[...], (((0,), (1,)), ((), ())), preferred_element_type=f32)
    o_ref[0] = (acc * s_ref[...]).astype(o_ref.dtype)


def _proj_t(w, xb, row_scale, tm, tn):
    k, n = w.shape
    t = xb.shape[0]
    return pl.pallas_call(
        _mmt_kernel,
        grid=(n // tm, t // tn),
        in_specs=[pl.BlockSpec((k, tm), lambda i, j: (0, i)),
                  pl.BlockSpec((tn, k), lambda i, j: (j, 0)),
                  pl.BlockSpec((tm, 1), lambda i, j: (i, 0))],
        out_specs=pl.BlockSpec((1, tm, tn), lambda i, j: (j, i, 0)),
        out_shape=jax.ShapeDtypeStruct((t // tn, n, tn), bf16),
        compiler_params=_params(("parallel", "arbitrary")),
        name="proj_t",
    )(w, xb, row_scale)


def _cast_kernel(x_ref, o_ref):
    o_ref[...] = x_ref[...].astype(o_ref.dtype)


def _to_bf16(w, tm, name):
    r, c = w.shape
    return pl.pallas_call(
        _cast_kernel,
        grid=(r // tm,),
        in_specs=[pl.BlockSpec((tm, c), lambda i: (i, 0))],
        out_specs=pl.BlockSpec((tm, c), lambda i: (i, 0)),
        out_shape=jax.ShapeDtypeStruct((r, c), bf16),
        compiler_params=_params(("parallel",)),
        name=name,
    )(w)


def _ya_kernel(bg_ref, cg_ref, hx_ref, cgh_ref, hxh_ref, cw_ref, w_ref, ma_ref, o_ref, a_ref):
    i = pl.program_id(0)
    tm = bg_ref.shape[0]
    cc = 512
    row = lax.broadcasted_iota(i32, (tm, cc), 0)
    for c in range(D_MODEL // cc):
        sl = slice(c * cc, (c + 1) * cc)
        u = cg_ref[:, sl].astype(f32) * hx_ref[:, sl].astype(f32)
        uh = cgh_ref[:, sl].astype(f32) * hxh_ref[:, sl].astype(f32)
        uh = jnp.where(i > 0, uh, 0.0)
        u1 = jnp.where(row == 0, uh[15:16, :], pltpu.roll(u, 1, 0))
        u2 = jnp.where(row == 0, uh[14:15, :], jnp.where(row == 1, uh[15:16, :], pltpu.roll(u, 2, 0)))
        cw = cw_ref[:, sl]
        conv = cw[0:1, :] * u2 + cw[1:2, :] * u1 + cw[2:3, :] * u
        a_ref[:, sl] = (bg_ref[:, sl].astype(f32) * conv).astype(bf16)
    y = _dot(a_ref[...], w_ref[...])
    o_ref[...] = jax.nn.sigmoid(ma_ref[...].astype(f32)) * y


def _mixer_a(projn, mab, conv_w, w_out, tm):
    t = projn.shape[0]
    hb = tm // 16
    wide = lambda c: pl.BlockSpec((tm, D_MODEL), lambda i, c=c: (i, c))
    halo = lambda c: pl.BlockSpec((16, D_MODEL), lambda i, c=c: (jnp.maximum(i * hb - 1, 0), c))
    return pl.pallas_call(
        _ya_kernel,
        grid=(t // tm,),
        in_specs=[wide(0), wide(1), wide(2), halo(1), halo(2),
                  pl.BlockSpec((3, D_MODEL), lambda i: (0, 0)),
                  pl.BlockSpec((D_MODEL, D_MODEL), lambda i: (0, 0)),
                  wide(0)],
        out_specs=pl.BlockSpec((tm, D_MODEL), lambda i: (i, 0)),
        out_shape=jax.ShapeDtypeStruct((t, D_MODEL), f32),
        scratch_shapes=[pltpu.VMEM((tm, D_MODEL), bf16)],
        compiler_params=_params(("parallel",)),
        name="mixer_a",
    )(projn, projn, projn, projn, projn, conv_w, w_out, mab)


def _compress_kernel(x_ref, w_ref, pe_ref, w2_ref, w2t_ref, on_ref, ot_ref, acc_ref, pet_ref):
    l = pl.program_id(1)
    nl = pl.num_programs(1)
    n = x_ref.shape[0]

    @pl.when(l == 0)
    def _():
        acc_ref[...] = jnp.zeros_like(acc_ref)
        pet_ref[...] = jnp.zeros_like(pet_ref)

    wblk = w_ref[0]
    for h in range(N_KV):
        acc_ref[h] += _dot(x_ref[:, h * HD:(h + 1) * HD], wblk)
    pet_ref[...] += _dot(pe_ref[0, 0].astype(bf16), wblk)

    @pl.when(l == nl - 1)
    def _():
        r = pet_ref[...]
        peterm = r[0:1, :CMP_HIDDEN] + r[1:2, CMP_HIDDEN:]
        for h in range(N_KV):
            ab = acc_ref[h]
            hid = ab[:, :CMP_HIDDEN] + pltpu.roll(ab[:, CMP_HIDDEN:], n - 1, 0) + peterm
            gl = jax.nn.gelu(hid).astype(bf16)
            on_ref[0, h] = _dot(gl, w2_ref[0]).astype(bf16)
            ot_ref[0, h] = _dot_nt(w2t_ref[0], gl).astype(bf16)


def _chunked_mm_kernel(x_ref, w_ref, o_ref, acc_ref):
    acc = _dot(x_ref[...], w_ref[...])
    n = w_ref.shape[1]
    rows = o_ref.shape[0]
    for c in range(n // LANES):
        acc_ref[c] = acc[:, c * LANES:(c + 1) * LANES]
    for l in range(CMP_STRIDE):
        for c in range(n // LANES):
            o_ref[:, l * n + c * LANES:l * n + (c + 1) * LANES] = (
                acc_ref[c, pl.ds(l, rows, stride=CMP_STRIDE), :].astype(o_ref.dtype))


def _proj_chunked(xb, w, tm, n, col_tile):
    m, k = xb.shape
    return pl.pallas_call(
        _chunked_mm_kernel,
        grid=(m // tm,),
        in_specs=[pl.BlockSpec((tm, k), lambda i: (i, 0)),
                  pl.BlockSpec((k, n), lambda i: (0, col_tile))],
        out_specs=pl.BlockSpec((tm // CMP_STRIDE, CMP_STRIDE * n), lambda i: (i, 0)),
        out_shape=jax.ShapeDtypeStruct((m // CMP_STRIDE, CMP_STRIDE * n), bf16),
        scratch_shapes=[pltpu.VMEM((n // LANES, tm, LANES), f32)],
        compiler_params=_params(("parallel",)),
        name="proj_cmp",
    )(xb, w)


def _compress(xr, pe2, w1ab, w2, w2t):
    n = xr.shape[0]
    return pl.pallas_call(
        _compress_kernel,
        grid=(2, CMP_STRIDE),
        in_specs=[pl.BlockSpec((n, N_KV * HD), lambda kv, l: (0, 2 * l + kv)),
                  pl.BlockSpec((1, HD, 2 * CMP_HIDDEN), lambda kv, l: (kv, l, 0)),
                  pl.BlockSpec((1, 1, 8, HD), lambda kv, l: (kv, l, 0, 0)),
                  pl.BlockSpec((1, CMP_HIDDEN, HD), lambda kv, l: (kv, 0, 0)),
                  pl.BlockSpec((1, HD, CMP_HIDDEN), lambda kv, l: (kv, 0, 0))],
        out_specs=[pl.BlockSpec((1, N_KV, n, HD), lambda kv, l: (kv, 0, 0, 0)),
                   pl.BlockSpec((1, N_KV, HD, n), lambda kv, l: (kv, 0, 0, 0))],
        out_shape=[jax.ShapeDtypeStruct((2, N_KV, n, HD), bf16),
                   jax.ShapeDtypeStruct((2, N_KV, HD, n), bf16)],
        scratch_shapes=[pltpu.VMEM((N_KV, n, 2 * CMP_HIDDEN), f32),
                        pltpu.VMEM((8, 2 * CMP_HIDDEN), f32)],
        compiler_params=_params(("arbitrary", "arbitrary")),
        name="compress",
    )(xr, w1ab, pe2, w2, w2t)


def _rel_bucket(dist):
    n = jnp.maximum(dist, 0)
    nf = jnp.maximum(n, 1).astype(f32)
    large = REL_EXACT + (jnp.log(nf / REL_EXACT) / math.log(REL_MAX_DIST / REL_EXACT)
                         * (REL_BUCKETS - REL_EXACT)).astype(i32)
    large = jnp.minimum(large, REL_BUCKETS - 1)
    return jnp.where(n < REL_EXACT, n, large)


def _bias_kernel(tbl_ref, o_ref, *, kstride, koff, n_reg, edge_o, edge_max):
    hk = pl.program_id(0)
    o = pl.program_id(1)
    is_edge = o == n_reg
    ki = lax.broadcasted_iota(i32, (LANES, LANES), 0)
    qi = lax.broadcasted_iota(i32, (LANES, LANES), 1)
    dist = jnp.where(is_edge, edge_o, o) * LANES + qi - kstride * ki - koff
    valid = (dist >= 0) & (dist <= jnp.where(is_edge, edge_max, 2 ** 30))
    bucket = _rel_bucket(dist)
    for g in range(GROUP):
        row = jnp.broadcast_to(tbl_ref[pl.ds(hk * GROUP + g, 1), :], (LANES, LANES))
        acc = jnp.take_along_axis(row, bucket, axis=1)
        o_ref[0, 0, :, g * LANES:(g + 1) * LANES] = jnp.where(valid, acc * LOG2E, NEG)


def _bias_tiles(rel_bias, n_reg, kstride, koff, edge_o, edge_max, name):
    return pl.pallas_call(
        functools.partial(_bias_kernel, kstride=kstride, koff=koff, n_reg=n_reg, edge_o=edge_o,
                          edge_max=edge_max),
        grid=(N_KV, n_reg + 1),
        in_specs=[pl.BlockSpec((N_HEADS, LANES), lambda hk, o: (0, 0))],
        out_specs=pl.BlockSpec((1, 1, LANES, GROUP * LANES), lambda hk, o: (hk, o, 0, 0)),
        out_shape=jax.ShapeDtypeStruct((N_KV, n_reg + 1, LANES, GROUP * LANES), f32),
        compiler_params=_params(("parallel", "arbitrary")),
        name=name,
    )(jnp.pad(rel_bias.astype(f32).T, ((0, 0), (0, LANES - REL_BUCKETS))))


def _query_lanes(q_ref, nq):
    qb = q_ref[0]
    return jnp.concatenate([qb[g * HD:(g + 1) * HD, h * QB:(h + 1) * QB]
                            for h in range(nq) for g in range(GROUP)], axis=1)


def _cmp_kernel(q_ref, kc_ref, vct_ref, bc_ref, ov_ref, g_ref, *rest, n_sel, n_steps, n_branch, nq):
    nw = nq + WINDOW // QB
    kws, vws = rest[:nw], rest[nw:2 * nw]
    bt_ref, o_ref, ns_ref = rest[2 * nw:]
    step_id = pl.program_id(1)
    b0 = step_id * nq
    nsr = ns_ref.shape[1]
    gl = GROUP * LANES
    qt = _query_lanes(q_ref, nq)
    gate = jax.nn.sigmoid(g_ref[0].astype(f32))
    cmp_per_q = QB // CMP_STRIDE
    q_per_tile = LANES // cmp_per_q

    def core(nt, rows):
        keys = nt * LANES
        s = _dot(kc_ref[0, 0, 0:keys, :], qt)
        col_parts = []
        for h in range(nq):
            parts = []
            for c in range(nt):
                o = b0 + h - q_per_tile * c
                idx = jnp.where(o < 0, N_CMP_BIAS, jnp.minimum(o, N_CMP_BIAS - 1))
                parts.append(s[c * LANES:(c + 1) * LANES, h * gl:(h + 1) * gl] + bc_ref[0, idx])
            col_parts.append(jnp.concatenate(parts, axis=0) if nt > 1 else parts[0])
        s = jnp.concatenate(col_parts, axis=1) if nq > 1 else col_parts[0]
        m = jnp.max(s, axis=0, keepdims=True)
        e = jnp.exp2(s - m)
        vals = jnp.concatenate([vct_ref[0, 0, :, 0:keys], jnp.ones((8, keys), bf16)], axis=0)
        acc = _dot(vals, e.astype(bf16))
        inv = jnp.where(m > 0.5 * NEG, 1.0 / acc[HD:HD + 1], 0.0)
        oct_ = acc[0:HD] * inv

        sw = _dot(jnp.concatenate([r[...] for r in kws], axis=0), qt)
        col_parts = []
        for h in range(nq):
            parts = []
            for jw in range(nw):
                dl = h + WINDOW // QB - jw
                if dl < 0 or dl > WINDOW // QB:
                    parts.append(jnp.full((LANES, gl), NEG, f32))
                    continue
                tile = N_TOK_BIAS if dl == WINDOW // QB else dl
                before_start = jnp.where(b0 + h - dl >= 0, 0.0, NEG)
                parts.append(sw[jw * LANES:(jw + 1) * LANES, h * gl:(h + 1) * gl] + bt_ref[0, tile] + before_start)
            col_parts.append(jnp.concatenate(parts, axis=0))
        sw = jnp.concatenate(col_parts, axis=1) if nq > 1 else col_parts[0]
        pw = jnp.exp2(sw - jnp.max(sw, axis=0, keepdims=True)).astype(bf16)
        vw = jnp.concatenate([r[0] for r in vws], axis=1)
        accw = _dot(jnp.concatenate([vw, jnp.ones((8, nw * LANES), bf16)], axis=0), pw)
        owt = accw[0:HD] / accw[HD:HD + 1]
        for h in range(nq):
            qs = slice(h * QB, (h + 1) * QB)
            for g in range(GROUP):
                ls = slice(h * gl + g * LANES, h * gl + (g + 1) * LANES)
                both = oct_[:, ls] * gate[g * 3:g * 3 + 1, qs] + owt[:, ls] * gate[g * 3 + 2:g * 3 + 3, qs]
                o_ref[qs, g * HD:(g + 1) * HD] = both.T.astype(o_ref.dtype)

        pgs = []
        for h in range(nq):
            pg = None
            for g in range(GROUP):
                ls = slice(h * gl + g * LANES, h * gl + (g + 1) * LANES)
                pgg = e[:, ls] * inv[:, ls]
                pg = pgg if pg is None else pg + pgg
            pgs.append(pg)
        pg = jnp.concatenate(pgs, axis=1) if nq > 1 else pgs[0]
        hi = pg.astype(bf16)
        lo = (pg - hi.astype(f32)).astype(bf16)
        ov = ov_ref[0:rows, 0:keys]
        imp = _dot(ov, hi) + _dot(ov, lo)

        j = lax.broadcasted_iota(i32, (rows, nq * QB), 0)
        tq = b0 * QB + lax.broadcasted_iota(i32, (rows, nq * QB), 1)
        jt = jnp.right_shift(tq, SLC_LEN.bit_length() - 1)
        forced = (j == 0) | (j == jt) | (j == jt - 1)
        future = j * SLC_LEN > tq
        work = jnp.where(forced, -3e38, jnp.where(future, -1.0, imp))
        jf = j.astype(f32)
        sel = jnp.where(forced, 1.0, 0.0)
        for _ in range(n_sel - 3):
            mx = jnp.max(work, axis=0, keepdims=True)
            jm = jnp.min(jnp.where(work == mx, jf, 1e9), axis=0, keepdims=True)
            pick = jf == jm
            sel = jnp.where(pick, 1.0, sel)
            work = jnp.where(pick, -3e38, work)
        ns_ref[0, 0:rows, :] = jnp.where(future, 1.0, 1.0 - sel).astype(bf16)
        if nsr > rows:
            ns_ref[0, rows:, :] = jnp.ones((nsr - rows, nq * QB), bf16)

    per_branch = n_steps // n_branch
    for k in range(n_branch):
        last_b = (k + 1) * per_branch * nq - 1
        nt = last_b // q_per_tile + 1
        rows = (last_b + 1) * (QB // SLC_LEN)
        pl.when(step_id // per_branch == k)(functools.partial(core, nt, rows))


def _cmp_attention(projn, projt, cn, ct, bias_c, bias_t, ov_t, t, nsr, nq):
    nb = t // QB
    ncp = t // CMP_STRIDE
    nslc = t // SLC_LEN
    per_kt = KT // QB
    per = per_kt // nq
    nw = nq + WINDOW // QB
    n_sel = min(SLC_TOPN, nslc)
    assert n_sel > 3
    n_steps = nb // nq
    n_branch = 4 if nb % 64 == 0 else 1
    resident = dict(pipeline_mode=pl.Buffered(1))

    def kw_spec(jw):
        return pl.BlockSpec((QB, HD), lambda hk, s, jw=jw: (jnp.maximum(s * nq - WINDOW // QB + jw, 0), _C_KW + hk))

    def vw_spec(jw):
        def imap(hk, s, jw=jw):
            bb = jnp.maximum(s * nq - WINDOW // QB + jw, 0)
            return (bb // per_kt, _R_VW + hk, bb % per_kt)
        return pl.BlockSpec((1, HD, QB), imap)

    return pl.pallas_call(
        functools.partial(_cmp_kernel, n_sel=n_sel, n_steps=n_steps, n_branch=n_branch, nq=nq),
        grid=(N_KV, n_steps),
        in_specs=[pl.BlockSpec((1, GROUP * HD, nq * QB), lambda hk, s: (s // per, hk, s % per)),
                  pl.BlockSpec((1, 1, ncp, HD), lambda hk, s: (0, hk, 0, 0), **resident),
                  pl.BlockSpec((1, 1, HD, ncp), lambda hk, s: (1, hk, 0, 0), **resident),
                  pl.BlockSpec((1, N_CMP_BIAS + 1, LANES, GROUP * LANES), lambda hk, s: (hk, 0, 0, 0), **resident),
                  pl.BlockSpec((nslc, ncp), lambda hk, s: (0, 0), **resident),
                  pl.BlockSpec((1, 16, nq * QB), lambda hk, s: (s // per, _R_G * 8 + hk, s % per))]
                 + [kw_spec(jw) for jw in range(nw)]
                 + [vw_spec(jw) for jw in range(nw)]
                 + [pl.BlockSpec((1, N_TOK_BIAS + 1, LANES, GROUP * LANES), lambda hk, s: (hk, 0, 0, 0), **resident)],
        out_specs=[pl.BlockSpec((nq * QB, GROUP * HD), lambda hk, s: (s, hk)),
                   pl.BlockSpec((1, nsr, nq * QB), lambda hk, s: (hk, 0, s))],
        out_shape=[jax.ShapeDtypeStruct((t, N_HEADS * HD), f32),
                   jax.ShapeDtypeStruct((N_KV, nsr, t), bf16)],
        compiler_params=_params(("parallel", "arbitrary")),
        name="cmp_attention",
    )(projt, cn, ct, bias_c, ov_t, projt, *([projn] * nw), *([projt] * nw), bias_t)


def _slc_kernel(q_ref, ks_ref, vst_ref, ns_ref, bt_ref, ind_ref, g_ref, o1_ref, o_ref,
                qa_ref, s0_ref, s1_ref, p0_ref, p1_ref, acc_ref, m_ref, al_ref, mx_ref, off_ref, *, nq):
    b0 = pl.program_id(1) * nq
    gl = GROUP * LANES
    qt = _query_lanes(q_ref, nq)
    gate = jax.nn.sigmoid(g_ref[0].astype(f32))

    n_chunks = qa_ref.shape[0]
    for jc in range(n_chunks):
        nsj = -ns_ref[0, jc * SEL_CHUNK:(jc + 1) * SEL_CHUNK, :]
        qa_ref[jc, 0:HD, :] = qt
        qa_ref[jc, HD:, :] = jnp.concatenate([nsj[:, h * QB:(h + 1) * QB]
                                              for h in range(nq) for _ in range(GROUP)], axis=1)

    n_tiles = vst_ref.shape[0]
    per_chunk = SEL_CHUNK * SLC_LEN // KT
    sub = KT // LANES

    far_bias = jnp.concatenate([bt_ref[0, N_TOK_BIAS - 1, 0:1, :]] * nq, axis=1)
    ones_rows = jnp.ones((8, KT), bf16)

    def scores(kt, s_ref, far):
        jc = kt // per_chunk
        r0 = pl.multiple_of(kt * KT, KT)
        i0 = pl.multiple_of((kt - jc * per_chunk) * KT, KT)
        kaug = jnp.concatenate([ks_ref[pl.ds(r0, KT), :], ind_ref[pl.ds(i0, KT), :]], axis=1)
        s = _dot(kaug, qa_ref[jc])
        if far:
            s_ref[...] = s
            mx_ref[...] = jnp.max(s, axis=0, keepdims=True) + far_bias
            off_ref[...] = far_bias
            return
        mxs = []
        for h in range(nq):
            cols = slice(h * gl, (h + 1) * gl)
            mx = None
            for jj in range(sub):
                rows = slice(jj * LANES, (jj + 1) * LANES)
                dl = b0 + h - (kt * sub + jj)
                sj = s[rows, cols] + bt_ref[0, jnp.clip(dl, 0, N_TOK_BIAS - 1)]
                s_ref[rows, cols] = sj
                mj = jnp.max(sj, axis=0, keepdims=True)
                mx = mj if mx is None else jnp.maximum(mx, mj)
            mxs.append(mx)
        mx_ref[...] = jnp.concatenate(mxs, axis=1) if nq > 1 else mxs[0]
        off_ref[...] = jnp.zeros_like(off_ref)

    def values(kt):
        return jnp.concatenate([vst_ref[kt], ones_rows], axis=0)

    def step(kt, s_cur, p_cur, s_nxt, p_prev, far):
        m_old = m_ref[...]
        m_new = jnp.maximum(m_old, mx_ref[...])
        shift = m_new - off_ref[...]
        al_old = al_ref[...]
        scores(jnp.minimum(kt + 1, n_tiles - 1), s_nxt, far)
        acc_ref[...] = al_old * acc_ref[...] + _dot(values(jnp.maximum(kt - 1, 0)), p_prev[...])
        p_cur[...] = jnp.exp2(s_cur[...] - shift).astype(bf16)
        m_ref[...] = m_new
        al_ref[...] = jnp.exp2(m_old - m_new)

    acc_ref[...] = jnp.zeros_like(acc_ref)
    m_ref[...] = jnp.full_like(m_ref, NEG)
    al_ref[...] = jnp.ones_like(al_ref)
    p1_ref[...] = jnp.zeros_like(p1_ref)
    scores(0, s0_ref, False)

    n_pairs = (b0 + nq - 1) // (2 * KT // QB) + 1
    n_far = jnp.maximum((b0 - (N_TOK_BIAS - 2)) // sub, 0)
    n_far_pairs = jnp.maximum((n_far - 1) // 2, 0)

    def pair(far):
        def body(i, carry):
            step(2 * i, s0_ref, p0_ref, s1_ref, p1_ref, far)
            step(2 * i + 1, s1_ref, p1_ref, s0_ref, p0_ref, far)
            return carry
        return body

    lax.fori_loop(0, n_far_pairs, pair(True), 0)
    lax.fori_loop(n_far_pairs, n_pairs, pair(False), 0)
    acc = al_ref[...] * acc_ref[...] + _dot(values(2 * n_pairs - 1), p1_ref[...])

    ost = acc[0:HD] / acc[HD:HD + 1]
    for h in range(nq):
        qs = slice(h * QB, (h + 1) * QB)
        for g in range(GROUP):
            ls = slice(h * gl + g * LANES, h * gl + (g + 1) * LANES)
            ds = slice(g * HD, (g + 1) * HD)
            o_ref[qs, ds] = ((ost[:, ls] * gate[g * 3 + 1:g * 3 + 2, qs]).T + o1_ref[qs, ds]).astype(o_ref.dtype)


def _slc_attention(projn, projt, nsel_t, bias_t, ind, o1, t, nq):
    nb = t // QB
    per = KT // QB // nq
    nsr = nsel_t.shape[1]
    lanes = nq * GROUP * LANES
    resident = dict(pipeline_mode=pl.Buffered(1))
    return pl.pallas_call(
        functools.partial(_slc_kernel, nq=nq),
        grid=(N_KV, nb // nq),
        in_specs=[pl.BlockSpec((1, GROUP * HD, nq * QB), lambda hk, bq: (bq // per, hk, bq % per)),
                  pl.BlockSpec((t, HD), lambda hk, bq: (0, _C_KS + hk), **resident),
                  pl.BlockSpec((t // KT, HD, KT), lambda hk, bq: (0, _R_VS + hk, 0), **resident),
                  pl.BlockSpec((1, nsr, nq * QB), lambda hk, bq: (hk, 0, bq)),
                  pl.BlockSpec((1, N_TOK_BIAS + 1, LANES, GROUP * LANES), lambda hk, bq: (hk, 0, 0, 0), **resident),
                  pl.BlockSpec(ind.shape, lambda hk, bq: (0, 0), **resident),
                  pl.BlockSpec((1, 16, nq * QB), lambda hk, bq: (bq // per, _R_G * 8 + hk, bq % per)),
                  pl.BlockSpec((nq * QB, GROUP * HD), lambda hk, bq: (bq, hk))],
        out_specs=pl.BlockSpec((nq * QB, GROUP * HD), lambda hk, bq: (bq, hk)),
        out_shape=jax.ShapeDtypeStruct((t, N_HEADS * HD), bf16),
        scratch_shapes=[pltpu.VMEM((nsr // SEL_CHUNK, 2 * HD, lanes), bf16),
                        pltpu.VMEM((KT, lanes), f32),
                        pltpu.VMEM((KT, lanes), f32),
                        pltpu.VMEM((KT, lanes), bf16),
                        pltpu.VMEM((KT, lanes), bf16),
                        pltpu.VMEM((HD + 8, lanes), f32)]
                       + [pltpu.VMEM((1, lanes), f32)] * 4,
        compiler_params=_params(("parallel", "arbitrary")),
        name="slc_attention",
    )(projt, projn, projt, nsel_t, bias_t, ind, projt, o1)


def _layer_norm(z, g, b):
    mu = jnp.mean(z, axis=-1, keepdims=True)
    var = jnp.mean(jnp.square(z - mu), axis=-1, keepdims=True)
    return (z - mu) * lax.rsqrt(var + LN_EPS) * g + b


def _post_kernel(o2_ref, ya_ref, mb_ref, x_ref, p_ref, wao_ref, wmix_ref, wple_ref, wpg_ref,
                 g_ref, b_ref, x1_ref, r2_ref, *, alpha):
    yb = _dot(o2_ref[...], wao_ref[...])
    mixed = ya_ref[...] + jax.nn.sigmoid(mb_ref[...].astype(f32)) * yb
    z = alpha * x_ref[...] + _dot(mixed.astype(bf16), wmix_ref[...])
    x1 = _layer_norm(z, g_ref[...], b_ref[...])
    x1b = x1.astype(bf16)
    x1_ref[...] = x1b
    ple = _dot(p_ref[...].astype(bf16), wple_ref[...]) * jax.nn.sigmoid(_dot(x1b, wpg_ref[...]))
    r2_ref[...] = alpha * x1 + ple


def _post_attention(o2, ya, mab, x2d, p2d, wao, wmix, wple, wpg, g1, b1, alpha, tm):
    t = x2d.shape[0]
    row = lambda w: pl.BlockSpec((tm, w), lambda i: (i, 0))
    full = lambda a: pl.BlockSpec(a.shape, lambda i: (0, 0), pipeline_mode=pl.Buffered(1))
    return pl.pallas_call(
        functools.partial(_post_kernel, alpha=alpha),
        grid=(t // tm,),
        in_specs=[row(D_MODEL), row(D_MODEL),
                  pl.BlockSpec((tm, D_MODEL), lambda i: (i, 1)),
                  row(D_MODEL), row(PLE_DIM),
                  full(wao), full(wmix), full(wple), full(wpg), full(g1), full(b1)],
        out_specs=[row(D_MODEL), row(D_MODEL)],
        out_shape=[jax.ShapeDtypeStruct((t, D_MODEL), bf16),
                   jax.ShapeDtypeStruct((t, D_MODEL), f32)],
        compiler_params=_params(("parallel",)),
        name="post_attention",
    )(o2, ya, mab, x2d, p2d, wao, wmix, wple, wpg, g1, b1)


def _mlp_kernel(x1_ref, r2_ref, wu_ref, wd_ref, g_ref, b_ref, o_ref):
    k = pl.program_id(1)

    @pl.when(k == 0)
    def _():
        o_ref[...] = r2_ref[...]

    h = jnp.square(jnp.maximum(_dot(x1_ref[...], wu_ref[...]), 0.0))
    o_ref[...] += _dot(h.astype(bf16), wd_ref[...])

    @pl.when(k == pl.num_programs(1) - 1)
    def _():
        o_ref[...] = _layer_norm(o_ref[...], g_ref[...], b_ref[...])


def _mlp(x1b, r2, wu, wd, g2, b2, tm, fc):
    t = x1b.shape[0]
    return pl.pallas_call(
        _mlp_kernel,
        grid=(t // tm, D_FF // fc),
        in_specs=[pl.BlockSpec((tm, D_MODEL), lambda i, k: (i, 0)),
                  pl.BlockSpec((tm, D_MODEL), lambda i, k: (i, 0)),
                  pl.BlockSpec((D_MODEL, fc), lambda i, k: (0, k)),
                  pl.BlockSpec((fc, D_MODEL), lambda i, k: (k, 0)),
                  pl.BlockSpec((1, D_MODEL), lambda i, k: (0, 0)),
                  pl.BlockSpec((1, D_MODEL), lambda i, k: (0, 0))],
        out_specs=pl.BlockSpec((tm, D_MODEL), lambda i, k: (i, 0)),
        out_shape=jax.ShapeDtypeStruct((t, D_MODEL), f32),
        compiler_params=_params(("parallel", "arbitrary")),
        name="mlp",
    )(x1b, r2, wu, wd, g2, b2)


def _layer(x2d, p2d, w_in, conv_w, pe_k, w1_k, w2_k, pe_v, w1_v, w2_v, w_conv_out, w_attn_out,
           w_mix_out, ln1_g, ln1_b, w_mlp_up, w_mlp_down, w_ple, w_ple_gate, ln2_g, ln2_b,
           rel_bias, alpha):
    t = x2d.shape[0]
    assert t % 1024 == 0 and x2d.shape[1] == D_MODEL
    kvw = N_KV * HD
    o_q = 3 * D_MODEL
    o_kc = o_q + N_HEADS * HD
    o_ng = o_kc + 6 * kvw
    o_ma = o_ng + N_HEADS * 3

    w_bf = _to_bf16(w_in, LANES, "cast_w_in")

    def cols(a, n):
        return w_bf[:, a:a + n]

    tn = kvw
    conv_tiles = 3 * D_MODEL // tn
    ks_tile, kw_tile = (o_kc + 2 * kvw) // tn, (o_kc + 4 * kvw) // tn
    assert _C_KS * LANES == conv_tiles * tn and _C_KW * LANES == (conv_tiles + 1) * tn

    def nat_tile(j):
        return jnp.where(j < conv_tiles, j, jnp.where(j == conv_tiles, ks_tile, kw_tile))

    wng = jnp.pad(cols(o_ng, N_HEADS * 3).reshape(D_MODEL, N_KV, GROUP * 3), ((0, 0), (0, 0), (0, 16 - GROUP * 3)))
    w_t = jnp.concatenate(
        [cols(o_q, N_HEADS * HD), cols(o_kc + 3 * kvw, kvw), cols(o_kc + 5 * kvw, kvw),
         wng.reshape(D_MODEL, N_KV * 16),
         jnp.zeros((D_MODEL, _N_T - _R_G * LANES - N_KV * 16), bf16)], axis=1)
    row_scale = jnp.concatenate([jnp.full((N_HEADS * HD, 1), HD ** -0.5 * LOG2E, f32),
                                 jnp.ones((_N_T - N_HEADS * HD, 1), f32)], axis=0)

    xb = x2d.astype(bf16)
    projn = _proj_nat(xb, w_bf, 1024, tn, "proj_nat", _N_NAT, nat_tile)
    mab = _proj_nat(xb, cols(o_ma, 2 * D_MODEL), 1024, 1024, "proj_gate")
    kvc = _proj_chunked(xb, w_bf, 1024, 2 * kvw, o_kc // (2 * kvw))
    projt = _proj_t(w_t, xb, row_scale, 640, KT)

    ya = _mixer_a(projn, mab, conv_w, w_conv_out.astype(bf16), 512)

    half = CMP_STRIDE * HD
    w1ab = jnp.stack([jnp.concatenate([w[:half], w[half:]], axis=1) for w in (w1_k, w1_v)]).astype(bf16)
    pe2 = jnp.stack([jnp.stack([pe[:CMP_STRIDE], pe[CMP_STRIDE:]], axis=1) for pe in (pe_k, pe_v)])
    pe2 = jnp.pad(pe2, ((0, 0), (0, 0), (0, 6), (0, 0)))
    w2 = jnp.stack([w2_k, w2_v]).astype(bf16)
    cn, ct = _compress(kvc, pe2, w1ab, w2, jnp.swapaxes(w2, 1, 2))

    bias_t = _bias_tiles(rel_bias, N_TOK_BIAS, 1, 0, WINDOW // QB, WINDOW - 1, "bias_tok")
    bias_c = _bias_tiles(rel_bias, N_CMP_BIAS, CMP_STRIDE, CMP_LEN - 1, 0, -1, "bias_cmp")

    ncp = t // CMP_STRIDE
    nslc = t // SLC_LEN
    nsr = -(-nslc // SEL_CHUNK) * SEL_CHUNK
    ci = jnp.arange(ncp)[None, :] * CMP_STRIDE
    sj = jnp.arange(nslc)[:, None] * SLC_LEN
    ov_t = ((ci < sj + SLC_LEN) & (ci + CMP_LEN > sj)).astype(bf16)
    pos = jnp.arange(min(t, SEL_CHUNK * SLC_LEN))[:, None] // SLC_LEN
    ind = jnp.where(pos == jnp.arange(SEL_CHUNK)[None, :], SEL_BIG, 0.0).astype(bf16)

    o1, nsel_t = _cmp_attention(projn, projt, cn, ct, bias_c, bias_t, ov_t, t, nsr, CMP_NQ)
    o2 = _slc_attention(projn, projt, nsel_t, bias_t, ind, o1, t, SLC_NQ)

    x1b, r2 = _post_attention(o2, ya, mab, x2d, p2d, w_attn_out.astype(bf16), w_mix_out.astype(bf16),
                              w_ple.astype(bf16), w_ple_gate.astype(bf16),
                              ln1_g[None, :], ln1_b[None, :], alpha, 256)
    return _mlp(x1b, r2, w_mlp_up.astype(bf16), w_mlp_down.astype(bf16), ln2_g[None, :], ln2_b[None, :], 512, 512)


def kernel(x, p, w_in, conv_w, cmp_pe_k, cmp_w1_k, cmp_w2_k, cmp_pe_v, cmp_w1_v, cmp_w2_v, w_conv_out, w_attn_out, w_mix_out, ln1_g, ln1_b, w_mlp_up, w_mlp_down, w_ple, w_ple_gate, ln2_g, ln2_b, rel_bias):
    bsz, t, d = x.shape
    assert bsz == 1
    depth = w_in.shape[0]
    alpha = (2 * depth) ** 0.25
    x2d = x[0]
    for i in range(depth):
        x2d = _layer(x2d, p[i, 0], w_in[i], conv_w[i], cmp_pe_k[i], cmp_w1_k[i], cmp_w2_k[i],
                     cmp_pe_v[i], cmp_w1_v[i], cmp_w2_v[i], w_conv_out[i], w_attn_out[i], w_mix_out[i],
                     ln1_g[i], ln1_b[i], w_mlp_up[i], w_mlp_down[i], w_ple[i], w_ple_gate[i],
                     ln2_g[i], ln2_b[i], rel_bias, alpha)
    return x2d[None]
```

```python
import functools
import math

import jax
import jax.numpy as jnp
from jax import lax
from jax.experimental import pallas as pl
from jax.experimental.pallas import tpu as pltpu

f32 = jnp.float32
bf16 = jnp.bfloat16
i32 = jnp.int32

D_MODEL = 2048
N_HEADS = 16
N_KV = 4
GROUP = N_HEADS // N_KV
HD = 128
CMP_LEN = 32
CMP_STRIDE = 16
CMP_HIDDEN = 2 * HD
SLC_LEN = 64
SLC_TOPN = 16
WINDOW = 512
QB = 128
D_FF = 4 * D_MODEL
PLE_DIM = 256
REL_BUCKETS = 32
REL_EXACT = REL_BUCKETS // 2
REL_MAX_DIST = 4096
LN_EPS = 1e-5
NEG = -1e30
FORCE_SCORE = 1e9
LOG2E = 1.4426950408889634

LANES = 128
KT = 512
SEL_BIG = 2.0 ** 100
SEL_CHUNK = 128
SLC_NQ = 1
CMP_NQ = 2
N_TOK_BIAS = 25
N_CMP_BIAS = 40
VMEM_LIMIT = 56 * 1024 * 1024

_C_BG, _C_CG, _C_HX = 0, 16, 32
_C_KS, _C_KW = 48, 52
_N_NAT = 56 * LANES
_R_Q, _R_VS, _R_VW, _R_G = 0, 16, 20, 24
_N_T = 25 * LANES


def _dot(a, b):
    return jnp.dot(a, b, preferred_element_type=f32)


def _dot_nt(a, b):
    return lax.dot_general(a, b, (((1,), (1,)), ((), ())), preferred_element_type=f32)


def _params(sem, vmem=VMEM_LIMIT):
    return pltpu.CompilerParams(dimension_semantics=sem, vmem_limit_bytes=vmem)


def _mm_kernel(x_ref, w_ref, o_ref):
    o_ref[...] = _dot(x_ref[...], w_ref[...]).astype(o_ref.dtype)


def _proj_nat(xb, w, tm, tn, name, n=None, col_tile=lambda j: j):
    m, k = xb.shape
    n = w.shape[1] if n is None else n
    return pl.pallas_call(
        _mm_kernel,
        grid=(m // tm, n // tn),
        in_specs=[pl.BlockSpec((tm, k), lambda i, j: (i, 0)),
                  pl.BlockSpec((k, tn), lambda i, j: (0, col_tile(j)))],
        out_specs=pl.BlockSpec((tm, tn), lambda i, j: (i, j)),
        out_shape=jax.ShapeDtypeStruct((m, n), bf16),
        compiler_params=_params(("parallel", "arbitrary")),
        name=name,
    )(xb, w)


def _mmt_kernel(w_ref, x_ref, s_ref, o_ref):
    acc = lax.dot_general(w_ref[...], x_ref[...], (((0,), (1,)), ((), ())), preferred_element_type=f32)
    o_ref[0] = (acc * s_ref[...]).astype(o_ref.dtype)


def _proj_t(w, xb, row_scale, tm, tn):
    k, n = w.shape
    t = xb.shape[0]
    return pl.pallas_call(
        _mmt_kernel,
        grid=(n // tm, t // tn),
        in_specs=[pl.BlockSpec((k, tm), lambda i, j: (0, i)),
                  pl.BlockSpec((tn, k), lambda i, j: (j, 0)),
                  pl.BlockSpec((tm, 1), lambda i, j: (i, 0))],
        out_specs=pl.BlockSpec((1, tm, tn), lambda i, j: (j, i, 0)),
        out_shape=jax.ShapeDtypeStruct((t // tn, n, tn), bf16),
        compiler_params=_params(("parallel", "arbitrary")),
        name="proj_t",
    )(w, xb, row_scale)


def _cast_kernel(x_ref, o_ref):
    o_ref[...] = x_ref[...].astype(o_ref.dtype)


def _to_bf16(w, tm, name):
    r, c = w.shape
    return pl.pallas_call(
        _cast_kernel,
        grid=(r // tm,),
        in_specs=[pl.BlockSpec((tm, c), lambda i: (i, 0))],
        out_specs=pl.BlockSpec((tm, c), lambda i: (i, 0)),
        out_shape=jax.ShapeDtypeStruct((r, c), bf16),
        compiler_params=_params(("parallel",)),
        name=name,
    )(w)


def _ya_kernel(bg_ref, cg_ref, hx_ref, cgh_ref, hxh_ref, cw_ref, w_ref, ma_ref, o_ref, a_ref):
    i = pl.program_id(0)
    tm = bg_ref.shape[0]
    cc = 512
    row = lax.broadcasted_iota(i32, (tm, cc), 0)
    for c in range(D_MODEL // cc):
        sl = slice(c * cc, (c + 1) * cc)
        u = cg_ref[:, sl].astype(f32) * hx_ref[:, sl].astype(f32)
        uh = cgh_ref[:, sl].astype(f32) * hxh_ref[:, sl].astype(f32)
        uh = jnp.where(i > 0, uh, 0.0)
        u1 = jnp.where(row == 0, uh[15:16, :], pltpu.roll(u, 1, 0))
        u2 = jnp.where(row == 0, uh[14:15, :], jnp.where(row == 1, uh[15:16, :], pltpu.roll(u, 2, 0)))
        cw = cw_ref[:, sl]
        conv = cw[0:1, :] * u2 + cw[1:2, :] * u1 + cw[2:3, :] * u
        a_ref[:, sl] = (bg_ref[:, sl].astype(f32) * conv).astype(bf16)
    y = _dot(a_ref[...], w_ref[...])
    o_ref[...] = jax.nn.sigmoid(ma_ref[...].astype(f32)) * y


def _mixer_a(projn, mab, conv_w, w_out, tm):
    t = projn.shape[0]
    hb = tm // 16
    wide = lambda c: pl.BlockSpec((tm, D_MODEL), lambda i, c=c: (i, c))
    halo = lambda c: pl.BlockSpec((16, D_MODEL), lambda i, c=c: (jnp.maximum(i * hb - 1, 0), c))
    return pl.pallas_call(
        _ya_kernel,
        grid=(t // tm,),
        in_specs=[wide(0), wide(1), wide(2), halo(1), halo(2),
                  pl.BlockSpec((3, D_MODEL), lambda i: (0, 0)),
                  pl.BlockSpec((D_MODEL, D_MODEL), lambda i: (0, 0)),
                  wide(0)],
        out_specs=pl.BlockSpec((tm, D_MODEL), lambda i: (i, 0)),
        out_shape=jax.ShapeDtypeStruct((t, D_MODEL), f32),
        scratch_shapes=[pltpu.VMEM((tm, D_MODEL), bf16)],
        compiler_params=_params(("parallel",)),
        name="mixer_a",
    )(projn, projn, projn, projn, projn, conv_w, w_out, mab)


def _compress_kernel(x_ref, w_ref, pe_ref, w2_ref, w2t_ref, on_ref, ot_ref, acc_ref, pet_ref):
    l = pl.program_id(1)
    nl = pl.num_programs(1)
    n = x_ref.shape[0]

    @pl.when(l == 0)
    def _():
        acc_ref[...] = jnp.zeros_like(acc_ref)
        pet_ref[...] = jnp.zeros_like(pet_ref)

    wblk = w_ref[0]
    for h in range(N_KV):
        acc_ref[h] += _dot(x_ref[:, h * HD:(h + 1) * HD], wblk)
    pet_ref[...] += _dot(pe_ref[0, 0].astype(bf16), wblk)

    @pl.when(l == nl - 1)
    def _():
        r = pet_ref[...]
        peterm = r[0:1, :CMP_HIDDEN] + r[1:2, CMP_HIDDEN:]
        for h in range(N_KV):
            ab = acc_ref[h]
            hid = ab[:, :CMP_HIDDEN] + pltpu.roll(ab[:, CMP_HIDDEN:], n - 1, 0) + peterm
            gl = jax.nn.gelu(hid).astype(bf16)
            on_ref[0, h] = _dot(gl, w2_ref[0]).astype(bf16)
            ot_ref[0, h] = _dot_nt(w2t_ref[0], gl).astype(bf16)


def _chunked_mm_kernel(x_ref, w_ref, o_ref, acc_ref):
    acc = _dot(x_ref[...], w_ref[...])
    n = w_ref.shape[1]
    rows = o_ref.shape[0]
    for c in range(n // LANES):
        acc_ref[c] = acc[:, c * LANES:(c + 1) * LANES]
    for l in range(CMP_STRIDE):
        for c in range(n // LANES):
            o_ref[:, l * n + c * LANES:l * n + (c + 1) * LANES] = (
                acc_ref[c, pl.ds(l, rows, stride=CMP_STRIDE), :].astype(o_ref.dtype))


def _proj_chunked(xb, w, tm, n, col_tile):
    m, k = xb.shape
    return pl.pallas_call(
        _chunked_mm_kernel,
        grid=(m // tm,),
        in_specs=[pl.BlockSpec((tm, k), lambda i: (i, 0)),
                  pl.BlockSpec((k, n), lambda i: (0, col_tile))],
        out_specs=pl.BlockSpec((tm // CMP_STRIDE, CMP_STRIDE * n), lambda i: (i, 0)),
        out_shape=jax.ShapeDtypeStruct((m // CMP_STRIDE, CMP_STRIDE * n), bf16),
        scratch_shapes=[pltpu.VMEM((n // LANES, tm, LANES), f32)],
        compiler_params=_params(("parallel",)),
        name="proj_cmp",
    )(xb, w)


def _compress(xr, pe2, w1ab, w2, w2t):
    n = xr.shape[0]
    return pl.pallas_call(
        _compress_kernel,
        grid=(2, CMP_STRIDE),
        in_specs=[pl.BlockSpec((n, N_KV * HD), lambda kv, l: (0, 2 * l + kv)),
                  pl.BlockSpec((1, HD, 2 * CMP_HIDDEN), lambda kv, l: (kv, l, 0)),
                  pl.BlockSpec((1, 1, 8, HD), lambda kv, l: (kv, l, 0, 0)),
                  pl.BlockSpec((1, CMP_HIDDEN, HD), lambda kv, l: (kv, 0, 0)),
                  pl.BlockSpec((1, HD, CMP_HIDDEN), lambda kv, l: (kv, 0, 0))],
        out_specs=[pl.BlockSpec((1, N_KV, n, HD), lambda kv, l: (kv, 0, 0, 0)),
                   pl.BlockSpec((1, N_KV, HD, n), lambda kv, l: (kv, 0, 0, 0))],
        out_shape=[jax.ShapeDtypeStruct((2, N_KV, n, HD), bf16),
                   jax.ShapeDtypeStruct((2, N_KV, HD, n), bf16)],
        scratch_shapes=[pltpu.VMEM((N_KV, n, 2 * CMP_HIDDEN), f32),
                        pltpu.VMEM((8, 2 * CMP_HIDDEN), f32)],
        compiler_params=_params(("arbitrary", "arbitrary")),
        name="compress",
    )(xr, w1ab, pe2, w2, w2t)


def _rel_bucket(dist):
    n = jnp.maximum(dist, 0)
    nf = jnp.maximum(n, 1).astype(f32)
    large = REL_EXACT + (jnp.log(nf / REL_EXACT) / math.log(REL_MAX_DIST / REL_EXACT)
                         * (REL_BUCKETS - REL_EXACT)).astype(i32)
    large = jnp.minimum(large, REL_BUCKETS - 1)
    return jnp.where(n < REL_EXACT, n, large)


def _bias_kernel(tbl_ref, o_ref, *, kstride, koff, n_reg, edge_o, edge_max):
    hk = pl.program_id(0)
    o = pl.program_id(1)
    is_edge = o == n_reg
    ki = lax.broadcasted_iota(i32, (LANES, LANES), 0)
    qi = lax.broadcasted_iota(i32, (LANES, LANES), 1)
    dist = jnp.where(is_edge, edge_o, o) * LANES + qi - kstride * ki - koff
    valid = (dist >= 0) & (dist <= jnp.where(is_edge, edge_max, 2 ** 30))
    bucket = _rel_bucket(dist)
    for g in range(GROUP):
        row = jnp.broadcast_to(tbl_ref[pl.ds(hk * GROUP + g, 1), :], (LANES, LANES))
        acc = jnp.take_along_axis(row, bucket, axis=1)
        o_ref[0, 0, :, g * LANES:(g + 1) * LANES] = jnp.where(valid, acc * LOG2E, NEG)


def _bias_tiles(rel_bias, n_reg, kstride, koff, edge_o, edge_max, name):
    return pl.pallas_call(
        functools.partial(_bias_kernel, kstride=kstride, koff=koff, n_reg=n_reg, edge_o=edge_o,
                          edge_max=edge_max),
        grid=(N_KV, n_reg + 1),
        in_specs=[pl.BlockSpec((N_HEADS, LANES), lambda hk, o: (0, 0))],
        out_specs=pl.BlockSpec((1, 1, LANES, GROUP * LANES), lambda hk, o: (hk, o, 0, 0)),
        out_shape=jax.ShapeDtypeStruct((N_KV, n_reg + 1, LANES, GROUP * LANES), f32),
        compiler_params=_params(("parallel", "arbitrary")),
        name=name,
    )(jnp.pad(rel_bias.astype(f32).T, ((0, 0), (0, LANES - REL_BUCKETS))))


def _query_lanes(q_ref, nq):
    qb = q_ref[0]
    return jnp.concatenate([qb[g * HD:(g + 1) * HD, h * QB:(h + 1) * QB]
                            for h in range(nq) for g in range(GROUP)], axis=1)


def _cmp_kernel(q_ref, kc_ref, vct_ref, bc_ref, ov_ref, g_ref, *rest, n_sel, n_steps, n_branch, nq):
    nw = nq + WINDOW // QB
    kws, vws = rest[:nw], rest[nw:2 * nw]
    bt_ref, o_ref, ns_ref = rest[2 * nw:]
    step_id = pl.program_id(1)
    b0 = step_id * nq
    nsr = ns_ref.shape[1]
    gl = GROUP * LANES
    qt = _query_lanes(q_ref, nq)
    gate = jax.nn.sigmoid(g_ref[0].astype(f32))
    cmp_per_q = QB // CMP_STRIDE
    q_per_tile = LANES // cmp_per_q

    def core(nt, rows):
        keys = nt * LANES
        s = _dot(kc_ref[0, 0, 0:keys, :], qt)
        col_parts = []
        for h in range(nq):
            parts = []
            for c in range(nt):
                o = b0 + h - q_per_tile * c
                idx = jnp.where(o < 0, N_CMP_BIAS, jnp.minimum(o, N_CMP_BIAS - 1))
                parts.append(s[c * LANES:(c + 1) * LANES, h * gl:(h + 1) * gl] + bc_ref[0, idx])
            col_parts.append(jnp.concatenate(parts, axis=0) if nt > 1 else parts[0])
        s = jnp.concatenate(col_parts, axis=1) if nq > 1 else col_parts[0]
        m = jnp.max(s, axis=0, keepdims=True)
        e = jnp.exp2(s - m)
        vals = jnp.concatenate([vct_ref[0, 0, :, 0:keys], jnp.ones((8, keys), bf16)], axis=0)
        acc = _dot(vals, e.astype(bf16))
        inv = jnp.where(m > 0.5 * NEG, 1.0 / acc[HD:HD + 1], 0.0)
        oct_ = acc[0:HD] * inv

        sw = _dot(jnp.concatenate([r[...] for r in kws], axis=0), qt)
        col_parts = []
        for h in range(nq):
            parts = []
            for jw in range(nw):
                dl = h + WINDOW // QB - jw
                if dl < 0 or dl > WINDOW // QB:
                    parts.append(jnp.full((LANES, gl), NEG, f32))
                    continue
                tile = N_TOK_BIAS if dl == WINDOW // QB else dl
                before_start = jnp.where(b0 + h - dl >= 0, 0.0, NEG)
                parts.append(sw[jw * LANES:(jw + 1) * LANES, h * gl:(h + 1) * gl] + bt_ref[0, tile] + before_start)
            col_parts.append(jnp.concatenate(parts, axis=0))
        sw = jnp.concatenate(col_parts, axis=1) if nq > 1 else col_parts[0]
        pw = jnp.exp2(sw - jnp.max(sw, axis=0, keepdims=True)).astype(bf16)
        vw = jnp.concatenate([r[0] for r in vws], axis=1)
        accw = _dot(jnp.concatenate([vw, jnp.ones((8, nw * LANES), bf16)], axis=0), pw)
        owt = accw[0:HD] / accw[HD:HD + 1]
        for h in range(nq):
            qs = slice(h * QB, (h + 1) * QB)
            for g in range(GROUP):
                ls = slice(h * gl + g * LANES, h * gl + (g + 1) * LANES)
                both = oct_[:, ls] * gate[g * 3:g * 3 + 1, qs] + owt[:, ls] * gate[g * 3 + 2:g * 3 + 3, qs]
                o_ref[qs, g * HD:(g + 1) * HD] = both.T.astype(o_ref.dtype)

        pgs = []
        for h in range(nq):
            pg = None
            for g in range(GROUP):
                ls = slice(h * gl + g * LANES, h * gl + (g + 1) * LANES)
                pgg = e[:, ls] * inv[:, ls]
                pg = pgg if pg is None else pg + pgg
            pgs.append(pg)
        pg = jnp.concatenate(pgs, axis=1) if nq > 1 else pgs[0]
        hi = pg.astype(bf16)
        lo = (pg - hi.astype(f32)).astype(bf16)
        ov = ov_ref[0:rows, 0:keys]
        imp = _dot(ov, hi) + _dot(ov, lo)

        j = lax.broadcasted_iota(i32, (rows, nq * QB), 0)
        tq = b0 * QB + lax.broadcasted_iota(i32, (rows, nq * QB), 1)
        jt = jnp.right_shift(tq, SLC_LEN.bit_length() - 1)
        forced = (j == 0) | (j == jt) | (j == jt - 1)
        future = j * SLC_LEN > tq
        work = jnp.where(forced, -3e38, jnp.where(future, -1.0, imp))
        jf = j.astype(f32)
        sel = jnp.where(forced, 1.0, 0.0)
        for _ in range(n_sel - 3):
            mx = jnp.max(work, axis=0, keepdims=True)
            jm = jnp.min(jnp.where(work == mx, jf, 1e9), axis=0, keepdims=True)
            pick = jf == jm
            sel = jnp.where(pick, 1.0, sel)
            work = jnp.where(pick, -3e38, work)
        ns_ref[0, 0:rows, :] = jnp.where(future, 1.0, 1.0 - sel).astype(bf16)
        if nsr > rows:
            ns_ref[0, rows:, :] = jnp.ones((nsr - rows, nq * QB), bf16)

    per_branch = n_steps // n_branch
    for k in range(n_branch):
        last_b = (k + 1) * per_branch * nq - 1
        nt = last_b // q_per_tile + 1
        rows = (last_b + 1) * (QB // SLC_LEN)
        pl.when(step_id // per_branch == k)(functools.partial(core, nt, rows))


def _cmp_attention(projn, projt, cn, ct, bias_c, bias_t, ov_t, t, nsr, nq):
    nb = t // QB
    ncp = t // CMP_STRIDE
    nslc = t // SLC_LEN
    per_kt = KT // QB
    per = per_kt // nq
    nw = nq + WINDOW // QB
    n_sel = min(SLC_TOPN, nslc)
    assert n_sel > 3
    n_steps = nb // nq
    n_branch = 4 if nb % 64 == 0 else 1
    resident = dict(pipeline_mode=pl.Buffered(1))

    def kw_spec(jw):
        return pl.BlockSpec((QB, HD), lambda hk, s, jw=jw: (jnp.maximum(s * nq - WINDOW // QB + jw, 0), _C_KW + hk))

    def vw_spec(jw):
        def imap(hk, s, jw=jw):
            bb = jnp.maximum(s * nq - WINDOW // QB + jw, 0)
            return (bb // per_kt, _R_VW + hk, bb % per_kt)
        return pl.BlockSpec((1, HD, QB), imap)

    return pl.pallas_call(
        functools.partial(_cmp_kernel, n_sel=n_sel, n_steps=n_steps, n_branch=n_branch, nq=nq),
        grid=(N_KV, n_steps),
        in_specs=[pl.BlockSpec((1, GROUP * HD, nq * QB), lambda hk, s: (s // per, hk, s % per)),
                  pl.BlockSpec((1, 1, ncp, HD), lambda hk, s: (0, hk, 0, 0), **resident),
                  pl.BlockSpec((1, 1, HD, ncp), lambda hk, s: (1, hk, 0, 0), **resident),
                  pl.BlockSpec((1, N_CMP_BIAS + 1, LANES, GROUP * LANES), lambda hk, s: (hk, 0, 0, 0), **resident),
                  pl.BlockSpec((nslc, ncp), lambda hk, s: (0, 0), **resident),
                  pl.BlockSpec((1, 16, nq * QB), lambda hk, s: (s // per, _R_G * 8 + hk, s % per))]
                 + [kw_spec(jw) for jw in range(nw)]
                 + [vw_spec(jw) for jw in range(nw)]
                 + [pl.BlockSpec((1, N_TOK_BIAS + 1, LANES, GROUP * LANES), lambda hk, s: (hk, 0, 0, 0), **resident)],
        out_specs=[pl.BlockSpec((nq * QB, GROUP * HD), lambda hk, s: (s, hk)),
                   pl.BlockSpec((1, nsr, nq * QB), lambda hk, s: (hk, 0, s))],
        out_shape=[jax.ShapeDtypeStruct((t, N_HEADS * HD), f32),
                   jax.ShapeDtypeStruct((N_KV, nsr, t), bf16)],
        compiler_params=_params(("parallel", "arbitrary")),
        name="cmp_attention",
    )(projt, cn, ct, bias_c, ov_t, projt, *([projn] * nw), *([projt] * nw), bias_t)


def _slc_kernel(q_ref, ks_ref, vst_ref, ns_ref, bt_ref, ind_ref, g_ref, o1_ref, o_ref,
                qa_ref, s0_ref, s1_ref, p0_ref, p1_ref, acc_ref, m_ref, al_ref, mx_ref, off_ref, *, nq):
    b0 = pl.program_id(1) * nq
    gl = GROUP * LANES
    qt = _query_lanes(q_ref, nq)
    gate = jax.nn.sigmoid(g_ref[0].astype(f32))

    n_chunks = qa_ref.shape[0]
    for jc in range(n_chunks):
        nsj = -ns_ref[0, jc * SEL_CHUNK:(jc + 1) * SEL_CHUNK, :]
        qa_ref[jc, 0:HD, :] = qt
        qa_ref[jc, HD:, :] = jnp.concatenate([nsj[:, h * QB:(h + 1) * QB]
                                              for h in range(nq) for _ in range(GROUP)], axis=1)

    n_tiles = vst_ref.shape[0]
    per_chunk = SEL_CHUNK * SLC_LEN // KT
    sub = KT // LANES

    far_bias = jnp.concatenate([bt_ref[0, N_TOK_BIAS - 1, 0:1, :]] * nq, axis=1)
    ones_rows = jnp.ones((8, KT), bf16)

    def scores(kt, s_ref, far):
        jc = kt // per_chunk
        r0 = pl.multiple_of(kt * KT, KT)
        i0 = pl.multiple_of((kt - jc * per_chunk) * KT, KT)
        kaug = jnp.concatenate([ks_ref[pl.ds(r0, KT), :], ind_ref[pl.ds(i0, KT), :]], axis=1)
        s = _dot(kaug, qa_ref[jc])
        if far:
            s_ref[...] = s
            mx_ref[...] = jnp.max(s, axis=0, keepdims=True) + far_bias
            off_ref[...] = far_bias
            return
        mxs = []
        for h in range(nq):
            cols = slice(h * gl, (h + 1) * gl)
            mx = None
            for jj in range(sub):
                rows = slice(jj * LANES, (jj + 1) * LANES)
                dl = b0 + h - (kt * sub + jj)
                sj = s[rows, cols] + bt_ref[0, jnp.clip(dl, 0, N_TOK_BIAS - 1)]
                s_ref[rows, cols] = sj
                mj = jnp.max(sj, axis=0, keepdims=True)
                mx = mj if mx is None else jnp.maximum(mx, mj)
            mxs.append(mx)
        mx_ref[...] = jnp.concatenate(mxs, axis=1) if nq > 1 else mxs[0]
        off_ref[...] = jnp.zeros_like(off_ref)

    def values(kt):
        return jnp.concatenate([vst_ref[kt], ones_rows], axis=0)

    def step(kt, s_cur, p_cur, s_nxt, p_prev, far):
        m_old = m_ref[...]
        m_new = jnp.maximum(m_old, mx_ref[...])
        shift = m_new - off_ref[...]
        al_old = al_ref[...]
        scores(jnp.minimum(kt + 1, n_tiles - 1), s_nxt, far)
        acc_ref[...] = al_old * acc_ref[...] + _dot(values(jnp.maximum(kt - 1, 0)), p_prev[...])
        p_cur[...] = jnp.exp2(s_cur[...] - shift).astype(bf16)
        m_ref[...] = m_new
        al_ref[...] = jnp.exp2(m_old - m_new)

    acc_ref[...] = jnp.zeros_like(acc_ref)
    m_ref[...] = jnp.full_like(m_ref, NEG)
    al_ref[...] = jnp.ones_like(al_ref)
    p1_ref[...] = jnp.zeros_like(p1_ref)
    scores(0, s0_ref, False)

    n_pairs = (b0 + nq - 1) // (2 * KT // QB) + 1
    n_far = jnp.maximum((b0 - (N_TOK_BIAS - 2)) // sub, 0)
    n_far_pairs = jnp.maximum((n_far - 1) // 2, 0)

    def pair(far):
        def body(i, carry):
            step(2 * i, s0_ref, p0_ref, s1_ref, p1_ref, far)
            step(2 * i + 1, s1_ref, p1_ref, s0_ref, p0_ref, far)
            return carry
        return body

    lax.fori_loop(0, n_far_pairs, pair(True), 0)
    lax.fori_loop(n_far_pairs, n_pairs, pair(False), 0)
    acc = al_ref[...] * acc_ref[...] + _dot(values(2 * n_pairs - 1), p1_ref[...])

    ost = acc[0:HD] / acc[HD:HD + 1]
    for h in range(nq):
        qs = slice(h * QB, (h + 1) * QB)
        for g in range(GROUP):
            ls = slice(h * gl + g * LANES, h * gl + (g + 1) * LANES)
            ds = slice(g * HD, (g + 1) * HD)
            o_ref[qs, ds] = ((ost[:, ls] * gate[g * 3 + 1:g * 3 + 2, qs]).T + o1_ref[qs, ds]).astype(o_ref.dtype)


def _slc_attention(projn, projt, nsel_t, bias_t, ind, o1, t, nq):
    nb = t // QB
    per = KT // QB // nq
    nsr = nsel_t.shape[1]
    lanes = nq * GROUP * LANES
    resident = dict(pipeline_mode=pl.Buffered(1))
    return pl.pallas_call(
        functools.partial(_slc_kernel, nq=nq),
        grid=(N_KV, nb // nq),
        in_specs=[pl.BlockSpec((1, GROUP * HD, nq * QB), lambda hk, bq: (bq // per, hk, bq % per)),
                  pl.BlockSpec((t, HD), lambda hk, bq: (0, _C_KS + hk), **resident),
                  pl.BlockSpec((t // KT, HD, KT), lambda hk, bq: (0, _R_VS + hk, 0), **resident),
                  pl.BlockSpec((1, nsr, nq * QB), lambda hk, bq: (hk, 0, bq)),
                  pl.BlockSpec((1, N_TOK_BIAS + 1, LANES, GROUP * LANES), lambda hk, bq: (hk, 0, 0, 0), **resident),
                  pl.BlockSpec(ind.shape, lambda hk, bq: (0, 0), **resident),
                  pl.BlockSpec((1, 16, nq * QB), lambda hk, bq: (bq // per, _R_G * 8 + hk, bq % per)),
                  pl.BlockSpec((nq * QB, GROUP * HD), lambda hk, bq: (bq, hk))],
        out_specs=pl.BlockSpec((nq * QB, GROUP * HD), lambda hk, bq: (bq, hk)),
        out_shape=jax.ShapeDtypeStruct((t, N_HEADS * HD), bf16),
        scratch_shapes=[pltpu.VMEM((nsr // SEL_CHUNK, 2 * HD, lanes), bf16),
                        pltpu.VMEM((KT, lanes), f32),
                        pltpu.VMEM((KT, lanes), f32),
                        pltpu.VMEM((KT, lanes), bf16),
                        pltpu.VMEM((KT, lanes), bf16),
                        pltpu.VMEM((HD + 8, lanes), f32)]
                       + [pltpu.VMEM((1, lanes), f32)] * 4,
        compiler_params=_params(("parallel", "arbitrary")),
        name="slc_attention",
    )(projt, projn, projt, nsel_t, bias_t, ind, projt, o1)


def _layer_norm(z, g, b):
    mu = jnp.mean(z, axis=-1, keepdims=True)
    var = jnp.mean(jnp.square(z - mu), axis=-1, keepdims=True)
    return (z - mu) * lax.rsqrt(var + LN_EPS) * g + b


def _post_kernel(o2_ref, ya_ref, mb_ref, x_ref, p_ref, wao_ref, wmix_ref, wple_ref, wpg_ref,
                 g_ref, b_ref, x1_ref, r2_ref, *, alpha):
    yb = _dot(o2_ref[...], wao_ref[...])
    mixed = ya_ref[...] + jax.nn.sigmoid(mb_ref[...].astype(f32)) * yb
    z = alpha * x_ref[...] + _dot(mixed.astype(bf16), wmix_ref[...])
    x1 = _layer_norm(z, g_ref[...], b_ref[...])
    x1b = x1.astype(bf16)
    x1_ref[...] = x1b
    ple = _dot(p_ref[...].astype(bf16), wple_ref[...]) * jax.nn.sigmoid(_dot(x1b, wpg_ref[...]))
    r2_ref[...] = alpha * x1 + ple


def _post_attention(o2, ya, mab, x2d, p2d, wao, wmix, wple, wpg, g1, b1, alpha, tm):
    t = x2d.shape[0]
    row = lambda w: pl.BlockSpec((tm, w), lambda i: (i, 0))
    full = lambda a: pl.BlockSpec(a.shape, lambda i: (0, 0), pipeline_mode=pl.Buffered(1))
    return pl.pallas_call(
        functools.partial(_post_kernel, alpha=alpha),
        grid=(t // tm,),
        in_specs=[row(D_MODEL), row(D_MODEL),
                  pl.BlockSpec((tm, D_MODEL), lambda i: (i, 1)),
                  row(D_MODEL), row(PLE_DIM),
                  full(wao), full(wmix), full(wple), full(wpg), full(g1), full(b1)],
        out_specs=[row(D_MODEL), row(D_MODEL)],
        out_shape=[jax.ShapeDtypeStruct((t, D_MODEL), bf16),
                   jax.ShapeDtypeStruct((t, D_MODEL), f32)],
        compiler_params=_params(("parallel",)),
        name="post_attention",
    )(o2, ya, mab, x2d, p2d, wao, wmix, wple, wpg, g1, b1)


def _mlp_kernel(x1_ref, r2_ref, wu_ref, wd_ref, g_ref, b_ref, o_ref):
    k = pl.program_id(1)

    @pl.when(k == 0)
    def _():
        o_ref[...] = r2_ref[...]

    h = jnp.square(jnp.maximum(_dot(x1_ref[...], wu_ref[...]), 0.0))
    o_ref[...] += _dot(h.astype(bf16), wd_ref[...])

    @pl.when(k == pl.num_programs(1) - 1)
    def _():
        o_ref[...] = _layer_norm(o_ref[...], g_ref[...], b_ref[...])


def _mlp(x1b, r2, wu, wd, g2, b2, tm, fc):
    t = x1b.shape[0]
    return pl.pallas_call(
        _mlp_kernel,
        grid=(t // tm, D_FF // fc),
        in_specs=[pl.BlockSpec((tm, D_MODEL), lambda i, k: (i, 0)),
                  pl.BlockSpec((tm, D_MODEL), lambda i, k: (i, 0)),
                  pl.BlockSpec((D_MODEL, fc), lambda i, k: (0, k)),
                  pl.BlockSpec((fc, D_MODEL), lambda i, k: (k, 0)),
                  pl.BlockSpec((1, D_MODEL), lambda i, k: (0, 0)),
                  pl.BlockSpec((1, D_MODEL), lambda i, k: (0, 0))],
        out_specs=pl.BlockSpec((tm, D_MODEL), lambda i, k: (i, 0)),
        out_shape=jax.ShapeDtypeStruct((t, D_MODEL), f32),
        compiler_params=_params(("parallel", "arbitrary")),
        name="mlp",
    )(x1b, r2, wu, wd, g2, b2)


def _layer(x2d, p2d, w_in, conv_w, pe_k, w1_k, w2_k, pe_v, w1_v, w2_v, w_conv_out, w_attn_out,
           w_mix_out, ln1_g, ln1_b, w_mlp_up, w_mlp_down, w_ple, w_ple_gate, ln2_g, ln2_b,
           rel_bias, alpha):
    t = x2d.shape[0]
    assert t % 1024 == 0 and x2d.shape[1] == D_MODEL
    kvw = N_KV * HD
    o_q = 3 * D_MODEL
    o_kc = o_q + N_HEADS * HD
    o_ng = o_kc + 6 * kvw
    o_ma = o_ng + N_HEADS * 3

    w_bf = _to_bf16(w_in, LANES, "cast_w_in")

    def cols(a, n):
        return w_bf[:, a:a + n]

    tn = kvw
    conv_tiles = 3 * D_MODEL // tn
    ks_tile, kw_tile = (o_kc + 2 * kvw) // tn, (o_kc + 4 * kvw) // tn
    assert _C_KS * LANES == conv_tiles * tn and _C_KW * LANES == (conv_tiles + 1) * tn

    def nat_tile(j):
        return jnp.where(j < conv_tiles, j, jnp.where(j == conv_tiles, ks_tile, kw_tile))

    wng = jnp.pad(cols(o_ng, N_HEADS * 3).reshape(D_MODEL, N_KV, GROUP * 3), ((0, 0), (0, 0), (0, 16 - GROUP * 3)))
    w_t = jnp.concatenate(
        [cols(o_q, N_HEADS * HD), cols(o_kc + 3 * kvw, kvw), cols(o_kc + 5 * kvw, kvw),
         wng.reshape(D_MODEL, N_KV * 16),
         jnp.zeros((D_MODEL, _N_T - _R_G * LANES - N_KV * 16), bf16)], axis=1)
    row_scale = jnp.concatenate([jnp.full((N_HEADS * HD, 1), HD ** -0.5 * LOG2E, f32),
                                 jnp.ones((_N_T - N_HEADS * HD, 1), f32)], axis=0)

    xb = x2d.astype(bf16)
    projn = _proj_nat(xb, w_bf, 2048, tn, "proj_nat", _N_NAT, nat_tile)
    mab = _proj_nat(xb, cols(o_ma, 2 * D_MODEL), 2048, 1024, "proj_gate")
    kvc = _proj_chunked(xb, w_bf, 1024, 2 * kvw, o_kc // (2 * kvw))
    projt = _proj_t(w_t, xb, row_scale, 640, KT)

    ya = _mixer_a(projn, mab, conv_w, w_conv_out.astype(bf16), 512)

    half = CMP_STRIDE * HD
    w1ab = jnp.stack([jnp.concatenate([w[:half], w[half:]], axis=1) for w in (w1_k, w1_v)]).astype(bf16)
    pe2 = jnp.stack([jnp.stack([pe[:CMP_STRIDE], pe[CMP_STRIDE:]], axis=1) for pe in (pe_k, pe_v)])
    pe2 = jnp.pad(pe2, ((0, 0), (0, 0), (0, 6), (0, 0)))
    w2 = jnp.stack([w2_k, w2_v]).astype(bf16)
    cn, ct = _compress(kvc, pe2, w1ab, w2, jnp.swapaxes(w2, 1, 2))

    bias_t = _bias_tiles(rel_bias, N_TOK_BIAS, 1, 0, WINDOW // QB, WINDOW - 1, "bias_tok")
    bias_c = _bias_tiles(rel_bias, N_CMP_BIAS, CMP_STRIDE, CMP_LEN - 1, 0, -1, "bias_cmp")

    ncp = t // CMP_STRIDE
    nslc = t // SLC_LEN
    nsr = -(-nslc // SEL_CHUNK) * SEL_CHUNK
    ci = jnp.arange(ncp)[None, :] * CMP_STRIDE
    sj = jnp.arange(nslc)[:, None] * SLC_LEN
    ov_t = ((ci < sj + SLC_LEN) & (ci + CMP_LEN > sj)).astype(bf16)
    pos = jnp.arange(min(t, SEL_CHUNK * SLC_LEN))[:, None] // SLC_LEN
    ind = jnp.where(pos == jnp.arange(SEL_CHUNK)[None, :], SEL_BIG, 0.0).astype(bf16)

    o1, nsel_t = _cmp_attention(projn, projt, cn, ct, bias_c, bias_t, ov_t, t, nsr, CMP_NQ)
    o2 = _slc_attention(projn, projt, nsel_t, bias_t, ind, o1, t, SLC_NQ)

    x1b, r2 = _post_attention(o2, ya, mab, x2d, p2d, w_attn_out.astype(bf16), w_mix_out.astype(bf16),
                              w_ple.astype(bf16), w_ple_gate.astype(bf16),
                              ln1_g[None, :], ln1_b[None, :], alpha, 256)
    return _mlp(x1b, r2, w_mlp_up.astype(bf16), w_mlp_down.astype(bf16), ln2_g[None, :], ln2_b[None, :], 1024, 512)


def kernel(x, p, w_in, conv_w, cmp_pe_k, cmp_w1_k, cmp_w2_k, cmp_pe_v, cmp_w1_v, cmp_w2_v, w_conv_out, w_attn_out, w_mix_out, ln1_g, ln1_b, w_mlp_up, w_mlp_down, w_ple, w_ple_gate, ln2_g, ln2_b, rel_bias):
    bsz, t, d = x.shape
    assert bsz == 1
    depth = w_in.shape[0]
    alpha = (2 * depth) ** 0.25
    x2d = x[0]
    for i in range(depth):
        x2d = _layer(x2d, p[i, 0], w_in[i], conv_w[i], cmp_pe_k[i], cmp_w1_k[i], cmp_w2_k[i],
                     cmp_pe_v[i], cmp_w1_v[i], cmp_w2_v[i], w_conv_out[i], w_attn_out[i], w_mix_out[i],
                     ln1_g[i], ln1_b[i], w_mlp_up[i], w_mlp_down[i], w_ple[i], w_ple_gate[i],
                     ln2_g[i], ln2_b[i], rel_bias, alpha)
    return x2d[None]
```

```python
import functools
import math

import jax
import jax.numpy as jnp
from jax import lax
from jax.experimental import pallas as pl
from jax.experimental.pallas import tpu as pltpu

f32 = jnp.float32
bf16 = jnp.bfloat16
i32 = jnp.int32

D_MODEL = 2048
N_HEADS = 16
N_KV = 4
GROUP = N_HEADS // N_KV
HD = 128
CMP_LEN = 32
CMP_STRIDE = 16
CMP_HIDDEN = 2 * HD
SLC_LEN = 64
SLC_TOPN = 16
WINDOW = 512
QB = 128
D_FF = 4 * D_MODEL
PLE_DIM = 256
REL_BUCKETS = 32
REL_EXACT = REL_BUCKETS // 2
REL_MAX_DIST = 4096
LN_EPS = 1e-5
NEG = -1e30
FORCE_SCORE = 1e9
LOG2E = 1.4426950408889634

LANES = 128
KT = 512
SEL_BIG = 2.0 ** 100
SEL_CHUNK = 128
SLC_NQ = 1
CMP_NQ = 2
N_TOK_BIAS = 25
N_CMP_BIAS = 40
VMEM_LIMIT = 56 * 1024 * 1024

_C_BG, _C_CG, _C_HX = 0, 16, 32
_C_KS, _C_KW = 48, 52
_N_NAT = 56 * LANES
_R_Q, _R_VS, _R_VW, _R_G = 0, 16, 20, 24
_N_T = 25 * LANES


def _dot(a, b):
    return jnp.dot(a, b, preferred_element_type=f32)


def _dot_nt(a, b):
    return lax.dot_general(a, b, (((1,), (1,)), ((), ())), preferred_element_type=f32)


def _params(sem, vmem=VMEM_LIMIT):
    return pltpu.CompilerParams(dimension_semantics=sem, vmem_limit_bytes=vmem)


def _mm_kernel(x_ref, w_ref, o_ref):
    o_ref[...] = _dot(x_ref[...], w_ref[...]).astype(o_ref.dtype)


def _proj_nat(xb, w, tm, tn, name, n=None, col_tile=lambda j: j):
    m, k = xb.shape
    n = w.shape[1] if n is None else n
    return pl.pallas_call(
        _mm_kernel,
        grid=(m // tm, n // tn),
        in_specs=[pl.BlockSpec((tm, k), lambda i, j: (i, 0)),
                  pl.BlockSpec((k, tn), lambda i, j: (0, col_tile(j)))],
        out_specs=pl.BlockSpec((tm, tn), lambda i, j: (i, j)),
        out_shape=jax.ShapeDtypeStruct((m, n), bf16),
        compiler_params=_params(("parallel", "arbitrary")),
        name=name,
    )(xb, w)


def _mmt_kernel(w_ref, x_ref, s_ref, o_ref):
    acc = lax.dot_general(w_ref[...], x_ref[...], (((0,), (1,)), ((), ())), preferred_element_type=f32)
    o_ref[0] = (acc * s_ref[...]).astype(o_ref.dtype)


def _proj_t(w, xb, row_scale, tm, tn):
    k, n = w.shape
    t = xb.shape[0]
    return pl.pallas_call(
        _mmt_kernel,
        grid=(n // tm, t // tn),
        in_specs=[pl.BlockSpec((k, tm), lambda i, j: (0, i)),
                  pl.BlockSpec((tn, k), lambda i, j: (j, 0)),
                  pl.BlockSpec((tm, 1), lambda i, j: (i, 0))],
        out_specs=pl.BlockSpec((1, tm, tn), lambda i, j: (j, i, 0)),
        out_shape=jax.ShapeDtypeStruct((t // tn, n, tn), bf16),
        compiler_params=_params(("parallel", "arbitrary")),
        name="proj_t",
    )(w, xb, row_scale)


def _cast_kernel(x_ref, o_ref):
    o_ref[...] = x_ref[...].astype(o_ref.dtype)


def _to_bf16(w, tm, name):
    r, c = w.shape
    return pl.pallas_call(
        _cast_kernel,
        grid=(r // tm,),
        in_specs=[pl.BlockSpec((tm, c), lambda i: (i, 0))],
        out_specs=pl.BlockSpec((tm, c), lambda i: (i, 0)),
        out_shape=jax.ShapeDtypeStruct((r, c), bf16),
        compiler_params=_params(("parallel",)),
        name=name,
    )(w)


def _ya_kernel(bg_ref, cg_ref, hx_ref, cgh_ref, hxh_ref, cw_ref, w_ref, ma_ref, o_ref, a_ref):
    i = pl.program_id(0)
    tm = bg_ref.shape[0]
    cc = 512
    row = lax.broadcasted_iota(i32, (tm, cc), 0)
    for c in range(D_MODEL // cc):
        sl = slice(c * cc, (c + 1) * cc)
        u = cg_ref[:, sl].astype(f32) * hx_ref[:, sl].astype(f32)
        uh = cgh_ref[:, sl].astype(f32) * hxh_ref[:, sl].astype(f32)
        uh = jnp.where(i > 0, uh, 0.0)
        u1 = jnp.where(row == 0, uh[15:16, :], pltpu.roll(u, 1, 0))
        u2 = jnp.where(row == 0, uh[14:15, :], jnp.where(row == 1, uh[15:16, :], pltpu.roll(u, 2, 0)))
        cw = cw_ref[:, sl]
        conv = cw[0:1, :] * u2 + cw[1:2, :] * u1 + cw[2:3, :] * u
        a_ref[:, sl] = (bg_ref[:, sl].astype(f32) * conv).astype(bf16)
    y = _dot(a_ref[...], w_ref[...])
    o_ref[...] = jax.nn.sigmoid(ma_ref[...].astype(f32)) * y


def _mixer_a(projn, mab, conv_w, w_out, tm):
    t = projn.shape[0]
    hb = tm // 16
    wide = lambda c: pl.BlockSpec((tm, D_MODEL), lambda i, c=c: (i, c))
    halo = lambda c: pl.BlockSpec((16, D_MODEL), lambda i, c=c: (jnp.maximum(i * hb - 1, 0), c))
    return pl.pallas_call(
        _ya_kernel,
        grid=(t // tm,),
        in_specs=[wide(0), wide(1), wide(2), halo(1), halo(2),
                  pl.BlockSpec((3, D_MODEL), lambda i: (0, 0)),
                  pl.BlockSpec((D_MODEL, D_MODEL), lambda i: (0, 0)),
                  wide(0)],
        out_specs=pl.BlockSpec((tm, D_MODEL), lambda i: (i, 0)),
        out_shape=jax.ShapeDtypeStruct((t, D_MODEL), f32),
        scratch_shapes=[pltpu.VMEM((tm, D_MODEL), bf16)],
        compiler_params=_params(("parallel",)),
        name="mixer_a",
    )(projn, projn, projn, projn, projn, conv_w, w_out, mab)


def _compress_kernel(x_ref, w_ref, pe_ref, w2_ref, w2t_ref, on_ref, ot_ref, acc_ref, pet_ref):
    l = pl.program_id(1)
    nl = pl.num_programs(1)
    n = x_ref.shape[0]

    @pl.when(l == 0)
    def _():
        acc_ref[...] = jnp.zeros_like(acc_ref)
        pet_ref[...] = jnp.zeros_like(pet_ref)

    wblk = w_ref[0]
    for h in range(N_KV):
        acc_ref[h] += _dot(x_ref[:, h * HD:(h + 1) * HD], wblk)
    pet_ref[...] += _dot(pe_ref[0, 0].astype(bf16), wblk)

    @pl.when(l == nl - 1)
    def _():
        r = pet_ref[...]
        peterm = r[0:1, :CMP_HIDDEN] + r[1:2, CMP_HIDDEN:]
        for h in range(N_KV):
            ab = acc_ref[h]
            hid = ab[:, :CMP_HIDDEN] + pltpu.roll(ab[:, CMP_HIDDEN:], n - 1, 0) + peterm
            gl = jax.nn.gelu(hid).astype(bf16)
            on_ref[0, h] = _dot(gl, w2_ref[0]).astype(bf16)
            ot_ref[0, h] = _dot_nt(w2t_ref[0], gl).astype(bf16)


def _chunked_mm_kernel(x_ref, w_ref, o_ref, acc_ref):
    acc = _dot(x_ref[...], w_ref[...])
    n = w_ref.shape[1]
    rows = o_ref.shape[0]
    for c in range(n // LANES):
        acc_ref[c] = acc[:, c * LANES:(c + 1) * LANES]
    for l in range(CMP_STRIDE):
        for c in range(n // LANES):
            o_ref[:, l * n + c * LANES:l * n + (c + 1) * LANES] = (
                acc_ref[c, pl.ds(l, rows, stride=CMP_STRIDE), :].astype(o_ref.dtype))


def _proj_chunked(xb, w, tm, n, col_tile):
    m, k = xb.shape
    return pl.pallas_call(
        _chunked_mm_kernel,
        grid=(m // tm,),
        in_specs=[pl.BlockSpec((tm, k), lambda i: (i, 0)),
                  pl.BlockSpec((k, n), lambda i: (0, col_tile))],
        out_specs=pl.BlockSpec((tm // CMP_STRIDE, CMP_STRIDE * n), lambda i: (i, 0)),
        out_shape=jax.ShapeDtypeStruct((m // CMP_STRIDE, CMP_STRIDE * n), bf16),
        scratch_shapes=[pltpu.VMEM((n // LANES, tm, LANES), f32)],
        compiler_params=_params(("parallel",)),
        name="proj_cmp",
    )(xb, w)


def _compress(xr, pe2, w1ab, w2, w2t):
    n = xr.shape[0]
    return pl.pallas_call(
        _compress_kernel,
        grid=(2, CMP_STRIDE),
        in_specs=[pl.BlockSpec((n, N_KV * HD), lambda kv, l: (0, 2 * l + kv)),
                  pl.BlockSpec((1, HD, 2 * CMP_HIDDEN), lambda kv, l: (kv, l, 0)),
                  pl.BlockSpec((1, 1, 8, HD), lambda kv, l: (kv, l, 0, 0)),
                  pl.BlockSpec((1, CMP_HIDDEN, HD), lambda kv, l: (kv, 0, 0)),
                  pl.BlockSpec((1, HD, CMP_HIDDEN), lambda kv, l: (kv, 0, 0))],
        out_specs=[pl.BlockSpec((1, N_KV, n, HD), lambda kv, l: (kv, 0, 0, 0)),
                   pl.BlockSpec((1, N_KV, HD, n), lambda kv, l: (kv, 0, 0, 0))],
        out_shape=[jax.ShapeDtypeStruct((2, N_KV, n, HD), bf16),
                   jax.ShapeDtypeStruct((2, N_KV, HD, n), bf16)],
        scratch_shapes=[pltpu.VMEM((N_KV, n, 2 * CMP_HIDDEN), f32),
                        pltpu.VMEM((8, 2 * CMP_HIDDEN), f32)],
        compiler_params=_params(("arbitrary", "arbitrary")),
        name="compress",
    )(xr, w1ab, pe2, w2, w2t)


def _rel_bucket(dist):
    n = jnp.maximum(dist, 0)
    nf = jnp.maximum(n, 1).astype(f32)
    large = REL_EXACT + (jnp.log(nf / REL_EXACT) / math.log(REL_MAX_DIST / REL_EXACT)
                         * (REL_BUCKETS - REL_EXACT)).astype(i32)
    large = jnp.minimum(large, REL_BUCKETS - 1)
    return jnp.where(n < REL_EXACT, n, large)


def _bias_kernel(tbl_ref, o_ref, *, kstride, koff, n_reg, edge_o, edge_max):
    hk = pl.program_id(0)
    o = pl.program_id(1)
    is_edge = o == n_reg
    ki = lax.broadcasted_iota(i32, (LANES, LANES), 0)
    qi = lax.broadcasted_iota(i32, (LANES, LANES), 1)
    dist = jnp.where(is_edge, edge_o, o) * LANES + qi - kstride * ki - koff
    valid = (dist >= 0) & (dist <= jnp.where(is_edge, edge_max, 2 ** 30))
    bucket = _rel_bucket(dist)
    for g in range(GROUP):
        row = jnp.broadcast_to(tbl_ref[pl.ds(hk * GROUP + g, 1), :], (LANES, LANES))
        acc = jnp.take_along_axis(row, bucket, axis=1)
        o_ref[0, 0, :, g * LANES:(g + 1) * LANES] = jnp.where(valid, acc * LOG2E, NEG)


def _bias_tiles(rel_bias, n_reg, kstride, koff, edge_o, edge_max, name):
    return pl.pallas_call(
        functools.partial(_bias_kernel, kstride=kstride, koff=koff, n_reg=n_reg, edge_o=edge_o,
                          edge_max=edge_max),
        grid=(N_KV, n_reg + 1),
        in_specs=[pl.BlockSpec((N_HEADS, LANES), lambda hk, o: (0, 0))],
        out_specs=pl.BlockSpec((1, 1, LANES, GROUP * LANES), lambda hk, o: (hk, o, 0, 0)),
        out_shape=jax.ShapeDtypeStruct((N_KV, n_reg + 1, LANES, GROUP * LANES), f32),
        compiler_params=_params(("parallel", "arbitrary")),
        name=name,
    )(jnp.pad(rel_bias.astype(f32).T, ((0, 0), (0, LANES - REL_BUCKETS))))


def _query_lanes(q_ref, nq):
    qb = q_ref[0]
    return jnp.concatenate([qb[g * HD:(g + 1) * HD, h * QB:(h + 1) * QB]
                            for h in range(nq) for g in range(GROUP)], axis=1)


def _cmp_kernel(q_ref, kc_ref, vct_ref, bc_ref, ov_ref, g_ref, *rest, n_sel, n_steps, n_branch, nq):
    nw = nq + WINDOW // QB
    kws, vws = rest[:nw], rest[nw:2 * nw]
    bt_ref, o_ref, ns_ref = rest[2 * nw:]
    step_id = pl.program_id(1)
    b0 = step_id * nq
    nsr = ns_ref.shape[1]
    gl = GROUP * LANES
    qt = _query_lanes(q_ref, nq)
    gate = jax.nn.sigmoid(g_ref[0].astype(f32))
    cmp_per_q = QB // CMP_STRIDE
    q_per_tile = LANES // cmp_per_q

    def core(nt, rows):
        keys = nt * LANES
        s = _dot(kc_ref[0, 0, 0:keys, :], qt)
        col_parts = []
        for h in range(nq):
            parts = []
            for c in range(nt):
                o = b0 + h - q_per_tile * c
                idx = jnp.where(o < 0, N_CMP_BIAS, jnp.minimum(o, N_CMP_BIAS - 1))
                parts.append(s[c * LANES:(c + 1) * LANES, h * gl:(h + 1) * gl] + bc_ref[0, idx])
            col_parts.append(jnp.concatenate(parts, axis=0) if nt > 1 else parts[0])
        s = jnp.concatenate(col_parts, axis=1) if nq > 1 else col_parts[0]
        m = jnp.max(s, axis=0, keepdims=True)
        e = jnp.exp2(s - m).astype(bf16)
        lhs = jnp.concatenate([vct_ref[0, 0, :, 0:keys], jnp.ones((16, keys), bf16), ov_ref[0:rows, 0:keys]], axis=0)
        acc = _dot(lhs, e)
        inv = jnp.where(m > 0.5 * NEG, 1.0 / acc[HD:HD + 1], 0.0)
        oct_ = acc[0:HD] * inv
        imp_g = acc[HD + 16:] * inv

        sw = _dot(jnp.concatenate([r[...] for r in kws], axis=0), qt)
        col_parts = []
        for h in range(nq):
            parts = []
            for jw in range(nw):
                dl = h + WINDOW // QB - jw
                if dl < 0 or dl > WINDOW // QB:
                    parts.append(jnp.full((LANES, gl), NEG, f32))
                    continue
                tile = N_TOK_BIAS if dl == WINDOW // QB else dl
                before_start = jnp.where(b0 + h - dl >= 0, 0.0, NEG)
                parts.append(sw[jw * LANES:(jw + 1) * LANES, h * gl:(h + 1) * gl] + bt_ref[0, tile] + before_start)
            col_parts.append(jnp.concatenate(parts, axis=0))
        sw = jnp.concatenate(col_parts, axis=1) if nq > 1 else col_parts[0]
        pw = jnp.exp2(sw - jnp.max(sw, axis=0, keepdims=True)).astype(bf16)
        vw = jnp.concatenate([r[0] for r in vws], axis=1)
        accw = _dot(jnp.concatenate([vw, jnp.ones((8, nw * LANES), bf16)], axis=0), pw)
        owt = accw[0:HD] / accw[HD:HD + 1]
        for h in range(nq):
            qs = slice(h * QB, (h + 1) * QB)
            for g in range(GROUP):
                ls = slice(h * gl + g * LANES, h * gl + (g + 1) * LANES)
                both = oct_[:, ls] * gate[g * 3:g * 3 + 1, qs] + owt[:, ls] * gate[g * 3 + 2:g * 3 + 3, qs]
                o_ref[qs, g * HD:(g + 1) * HD] = both.T.astype(o_ref.dtype)

        imps = []
        for h in range(nq):
            ig = None
            for g in range(GROUP):
                part = imp_g[:, h * gl + g * LANES:h * gl + (g + 1) * LANES]
                ig = part if ig is None else ig + part
            imps.append(ig)
        imp = jnp.concatenate(imps, axis=1) if nq > 1 else imps[0]

        j = lax.broadcasted_iota(i32, (rows, nq * QB), 0)
        tq = b0 * QB + lax.broadcasted_iota(i32, (rows, nq * QB), 1)
        jt = jnp.right_shift(tq, SLC_LEN.bit_length() - 1)
        forced = (j == 0) | (j == jt) | (j == jt - 1)
        future = j * SLC_LEN > tq
        taken = -3e38
        work = jnp.where(forced, taken, jnp.where(future, -1.0, imp))
        jf = j.astype(f32)
        for _ in range(n_sel - 3):
            mx = jnp.max(work, axis=0, keepdims=True)
            jm = jnp.min(jnp.where(work == mx, jf, 1e9), axis=0, keepdims=True)
            work = jnp.where(jf == jm, taken, work)
        ns_ref[0, 0:rows, :] = jnp.where(future | (work != taken), 1.0, 0.0).astype(bf16)
        if nsr > rows:
            ns_ref[0, rows:, :] = jnp.ones((nsr - rows, nq * QB), bf16)

    per_branch = n_steps // n_branch
    for k in range(n_branch):
        last_b = (k + 1) * per_branch * nq - 1
        nt = last_b // q_per_tile + 1
        rows = (last_b + 1) * (QB // SLC_LEN)
        pl.when(step_id // per_branch == k)(functools.partial(core, nt, rows))


def _cmp_attention(projn, projt, cn, ct, bias_c, bias_t, ov_t, t, nsr, nq):
    nb = t // QB
    ncp = t // CMP_STRIDE
    nslc = t // SLC_LEN
    per_kt = KT // QB
    per = per_kt // nq
    nw = nq + WINDOW // QB
    n_sel = min(SLC_TOPN, nslc)
    assert n_sel > 3
    n_steps = nb // nq
    n_branch = 4 if nb % 64 == 0 else 1
    resident = dict(pipeline_mode=pl.Buffered(1))

    def kw_spec(jw):
        return pl.BlockSpec((QB, HD), lambda hk, s, jw=jw: (jnp.maximum(s * nq - WINDOW // QB + jw, 0), _C_KW + hk))

    def vw_spec(jw):
        def imap(hk, s, jw=jw):
            bb = jnp.maximum(s * nq - WINDOW // QB + jw, 0)
            return (bb // per_kt, _R_VW + hk, bb % per_kt)
        return pl.BlockSpec((1, HD, QB), imap)

    return pl.pallas_call(
        functools.partial(_cmp_kernel, n_sel=n_sel, n_steps=n_steps, n_branch=n_branch, nq=nq),
        grid=(N_KV, n_steps),
        in_specs=[pl.BlockSpec((1, GROUP * HD, nq * QB), lambda hk, s: (s // per, hk, s % per)),
                  pl.BlockSpec((1, 1, ncp, HD), lambda hk, s: (0, hk, 0, 0), **resident),
                  pl.BlockSpec((1, 1, HD, ncp), lambda hk, s: (1, hk, 0, 0), **resident),
                  pl.BlockSpec((1, N_CMP_BIAS + 1, LANES, GROUP * LANES), lambda hk, s: (hk, 0, 0, 0), **resident),
                  pl.BlockSpec((nslc, ncp), lambda hk, s: (0, 0), **resident),
                  pl.BlockSpec((1, 16, nq * QB), lambda hk, s: (s // per, _R_G * 8 + hk, s % per))]
                 + [kw_spec(jw) for jw in range(nw)]
                 + [vw_spec(jw) for jw in range(nw)]
                 + [pl.BlockSpec((1, N_TOK_BIAS + 1, LANES, GROUP * LANES), lambda hk, s: (hk, 0, 0, 0), **resident)],
        out_specs=[pl.BlockSpec((nq * QB, GROUP * HD), lambda hk, s: (s, hk)),
                   pl.BlockSpec((1, nsr, nq * QB), lambda hk, s: (hk, 0, s))],
        out_shape=[jax.ShapeDtypeStruct((t, N_HEADS * HD), f32),
                   jax.ShapeDtypeStruct((N_KV, nsr, t), bf16)],
        compiler_params=_params(("parallel", "arbitrary")),
        name="cmp_attention",
    )(projt, cn, ct, bias_c, ov_t, projt, *([projn] * nw), *([projt] * nw), bias_t)


def _slc_kernel(q_ref, ks_ref, vst_ref, ns_ref, bt_ref, ind_ref, g_ref, o1_ref, o_ref,
                qa_ref, s0_ref, s1_ref, p0_ref, p1_ref, acc_ref, m_ref, al_ref, mx_ref, off_ref, *, nq):
    b0 = pl.program_id(1) * nq
    gl = GROUP * LANES
    qt = _query_lanes(q_ref, nq)
    gate = jax.nn.sigmoid(g_ref[0].astype(f32))

    n_chunks = qa_ref.shape[0]
    for jc in range(n_chunks):
        nsj = -ns_ref[0, jc * SEL_CHUNK:(jc + 1) * SEL_CHUNK, :]
        qa_ref[jc, 0:HD, :] = qt
        qa_ref[jc, HD:, :] = jnp.concatenate([nsj[:, h * QB:(h + 1) * QB]
                                              for h in range(nq) for _ in range(GROUP)], axis=1)

    n_tiles = vst_ref.shape[0]
    per_chunk = SEL_CHUNK * SLC_LEN // KT
    sub = KT // LANES

    far_bias = jnp.concatenate([bt_ref[0, N_TOK_BIAS - 1, 0:1, :]] * nq, axis=1)
    ones_rows = jnp.ones((8, KT), bf16)

    def scores(kt, s_ref, far):
        jc = kt // per_chunk
        r0 = pl.multiple_of(kt * KT, KT)
        i0 = pl.multiple_of((kt - jc * per_chunk) * KT, KT)
        kaug = jnp.concatenate([ks_ref[pl.ds(r0, KT), :], ind_ref[pl.ds(i0, KT), :]], axis=1)
        s = _dot(kaug, qa_ref[jc])
        if far:
            s_ref[...] = s
            mx_ref[...] = jnp.max(s, axis=0, keepdims=True) + far_bias
            off_ref[...] = far_bias
            return
        mxs = []
        for h in range(nq):
            cols = slice(h * gl, (h + 1) * gl)
            mx = None
            for jj in range(sub):
                rows = slice(jj * LANES, (jj + 1) * LANES)
                dl = b0 + h - (kt * sub + jj)
                sj = s[rows, cols] + bt_ref[0, jnp.clip(dl, 0, N_TOK_BIAS - 1)]
                s_ref[rows, cols] = sj
                mj = jnp.max(sj, axis=0, keepdims=True)
                mx = mj if mx is None else jnp.maximum(mx, mj)
            mxs.append(mx)
        mx_ref[...] = jnp.concatenate(mxs, axis=1) if nq > 1 else mxs[0]
        off_ref[...] = jnp.zeros_like(off_ref)

    def values(kt):
        return jnp.concatenate([vst_ref[kt], ones_rows], axis=0)

    def step(kt, s_cur, p_cur, s_nxt, p_prev, far):
        m_old = m_ref[...]
        m_new = jnp.maximum(m_old, mx_ref[...])
        shift = m_new - off_ref[...]
        al_old = al_ref[...]
        scores(jnp.minimum(kt + 1, n_tiles - 1), s_nxt, far)
        acc_ref[...] = al_old * acc_ref[...] + _dot(values(jnp.maximum(kt - 1, 0)), p_prev[...])
        p_cur[...] = jnp.exp2(s_cur[...] - shift).astype(bf16)
        m_ref[...] = m_new
        al_ref[...] = jnp.exp2(m_old - m_new)

    acc_ref[...] = jnp.zeros_like(acc_ref)
    m_ref[...] = jnp.full_like(m_ref, NEG)
    al_ref[...] = jnp.ones_like(al_ref)
    p1_ref[...] = jnp.zeros_like(p1_ref)
    scores(0, s0_ref, False)

    n_pairs = (b0 + nq - 1) // (2 * KT // QB) + 1
    n_far = jnp.maximum((b0 - (N_TOK_BIAS - 2)) // sub, 0)
    n_far_pairs = jnp.maximum((n_far - 1) // 2, 0)

    def pair(far):
        def body(i, carry):
            step(2 * i, s0_ref, p0_ref, s1_ref, p1_ref, far)
            step(2 * i + 1, s1_ref, p1_ref, s0_ref, p0_ref, far)
            return carry
        return body

    lax.fori_loop(0, n_far_pairs, pair(True), 0)
    lax.fori_loop(n_far_pairs, n_pairs, pair(False), 0)
    acc = al_ref[...] * acc_ref[...] + _dot(values(2 * n_pairs - 1), p1_ref[...])

    ost = acc[0:HD] / acc[HD:HD + 1]
    for h in range(nq):
        qs = slice(h * QB, (h + 1) * QB)
        for g in range(GROUP):
            ls = slice(h * gl + g * LANES, h * gl + (g + 1) * LANES)
            ds = slice(g * HD, (g + 1) * HD)
            o_ref[qs, ds] = ((ost[:, ls] * gate[g * 3 + 1:g * 3 + 2, qs]).T + o1_ref[qs, ds]).astype(o_ref.dtype)


def _slc_attention(projn, projt, nsel_t, bias_t, ind, o1, t, nq):
    nb = t // QB
    per = KT // QB // nq
    nsr = nsel_t.shape[1]
    lanes = nq * GROUP * LANES
    resident = dict(pipeline_mode=pl.Buffered(1))
    return pl.pallas_call(
        functools.partial(_slc_kernel, nq=nq),
        grid=(N_KV, nb // nq),
        in_specs=[pl.BlockSpec((1, GROUP * HD, nq * QB), lambda hk, bq: (bq // per, hk, bq % per)),
                  pl.BlockSpec((t, HD), lambda hk, bq: (0, _C_KS + hk), **resident),
                  pl.BlockSpec((t // KT, HD, KT), lambda hk, bq: (0, _R_VS + hk, 0), **resident),
                  pl.BlockSpec((1, nsr, nq * QB), lambda hk, bq: (hk, 0, bq)),
                  pl.BlockSpec((1, N_TOK_BIAS + 1, LANES, GROUP * LANES), lambda hk, bq: (hk, 0, 0, 0), **resident),
                  pl.BlockSpec(ind.shape, lambda hk, bq: (0, 0), **resident),
                  pl.BlockSpec((1, 16, nq * QB), lambda hk, bq: (bq // per, _R_G * 8 + hk, bq % per)),
                  pl.BlockSpec((nq * QB, GROUP * HD), lambda hk, bq: (bq, hk))],
        out_specs=pl.BlockSpec((nq * QB, GROUP * HD), lambda hk, bq: (bq, hk)),
        out_shape=jax.ShapeDtypeStruct((t, N_HEADS * HD), bf16),
        scratch_shapes=[pltpu.VMEM((nsr // SEL_CHUNK, 2 * HD, lanes), bf16),
                        pltpu.VMEM((KT, lanes), f32),
                        pltpu.VMEM((KT, lanes), f32),
                        pltpu.VMEM((KT, lanes), bf16),
                        pltpu.VMEM((KT, lanes), bf16),
                        pltpu.VMEM((HD + 8, lanes), f32)]
                       + [pltpu.VMEM((1, lanes), f32)] * 4,
        compiler_params=_params(("parallel", "arbitrary")),
        name="slc_attention",
    )(projt, projn, projt, nsel_t, bias_t, ind, projt, o1)


def _layer_norm(z, g, b):
    mu = jnp.mean(z, axis=-1, keepdims=True)
    var = jnp.mean(jnp.square(z - mu), axis=-1, keepdims=True)
    return (z - mu) * lax.rsqrt(var + LN_EPS) * g + b


def _post_kernel(o2_ref, ya_ref, mb_ref, x_ref, p_ref, wao_ref, wmix_ref, wple_ref, wpg_ref,
                 g_ref, b_ref, x1_ref, r2_ref, *, alpha):
    yb = _dot(o2_ref[...], wao_ref[...])
    mixed = ya_ref[...] + jax.nn.sigmoid(mb_ref[...].astype(f32)) * yb
    z = alpha * x_ref[...] + _dot(mixed.astype(bf16), wmix_ref[...])
    x1 = _layer_norm(z, g_ref[...], b_ref[...])
    x1b = x1.astype(bf16)
    x1_ref[...] = x1b
    ple = _dot(p_ref[...].astype(bf16), wple_ref[...]) * jax.nn.sigmoid(_dot(x1b, wpg_ref[...]))
    r2_ref[...] = alpha * x1 + ple


def _post_attention(o2, ya, mab, x2d, p2d, wao, wmix, wple, wpg, g1, b1, alpha, tm):
    t = x2d.shape[0]
    row = lambda w: pl.BlockSpec((tm, w), lambda i: (i, 0))
    full = lambda a: pl.BlockSpec(a.shape, lambda i: (0, 0), pipeline_mode=pl.Buffered(1))
    return pl.pallas_call(
        functools.partial(_post_kernel, alpha=alpha),
        grid=(t // tm,),
        in_specs=[row(D_MODEL), row(D_MODEL),
                  pl.BlockSpec((tm, D_MODEL), lambda i: (i, 1)),
                  row(D_MODEL), row(PLE_DIM),
                  full(wao), full(wmix), full(wple), full(wpg), full(g1), full(b1)],
        out_specs=[row(D_MODEL), row(D_MODEL)],
        out_shape=[jax.ShapeDtypeStruct((t, D_MODEL), bf16),
                   jax.ShapeDtypeStruct((t, D_MODEL), f32)],
        compiler_params=_params(("parallel",)),
        name="post_attention",
    )(o2, ya, mab, x2d, p2d, wao, wmix, wple, wpg, g1, b1)


def _mlp_kernel(x1_ref, r2_ref, wu_ref, wd_ref, g_ref, b_ref, o_ref):
    k = pl.program_id(1)

    @pl.when(k == 0)
    def _():
        o_ref[...] = r2_ref[...]

    h = jnp.square(jnp.maximum(_dot(x1_ref[...], wu_ref[...]), 0.0))
    o_ref[...] += _dot(h.astype(bf16), wd_ref[...])

    @pl.when(k == pl.num_programs(1) - 1)
    def _():
        o_ref[...] = _layer_norm(o_ref[...], g_ref[...], b_ref[...])


def _mlp(x1b, r2, wu, wd, g2, b2, tm, fc):
    t = x1b.shape[0]
    return pl.pallas_call(
        _mlp_kernel,
        grid=(t // tm, D_FF // fc),
        in_specs=[pl.BlockSpec((tm, D_MODEL), lambda i, k: (i, 0)),
                  pl.BlockSpec((tm, D_MODEL), lambda i, k: (i, 0)),
                  pl.BlockSpec((D_MODEL, fc), lambda i, k: (0, k)),
                  pl.BlockSpec((fc, D_MODEL), lambda i, k: (k, 0)),
                  pl.BlockSpec((1, D_MODEL), lambda i, k: (0, 0)),
                  pl.BlockSpec((1, D_MODEL), lambda i, k: (0, 0))],
        out_specs=pl.BlockSpec((tm, D_MODEL), lambda i, k: (i, 0)),
        out_shape=jax.ShapeDtypeStruct((t, D_MODEL), f32),
        compiler_params=_params(("parallel", "arbitrary")),
        name="mlp",
    )(x1b, r2, wu, wd, g2, b2)


def _layer(x2d, p2d, w_in, conv_w, pe_k, w1_k, w2_k, pe_v, w1_v, w2_v, w_conv_out, w_attn_out,
           w_mix_out, ln1_g, ln1_b, w_mlp_up, w_mlp_down, w_ple, w_ple_gate, ln2_g, ln2_b,
           rel_bias, alpha):
    t = x2d.shape[0]
    assert t % 1024 == 0 and x2d.shape[1] == D_MODEL
    kvw = N_KV * HD
    o_q = 3 * D_MODEL
    o_kc = o_q + N_HEADS * HD
    o_ng = o_kc + 6 * kvw
    o_ma = o_ng + N_HEADS * 3

    w_bf = _to_bf16(w_in, LANES, "cast_w_in")

    def cols(a, n):
        return w_bf[:, a:a + n]

    tn = kvw
    conv_tiles = 3 * D_MODEL // tn
    ks_tile, kw_tile = (o_kc + 2 * kvw) // tn, (o_kc + 4 * kvw) // tn
    assert _C_KS * LANES == conv_tiles * tn and _C_KW * LANES == (conv_tiles + 1) * tn

    def nat_tile(j):
        return jnp.where(j < conv_tiles, j, jnp.where(j == conv_tiles, ks_tile, kw_tile))

    wng = jnp.pad(cols(o_ng, N_HEADS * 3).reshape(D_MODEL, N_KV, GROUP * 3), ((0, 0), (0, 0), (0, 16 - GROUP * 3)))
    w_t = jnp.concatenate(
        [cols(o_q, N_HEADS * HD), cols(o_kc + 3 * kvw, kvw), cols(o_kc + 5 * kvw, kvw),
         wng.reshape(D_MODEL, N_KV * 16),
         jnp.zeros((D_MODEL, _N_T - _R_G * LANES - N_KV * 16), bf16)], axis=1)
    row_scale = jnp.concatenate([jnp.full((N_HEADS * HD, 1), HD ** -0.5 * LOG2E, f32),
                                 jnp.ones((_N_T - N_HEADS * HD, 1), f32)], axis=0)

    xb = x2d.astype(bf16)
    projn = _proj_nat(xb, w_bf, 2048, tn, "proj_nat", _N_NAT, nat_tile)
    mab = _proj_nat(xb, cols(o_ma, 2 * D_MODEL), 2048, 1024, "proj_gate")
    kvc = _proj_chunked(xb, w_bf, 1024, 2 * kvw, o_kc // (2 * kvw))
    projt = _proj_t(w_t, xb, row_scale, 640, KT)

    ya = _mixer_a(projn, mab, conv_w, w_conv_out.astype(bf16), 512)

    half = CMP_STRIDE * HD
    w1ab = jnp.stack([jnp.concatenate([w[:half], w[half:]], axis=1) for w in (w1_k, w1_v)]).astype(bf16)
    pe2 = jnp.stack([jnp.stack([pe[:CMP_STRIDE], pe[CMP_STRIDE:]], axis=1) for pe in (pe_k, pe_v)])
    pe2 = jnp.pad(pe2, ((0, 0), (0, 0), (0, 6), (0, 0)))
    w2 = jnp.stack([w2_k, w2_v]).astype(bf16)
    cn, ct = _compress(kvc, pe2, w1ab, w2, jnp.swapaxes(w2, 1, 2))

    bias_t = _bias_tiles(rel_bias, N_TOK_BIAS, 1, 0, WINDOW // QB, WINDOW - 1, "bias_tok")
    bias_c = _bias_tiles(rel_bias, N_CMP_BIAS, CMP_STRIDE, CMP_LEN - 1, 0, -1, "bias_cmp")

    ncp = t // CMP_STRIDE
    nslc = t // SLC_LEN
    nsr = -(-nslc // SEL_CHUNK) * SEL_CHUNK
    ci = jnp.arange(ncp)[None, :] * CMP_STRIDE
    sj = jnp.arange(nslc)[:, None] * SLC_LEN
    ov_t = ((ci < sj + SLC_LEN) & (ci + CMP_LEN > sj)).astype(bf16)
    pos = jnp.arange(min(t, SEL_CHUNK * SLC_LEN))[:, None] // SLC_LEN
    ind = jnp.where(pos == jnp.arange(SEL_CHUNK)[None, :], SEL_BIG, 0.0).astype(bf16)

    o1, nsel_t = _cmp_attention(projn, projt, cn, ct, bias_c, bias_t, ov_t, t, nsr, CMP_NQ)
    o2 = _slc_attention(projn, projt, nsel_t, bias_t, ind, o1, t, SLC_NQ)

    x1b, r2 = _post_attention(o2, ya, mab, x2d, p2d, w_attn_out.astype(bf16), w_mix_out.astype(bf16),
                              w_ple.astype(bf16), w_ple_gate.astype(bf16),
                              ln1_g[None, :], ln1_b[None, :], alpha, 256)
    return _mlp(x1b, r2, w_mlp_up.astype(bf16), w_mlp_down.astype(bf16), ln2_g[None, :], ln2_b[None, :], 1024, 512)


def kernel(x, p, w_in, conv_w, cmp_pe_k, cmp_w1_k, cmp_w2_k, cmp_pe_v, cmp_w1_v, cmp_w2_v, w_conv_out, w_attn_out, w_mix_out, ln1_g, ln1_b, w_mlp_up, w_mlp_down, w_ple, w_ple_gate, ln2_g, ln2_b, rel_bias):
    bsz, t, d = x.shape
    assert bsz == 1
    depth = w_in.shape[0]
    alpha = (2 * depth) ** 0.25
    x2d = x[0]
    for i in range(depth):
        x2d = _layer(x2d, p[i, 0], w_in[i], conv_w[i], cmp_pe_k[i], cmp_w1_k[i], cmp_w2_k[i],
                     cmp_pe_v[i], cmp_w1_v[i], cmp_w2_v[i], w_conv_out[i], w_attn_out[i], w_mix_out[i],
                     ln1_g[i], ln1_b[i], w_mlp_up[i], w_mlp_down[i], w_ple[i], w_ple_gate[i],
                     ln2_g[i], ln2_b[i], rel_bias, alpha)
    return x2d[None]
```

```python
import functools
import math

import jax
import jax.numpy as jnp
from jax import lax
from jax.experimental import pallas as pl
from jax.experimental.pallas import tpu as pltpu

f32 = jnp.float32
bf16 = jnp.bfloat16
i32 = jnp.int32

D_MODEL = 2048
N_HEADS = 16
N_KV = 4
GROUP = N_HEADS // N_KV
HD = 128
CMP_LEN = 32
CMP_STRIDE = 16
CMP_HIDDEN = 2 * HD
SLC_LEN = 64
SLC_TOPN = 16
WINDOW = 512
QB = 128
D_FF = 4 * D_MODEL
PLE_DIM = 256
REL_BUCKETS = 32
REL_EXACT = REL_BUCKETS // 2
REL_MAX_DIST = 4096
LN_EPS = 1e-5
NEG = -1e30
FORCE_SCORE = 1e9
LOG2E = 1.4426950408889634

LANES = 128
KT = 512
SEL_BIG = 2.0 ** 100
SEL_CHUNK = 128
SLC_NQ = 1
CMP_NQ = 2
N_TOK_BIAS = 25
N_CMP_BIAS = 40
VMEM_LIMIT = 56 * 1024 * 1024

_C_BG, _C_CG, _C_HX = 0, 16, 32
_C_KS, _C_KW = 48, 52
_N_NAT = 56 * LANES
_R_Q, _R_VS, _R_VW, _R_G = 0, 16, 20, 24
_N_T = 25 * LANES


def _dot(a, b):
    return jnp.dot(a, b, preferred_element_type=f32)


def _dot_nt(a, b):
    return lax.dot_general(a, b, (((1,), (1,)), ((), ())), preferred_element_type=f32)


def _params(sem, vmem=VMEM_LIMIT):
    return pltpu.CompilerParams(dimension_semantics=sem, vmem_limit_bytes=vmem)


def _mm_kernel(x_ref, w_ref, o_ref):
    o_ref[...] = _dot_nt(x_ref[...], w_ref[...]).astype(o_ref.dtype)


def _proj_nat(xb, wt, tm, tn, name, n=None, col_tile=lambda j: j):
    m, k = xb.shape
    n = wt.shape[0] if n is None else n
    return pl.pallas_call(
        _mm_kernel,
        grid=(m // tm, n // tn),
        in_specs=[pl.BlockSpec((tm, k), lambda i, j: (i, 0)),
                  pl.BlockSpec((tn, k), lambda i, j: (col_tile(j), 0))],
        out_specs=pl.BlockSpec((tm, tn), lambda i, j: (i, j)),
        out_shape=jax.ShapeDtypeStruct((m, n), bf16),
        compiler_params=_params(("parallel", "arbitrary")),
        name=name,
    )(xb, wt)


def _mmt_kernel(w_ref, x_ref, s_ref, o_ref):
    o_ref[0] = (_dot_nt(w_ref[...], x_ref[...]) * s_ref[...]).astype(o_ref.dtype)


def _proj_t(w, xb, row_scale, tm, tn):
    n, k = w.shape
    t = xb.shape[0]
    return pl.pallas_call(
        _mmt_kernel,
        grid=(n // tm, t // tn),
        in_specs=[pl.BlockSpec((tm, k), lambda i, j: (i, 0)),
                  pl.BlockSpec((tn, k), lambda i, j: (j, 0)),
                  pl.BlockSpec((tm, 1), lambda i, j: (i, 0))],
        out_specs=pl.BlockSpec((1, tm, tn), lambda i, j: (j, i, 0)),
        out_shape=jax.ShapeDtypeStruct((t // tn, n, tn), bf16),
        compiler_params=_params(("parallel", "arbitrary")),
        name="proj_t",
    )(w, xb, row_scale)


def _cast_kernel(x_ref, o_ref):
    o_ref[...] = x_ref[...].astype(o_ref.dtype)


def _to_bf16(w, tm, name):
    r, c = w.shape
    return pl.pallas_call(
        _cast_kernel,
        grid=(r // tm,),
        in_specs=[pl.BlockSpec((tm, c), lambda i: (i, 0))],
        out_specs=pl.BlockSpec((tm, c), lambda i: (i, 0)),
        out_shape=jax.ShapeDtypeStruct((r, c), bf16),
        compiler_params=_params(("parallel",)),
        name=name,
    )(w)


def _ya_kernel(bg_ref, cg_ref, hx_ref, cgh_ref, hxh_ref, cw_ref, w_ref, ma_ref, o_ref, a_ref):
    i = pl.program_id(0)
    tm = bg_ref.shape[0]
    cc = 512
    row = lax.broadcasted_iota(i32, (tm, cc), 0)
    for c in range(D_MODEL // cc):
        sl = slice(c * cc, (c + 1) * cc)
        u = cg_ref[:, sl].astype(f32) * hx_ref[:, sl].astype(f32)
        uh = cgh_ref[:, sl].astype(f32) * hxh_ref[:, sl].astype(f32)
        uh = jnp.where(i > 0, uh, 0.0)
        u1 = jnp.where(row == 0, uh[15:16, :], pltpu.roll(u, 1, 0))
        u2 = jnp.where(row == 0, uh[14:15, :], jnp.where(row == 1, uh[15:16, :], pltpu.roll(u, 2, 0)))
        cw = cw_ref[:, sl]
        conv = cw[0:1, :] * u2 + cw[1:2, :] * u1 + cw[2:3, :] * u
        a_ref[:, sl] = (bg_ref[:, sl].astype(f32) * conv).astype(bf16)
    y = _dot(a_ref[...], w_ref[...])
    o_ref[...] = jax.nn.sigmoid(ma_ref[...].astype(f32)) * y


def _mixer_a(projn, mab, conv_w, w_out, tm):
    t = projn.shape[0]
    hb = tm // 16
    wide = lambda c: pl.BlockSpec((tm, D_MODEL), lambda i, c=c: (i, c))
    halo = lambda c: pl.BlockSpec((16, D_MODEL), lambda i, c=c: (jnp.maximum(i * hb - 1, 0), c))
    return pl.pallas_call(
        _ya_kernel,
        grid=(t // tm,),
        in_specs=[wide(0), wide(1), wide(2), halo(1), halo(2),
                  pl.BlockSpec((3, D_MODEL), lambda i: (0, 0)),
                  pl.BlockSpec((D_MODEL, D_MODEL), lambda i: (0, 0)),
                  wide(0)],
        out_specs=pl.BlockSpec((tm, D_MODEL), lambda i: (i, 0)),
        out_shape=jax.ShapeDtypeStruct((t, D_MODEL), f32),
        scratch_shapes=[pltpu.VMEM((tm, D_MODEL), bf16)],
        compiler_params=_params(("parallel",)),
        name="mixer_a",
    )(projn, projn, projn, projn, projn, conv_w, w_out, mab)


def _compress_kernel(x_ref, w_ref, pe_ref, w2_ref, w2t_ref, on_ref, ot_ref, acc_ref, pet_ref):
    l = pl.program_id(1)
    nl = pl.num_programs(1)
    n = x_ref.shape[0]

    @pl.when(l == 0)
    def _():
        acc_ref[...] = jnp.zeros_like(acc_ref)
        pet_ref[...] = jnp.zeros_like(pet_ref)

    wblk = w_ref[0]
    for h in range(N_KV):
        acc_ref[h] += _dot(x_ref[:, h * HD:(h + 1) * HD], wblk)
    pet_ref[...] += _dot(pe_ref[0, 0].astype(bf16), wblk)

    @pl.when(l == nl - 1)
    def _():
        r = pet_ref[...]
        peterm = r[0:1, :CMP_HIDDEN] + r[1:2, CMP_HIDDEN:]
        for h in range(N_KV):
            ab = acc_ref[h]
            hid = ab[:, :CMP_HIDDEN] + pltpu.roll(ab[:, CMP_HIDDEN:], n - 1, 0) + peterm
            gl = jax.nn.gelu(hid).astype(bf16)
            on_ref[0, h] = _dot(gl, w2_ref[0]).astype(bf16)
            ot_ref[0, h] = _dot_nt(w2t_ref[0], gl).astype(bf16)


def _chunked_mm_kernel(x_ref, w_ref, o_ref, acc_ref):
    acc = _dot_nt(x_ref[...], w_ref[...])
    n = w_ref.shape[0]
    rows = o_ref.shape[0]
    for c in range(n // LANES):
        acc_ref[c] = acc[:, c * LANES:(c + 1) * LANES]
    for l in range(CMP_STRIDE):
        for c in range(n // LANES):
            o_ref[:, l * n + c * LANES:l * n + (c + 1) * LANES] = (
                acc_ref[c, pl.ds(l, rows, stride=CMP_STRIDE), :].astype(o_ref.dtype))


def _proj_chunked(xb, w, tm, n, col_tile):
    m, k = xb.shape
    return pl.pallas_call(
        _chunked_mm_kernel,
        grid=(m // tm,),
        in_specs=[pl.BlockSpec((tm, k), lambda i: (i, 0)),
                  pl.BlockSpec((n, k), lambda i: (col_tile, 0))],
        out_specs=pl.BlockSpec((tm // CMP_STRIDE, CMP_STRIDE * n), lambda i: (i, 0)),
        out_shape=jax.ShapeDtypeStruct((m // CMP_STRIDE, CMP_STRIDE * n), bf16),
        scratch_shapes=[pltpu.VMEM((n // LANES, tm, LANES), f32)],
        compiler_params=_params(("parallel",)),
        name="proj_cmp",
    )(xb, w)


def _compress(xr, pe2, w1ab, w2, w2t):
    n = xr.shape[0]
    return pl.pallas_call(
        _compress_kernel,
        grid=(2, CMP_STRIDE),
        in_specs=[pl.BlockSpec((n, N_KV * HD), lambda kv, l: (0, 2 * l + kv)),
                  pl.BlockSpec((1, HD, 2 * CMP_HIDDEN), lambda kv, l: (kv, l, 0)),
                  pl.BlockSpec((1, 1, 8, HD), lambda kv, l: (kv, l, 0, 0)),
                  pl.BlockSpec((1, CMP_HIDDEN, HD), lambda kv, l: (kv, 0, 0)),
                  pl.BlockSpec((1, HD, CMP_HIDDEN), lambda kv, l: (kv, 0, 0))],
        out_specs=[pl.BlockSpec((1, N_KV, n, HD), lambda kv, l: (kv, 0, 0, 0)),
                   pl.BlockSpec((1, N_KV, HD, n), lambda kv, l: (kv, 0, 0, 0))],
        out_shape=[jax.ShapeDtypeStruct((2, N_KV, n, HD), bf16),
                   jax.ShapeDtypeStruct((2, N_KV, HD, n), bf16)],
        scratch_shapes=[pltpu.VMEM((N_KV, n, 2 * CMP_HIDDEN), f32),
                        pltpu.VMEM((8, 2 * CMP_HIDDEN), f32)],
        compiler_params=_params(("arbitrary", "arbitrary")),
        name="compress",
    )(xr, w1ab, pe2, w2, w2t)


def _rel_bucket(dist):
    n = jnp.maximum(dist, 0)
    nf = jnp.maximum(n, 1).astype(f32)
    large = REL_EXACT + (jnp.log(nf / REL_EXACT) / math.log(REL_MAX_DIST / REL_EXACT)
                         * (REL_BUCKETS - REL_EXACT)).astype(i32)
    large = jnp.minimum(large, REL_BUCKETS - 1)
    return jnp.where(n < REL_EXACT, n, large)


def _bias_kernel(tbl_ref, o_ref, *, kstride, koff, n_reg, edge_o, edge_max):
    hk = pl.program_id(0)
    o = pl.program_id(1)
    is_edge = o == n_reg
    ki = lax.broadcasted_iota(i32, (LANES, LANES), 0)
    qi = lax.broadcasted_iota(i32, (LANES, LANES), 1)
    dist = jnp.where(is_edge, edge_o, o) * LANES + qi - kstride * ki - koff
    valid = (dist >= 0) & (dist <= jnp.where(is_edge, edge_max, 2 ** 30))
    bucket = _rel_bucket(dist)
    for g in range(GROUP):
        row = jnp.broadcast_to(tbl_ref[pl.ds(hk * GROUP + g, 1), :], (LANES, LANES))
        acc = jnp.take_along_axis(row, bucket, axis=1)
        o_ref[0, 0, :, g * LANES:(g + 1) * LANES] = jnp.where(valid, acc * LOG2E, NEG)


def _bias_tiles(rel_bias, n_reg, kstride, koff, edge_o, edge_max, name):
    return pl.pallas_call(
        functools.partial(_bias_kernel, kstride=kstride, koff=koff, n_reg=n_reg, edge_o=edge_o,
                          edge_max=edge_max),
        grid=(N_KV, n_reg + 1),
        in_specs=[pl.BlockSpec((N_HEADS, LANES), lambda hk, o: (0, 0))],
        out_specs=pl.BlockSpec((1, 1, LANES, GROUP * LANES), lambda hk, o: (hk, o, 0, 0)),
        out_shape=jax.ShapeDtypeStruct((N_KV, n_reg + 1, LANES, GROUP * LANES), f32),
        compiler_params=_params(("parallel", "arbitrary")),
        name=name,
    )(jnp.pad(rel_bias.astype(f32).T, ((0, 0), (0, LANES - REL_BUCKETS))))


def _query_lanes(q_ref, nq):
    qb = q_ref[0]
    return jnp.concatenate([qb[g * HD:(g + 1) * HD, h * QB:(h + 1) * QB]
                            for h in range(nq) for g in range(GROUP)], axis=1)


def _cmp_kernel(q_ref, kc_ref, vct_ref, bc_ref, ov_ref, g_ref, *rest, n_sel, n_steps, n_branch, nq):
    nw = nq + WINDOW // QB
    kws, vws = rest[:nw], rest[nw:2 * nw]
    bt_ref, o_ref, ns_ref = rest[2 * nw:]
    step_id = pl.program_id(1)
    b0 = step_id * nq
    nsr = ns_ref.shape[1]
    gl = GROUP * LANES
    qt = _query_lanes(q_ref, nq)
    gate = jax.nn.sigmoid(g_ref[0].astype(f32))
    cmp_per_q = QB // CMP_STRIDE
    q_per_tile = LANES // cmp_per_q

    def core(nt, rows):
        keys = nt * LANES
        s = _dot(kc_ref[0, 0, 0:keys, :], qt)
        col_parts = []
        for h in range(nq):
            parts = []
            for c in range(nt):
                o = b0 + h - q_per_tile * c
                idx = jnp.where(o < 0, N_CMP_BIAS, jnp.minimum(o, N_CMP_BIAS - 1))
                parts.append(s[c * LANES:(c + 1) * LANES, h * gl:(h + 1) * gl] + bc_ref[0, idx])
            col_parts.append(jnp.concatenate(parts, axis=0) if nt > 1 else parts[0])
        s = jnp.concatenate(col_parts, axis=1) if nq > 1 else col_parts[0]
        m = jnp.max(s, axis=0, keepdims=True)
        e = jnp.exp2(s - m).astype(bf16)
        lhs = jnp.concatenate([vct_ref[0, 0, :, 0:keys], jnp.ones((16, keys), bf16), ov_ref[0:rows, 0:keys]], axis=0)
        acc = _dot(lhs, e)
        inv = jnp.where(m > 0.5 * NEG, 1.0 / acc[HD:HD + 1], 0.0)
        oct_ = acc[0:HD] * inv
        imp_g = acc[HD + 16:] * inv

        sw = _dot(jnp.concatenate([r[...] for r in kws], axis=0), qt)
        col_parts = []
        for h in range(nq):
            parts = []
            for jw in range(nw):
                dl = h + WINDOW // QB - jw
                if dl < 0 or dl > WINDOW // QB:
                    parts.append(jnp.full((LANES, gl), NEG, f32))
                    continue
                tile = N_TOK_BIAS if dl == WINDOW // QB else dl
                before_start = jnp.where(b0 + h - dl >= 0, 0.0, NEG)
                parts.append(sw[jw * LANES:(jw + 1) * LANES, h * gl:(h + 1) * gl] + bt_ref[0, tile] + before_start)
            col_parts.append(jnp.concatenate(parts, axis=0))
        sw = jnp.concatenate(col_parts, axis=1) if nq > 1 else col_parts[0]
        pw = jnp.exp2(sw - jnp.max(sw, axis=0, keepdims=True)).astype(bf16)
        vw = jnp.concatenate([r[0] for r in vws], axis=1)
        accw = _dot(jnp.concatenate([vw, jnp.ones((8, nw * LANES), bf16)], axis=0), pw)
        owt = accw[0:HD] / accw[HD:HD + 1]
        for h in range(nq):
            qs = slice(h * QB, (h + 1) * QB)
            for g in range(GROUP):
                ls = slice(h * gl + g * LANES, h * gl + (g + 1) * LANES)
                both = oct_[:, ls] * gate[g * 3:g * 3 + 1, qs] + owt[:, ls] * gate[g * 3 + 2:g * 3 + 3, qs]
                o_ref[qs, g * HD:(g + 1) * HD] = both.T.astype(o_ref.dtype)

        imps = []
        for h in range(nq):
            ig = None
            for g in range(GROUP):
                part = imp_g[:, h * gl + g * LANES:h * gl + (g + 1) * LANES]
                ig = part if ig is None else ig + part
            imps.append(ig)
        imp = jnp.concatenate(imps, axis=1) if nq > 1 else imps[0]

        j = lax.broadcasted_iota(i32, (rows, nq * QB), 0)
        tq = b0 * QB + lax.broadcasted_iota(i32, (rows, nq * QB), 1)
        jt = jnp.right_shift(tq, SLC_LEN.bit_length() - 1)
        forced = (j == 0) | (j == jt) | (j == jt - 1)
        future = j * SLC_LEN > tq
        taken = -3e38
        work = jnp.where(forced, taken, jnp.where(future, -1.0, imp))
        jf = j.astype(f32)
        for _ in range(n_sel - 3):
            mx = jnp.max(work, axis=0, keepdims=True)
            jm = jnp.min(jnp.where(work == mx, jf, 1e9), axis=0, keepdims=True)
            work = jnp.where(jf == jm, taken, work)
        ns_ref[0, 0:rows, :] = jnp.where(future | (work != taken), 1.0, 0.0).astype(bf16)
        if nsr > rows:
            ns_ref[0, rows:, :] = jnp.ones((nsr - rows, nq * QB), bf16)

    per_branch = n_steps // n_branch
    for k in range(n_branch):
        last_b = (k + 1) * per_branch * nq - 1
        nt = last_b // q_per_tile + 1
        rows = (last_b + 1) * (QB // SLC_LEN)
        pl.when(step_id // per_branch == k)(functools.partial(core, nt, rows))


def _cmp_attention(projn, projt, cn, ct, bias_c, bias_t, ov_t, t, nsr, nq):
    nb = t // QB
    ncp = t // CMP_STRIDE
    nslc = t // SLC_LEN
    per_kt = KT // QB
    per = per_kt // nq
    nw = nq + WINDOW // QB
    n_sel = min(SLC_TOPN, nslc)
    assert n_sel > 3
    n_steps = nb // nq
    n_branch = 4 if nb % 64 == 0 else 1
    resident = dict(pipeline_mode=pl.Buffered(1))

    def kw_spec(jw):
        return pl.BlockSpec((QB, HD), lambda hk, s, jw=jw: (jnp.maximum(s * nq - WINDOW // QB + jw, 0), _C_KW + hk))

    def vw_spec(jw):
        def imap(hk, s, jw=jw):
            bb = jnp.maximum(s * nq - WINDOW // QB + jw, 0)
            return (bb // per_kt, _R_VW + hk, bb % per_kt)
        return pl.BlockSpec((1, HD, QB), imap)

    return pl.pallas_call(
        functools.partial(_cmp_kernel, n_sel=n_sel, n_steps=n_steps, n_branch=n_branch, nq=nq),
        grid=(N_KV, n_steps),
        in_specs=[pl.BlockSpec((1, GROUP * HD, nq * QB), lambda hk, s: (s // per, hk, s % per)),
                  pl.BlockSpec((1, 1, ncp, HD), lambda hk, s: (0, hk, 0, 0), **resident),
                  pl.BlockSpec((1, 1, HD, ncp), lambda hk, s: (1, hk, 0, 0), **resident),
                  pl.BlockSpec((1, N_CMP_BIAS + 1, LANES, GROUP * LANES), lambda hk, s: (hk, 0, 0, 0), **resident),
                  pl.BlockSpec((nslc, ncp), lambda hk, s: (0, 0), **resident),
                  pl.BlockSpec((1, 16, nq * QB), lambda hk, s: (s // per, _R_G * 8 + hk, s % per))]
                 + [kw_spec(jw) for jw in range(nw)]
                 + [vw_spec(jw) for jw in range(nw)]
                 + [pl.BlockSpec((1, N_TOK_BIAS + 1, LANES, GROUP * LANES), lambda hk, s: (hk, 0, 0, 0), **resident)],
        out_specs=[pl.BlockSpec((nq * QB, GROUP * HD), lambda hk, s: (s, hk)),
                   pl.BlockSpec((1, nsr, nq * QB), lambda hk, s: (hk, 0, s))],
        out_shape=[jax.ShapeDtypeStruct((t, N_HEADS * HD), f32),
                   jax.ShapeDtypeStruct((N_KV, nsr, t), bf16)],
        compiler_params=_params(("parallel", "arbitrary")),
        name="cmp_attention",
    )(projt, cn, ct, bias_c, ov_t, projt, *([projn] * nw), *([projt] * nw), bias_t)


def _slc_kernel(q_ref, ks_ref, vst_ref, ns_ref, bt_ref, ind_ref, g_ref, o1_ref, o_ref,
                qa_ref, s0_ref, s1_ref, p0_ref, p1_ref, acc_ref, m_ref, al_ref, mx_ref, off_ref, *, nq):
    b0 = pl.program_id(1) * nq
    gl = GROUP * LANES
    qt = _query_lanes(q_ref, nq)
    gate = jax.nn.sigmoid(g_ref[0].astype(f32))

    n_chunks = qa_ref.shape[0]
    for jc in range(n_chunks):
        nsj = -ns_ref[0, jc * SEL_CHUNK:(jc + 1) * SEL_CHUNK, :]
        qa_ref[jc, 0:HD, :] = qt
        qa_ref[jc, HD:, :] = jnp.concatenate([nsj[:, h * QB:(h + 1) * QB]
                                              for h in range(nq) for _ in range(GROUP)], axis=1)

    n_tiles = vst_ref.shape[0]
    per_chunk = SEL_CHUNK * SLC_LEN // KT
    sub = KT // LANES

    far_bias = jnp.concatenate([bt_ref[0, N_TOK_BIAS - 1, 0:1, :]] * nq, axis=1)
    ones_rows = jnp.ones((8, KT), bf16)

    def scores(kt, s_ref, far):
        jc = kt // per_chunk
        r0 = pl.multiple_of(kt * KT, KT)
        i0 = pl.multiple_of((kt - jc * per_chunk) * KT, KT)
        kaug = jnp.concatenate([ks_ref[pl.ds(r0, KT), :], ind_ref[pl.ds(i0, KT), :]], axis=1)
        s = _dot(kaug, qa_ref[jc])
        if far:
            s_ref[...] = s
            mx_ref[...] = jnp.max(s, axis=0, keepdims=True) + far_bias
            off_ref[...] = far_bias
            return
        mxs = []
        for h in range(nq):
            cols = slice(h * gl, (h + 1) * gl)
            mx = None
            for jj in range(sub):
                rows = slice(jj * LANES, (jj + 1) * LANES)
                dl = b0 + h - (kt * sub + jj)
                sj = s[rows, cols] + bt_ref[0, jnp.clip(dl, 0, N_TOK_BIAS - 1)]
                s_ref[rows, cols] = sj
                mj = jnp.max(sj, axis=0, keepdims=True)
                mx = mj if mx is None else jnp.maximum(mx, mj)
            mxs.append(mx)
        mx_ref[...] = jnp.concatenate(mxs, axis=1) if nq > 1 else mxs[0]
        off_ref[...] = jnp.zeros_like(off_ref)

    def values(kt):
        return jnp.concatenate([vst_ref[kt], ones_rows], axis=0)

    def step(kt, s_cur, p_cur, s_nxt, p_prev, far):
        m_old = m_ref[...]
        m_new = jnp.maximum(m_old, mx_ref[...])
        shift = m_new - off_ref[...]
        al_old = al_ref[...]
        scores(jnp.minimum(kt + 1, n_tiles - 1), s_nxt, far)
        acc_ref[...] = al_old * acc_ref[...] + _dot(values(jnp.maximum(kt - 1, 0)), p_prev[...])
        p_cur[...] = jnp.exp2(s_cur[...] - shift).astype(bf16)
        m_ref[...] = m_new
        al_ref[...] = jnp.exp2(m_old - m_new)

    acc_ref[...] = jnp.zeros_like(acc_ref)
    m_ref[...] = jnp.full_like(m_ref, NEG)
    al_ref[...] = jnp.ones_like(al_ref)
    p1_ref[...] = jnp.zeros_like(p1_ref)
    scores(0, s0_ref, False)

    n_pairs = (b0 + nq - 1) // (2 * KT // QB) + 1
    n_far = jnp.maximum((b0 - (N_TOK_BIAS - 2)) // sub, 0)
    n_far_pairs = jnp.maximum((n_far - 1) // 2, 0)

    def pair(far):
        def body(i, carry):
            step(2 * i, s0_ref, p0_ref, s1_ref, p1_ref, far)
            step(2 * i + 1, s1_ref, p1_ref, s0_ref, p0_ref, far)
            return carry
        return body

    lax.fori_loop(0, n_far_pairs, pair(True), 0)
    lax.fori_loop(n_far_pairs, n_pairs, pair(False), 0)
    acc = al_ref[...] * acc_ref[...] + _dot(values(2 * n_pairs - 1), p1_ref[...])

    ost = acc[0:HD] / acc[HD:HD + 1]
    for h in range(nq):
        qs = slice(h * QB, (h + 1) * QB)
        for g in range(GROUP):
            ls = slice(h * gl + g * LANES, h * gl + (g + 1) * LANES)
            ds = slice(g * HD, (g + 1) * HD)
            o_ref[qs, ds] = ((ost[:, ls] * gate[g * 3 + 1:g * 3 + 2, qs]).T + o1_ref[qs, ds]).astype(o_ref.dtype)


def _slc_attention(projn, projt, nsel_t, bias_t, ind, o1, t, nq):
    nb = t // QB
    per = KT // QB // nq
    nsr = nsel_t.shape[1]
    lanes = nq * GROUP * LANES
    resident = dict(pipeline_mode=pl.Buffered(1))
    return pl.pallas_call(
        functools.partial(_slc_kernel, nq=nq),
        grid=(N_KV, nb // nq),
        in_specs=[pl.BlockSpec((1, GROUP * HD, nq * QB), lambda hk, bq: (bq // per, hk, bq % per)),
                  pl.BlockSpec((t, HD), lambda hk, bq: (0, _C_KS + hk), **resident),
                  pl.BlockSpec((t // KT, HD, KT), lambda hk, bq: (0, _R_VS + hk, 0), **resident),
                  pl.BlockSpec((1, nsr, nq * QB), lambda hk, bq: (hk, 0, bq)),
                  pl.BlockSpec((1, N_TOK_BIAS + 1, LANES, GROUP * LANES), lambda hk, bq: (hk, 0, 0, 0), **resident),
                  pl.BlockSpec(ind.shape, lambda hk, bq: (0, 0), **resident),
                  pl.BlockSpec((1, 16, nq * QB), lambda hk, bq: (bq // per, _R_G * 8 + hk, bq % per)),
                  pl.BlockSpec((nq * QB, GROUP * HD), lambda hk, bq: (bq, hk))],
        out_specs=pl.BlockSpec((nq * QB, GROUP * HD), lambda hk, bq: (bq, hk)),
        out_shape=jax.ShapeDtypeStruct((t, N_HEADS * HD), bf16),
        scratch_shapes=[pltpu.VMEM((nsr // SEL_CHUNK, 2 * HD, lanes), bf16),
                        pltpu.VMEM((KT, lanes), f32),
                        pltpu.VMEM((KT, lanes), f32),
                        pltpu.VMEM((KT, lanes), bf16),
                        pltpu.VMEM((KT, lanes), bf16),
                        pltpu.VMEM((HD + 8, lanes), f32)]
                       + [pltpu.VMEM((1, lanes), f32)] * 4,
        compiler_params=_params(("parallel", "arbitrary")),
        name="slc_attention",
    )(projt, projn, projt, nsel_t, bias_t, ind, projt, o1)


def _layer_norm(z, g, b):
    mu = jnp.mean(z, axis=-1, keepdims=True)
    var = jnp.mean(jnp.square(z - mu), axis=-1, keepdims=True)
    return (z - mu) * lax.rsqrt(var + LN_EPS) * g + b


def _post_kernel(o2_ref, ya_ref, mb_ref, x_ref, p_ref, wao_ref, wmix_ref, wple_ref, wpg_ref,
                 g_ref, b_ref, x1_ref, r2_ref, *, alpha):
    yb = _dot(o2_ref[...], wao_ref[...])
    mixed = ya_ref[...] + jax.nn.sigmoid(mb_ref[...].astype(f32)) * yb
    z = alpha * x_ref[...] + _dot(mixed.astype(bf16), wmix_ref[...])
    x1 = _layer_norm(z, g_ref[...], b_ref[...])
    x1b = x1.astype(bf16)
    x1_ref[...] = x1b
    ple = _dot(p_ref[...].astype(bf16), wple_ref[...]) * jax.nn.sigmoid(_dot(x1b, wpg_ref[...]))
    r2_ref[...] = alpha * x1 + ple


def _post_attention(o2, ya, mab, x2d, p2d, wao, wmix, wple, wpg, g1, b1, alpha, tm):
    t = x2d.shape[0]
    row = lambda w: pl.BlockSpec((tm, w), lambda i: (i, 0))
    full = lambda a: pl.BlockSpec(a.shape, lambda i: (0, 0), pipeline_mode=pl.Buffered(1))
    return pl.pallas_call(
        functools.partial(_post_kernel, alpha=alpha),
        grid=(t // tm,),
        in_specs=[row(D_MODEL), row(D_MODEL),
                  pl.BlockSpec((tm, D_MODEL), lambda i: (i, 1)),
                  row(D_MODEL), row(PLE_DIM),
                  full(wao), full(wmix), full(wple), full(wpg), full(g1), full(b1)],
        out_specs=[row(D_MODEL), row(D_MODEL)],
        out_shape=[jax.ShapeDtypeStruct((t, D_MODEL), bf16),
                   jax.ShapeDtypeStruct((t, D_MODEL), f32)],
        compiler_params=_params(("parallel",)),
        name="post_attention",
    )(o2, ya, mab, x2d, p2d, wao, wmix, wple, wpg, g1, b1)


def _mlp_kernel(x1_ref, r2_ref, wu_ref, wd_ref, g_ref, b_ref, o_ref):
    k = pl.program_id(1)

    @pl.when(k == 0)
    def _():
        o_ref[...] = r2_ref[...]

    h = jnp.square(jnp.maximum(_dot(x1_ref[...], wu_ref[...]), 0.0))
    o_ref[...] += _dot(h.astype(bf16), wd_ref[...])

    @pl.when(k == pl.num_programs(1) - 1)
    def _():
        o_ref[...] = _layer_norm(o_ref[...], g_ref[...], b_ref[...])


def _mlp(x1b, r2, wu, wd, g2, b2, tm, fc):
    t = x1b.shape[0]
    return pl.pallas_call(
        _mlp_kernel,
        grid=(t // tm, D_FF // fc),
        in_specs=[pl.BlockSpec((tm, D_MODEL), lambda i, k: (i, 0)),
                  pl.BlockSpec((tm, D_MODEL), lambda i, k: (i, 0)),
                  pl.BlockSpec((D_MODEL, fc), lambda i, k: (0, k)),
                  pl.BlockSpec((fc, D_MODEL), lambda i, k: (k, 0)),
                  pl.BlockSpec((1, D_MODEL), lambda i, k: (0, 0)),
                  pl.BlockSpec((1, D_MODEL), lambda i, k: (0, 0))],
        out_specs=pl.BlockSpec((tm, D_MODEL), lambda i, k: (i, 0)),
        out_shape=jax.ShapeDtypeStruct((t, D_MODEL), f32),
        compiler_params=_params(("parallel", "arbitrary")),
        name="mlp",
    )(x1b, r2, wu, wd, g2, b2)


def _layer(x2d, p2d, w_in, conv_w, pe_k, w1_k, w2_k, pe_v, w1_v, w2_v, w_conv_out, w_attn_out,
           w_mix_out, ln1_g, ln1_b, w_mlp_up, w_mlp_down, w_ple, w_ple_gate, ln2_g, ln2_b,
           rel_bias, alpha):
    t = x2d.shape[0]
    assert t % 1024 == 0 and x2d.shape[1] == D_MODEL
    kvw = N_KV * HD
    o_q = 3 * D_MODEL
    o_kc = o_q + N_HEADS * HD
    o_ng = o_kc + 6 * kvw
    o_ma = o_ng + N_HEADS * 3

    n_in = w_in.shape[1]
    cast_rows = max(r for r in range(16, D_MODEL + 1, 16) if n_in % r == 0)
    w_bf = _to_bf16(w_in.T, cast_rows, "cast_w_in")

    def cols(a, n):
        return w_bf[a:a + n]

    tn = kvw
    conv_tiles = 3 * D_MODEL // tn
    ks_tile, kw_tile = (o_kc + 2 * kvw) // tn, (o_kc + 4 * kvw) // tn
    assert _C_KS * LANES == conv_tiles * tn and _C_KW * LANES == (conv_tiles + 1) * tn

    def nat_tile(j):
        return jnp.where(j < conv_tiles, j, jnp.where(j == conv_tiles, ks_tile, kw_tile))

    wng = jnp.pad(cols(o_ng, N_HEADS * 3).reshape(N_KV, GROUP * 3, D_MODEL), ((0, 0), (0, 16 - GROUP * 3), (0, 0)))
    w_t = jnp.concatenate(
        [cols(o_q, N_HEADS * HD), cols(o_kc + 3 * kvw, kvw), cols(o_kc + 5 * kvw, kvw),
         wng.reshape(N_KV * 16, D_MODEL),
         jnp.zeros((_N_T - _R_G * LANES - N_KV * 16, D_MODEL), bf16)], axis=0)
    row_scale = jnp.concatenate([jnp.full((N_HEADS * HD, 1), HD ** -0.5 * LOG2E, f32),
                                 jnp.ones((_N_T - N_HEADS * HD, 1), f32)], axis=0)

    xb = x2d.astype(bf16)
    projn = _proj_nat(xb, w_bf, 2048, tn, "proj_nat", _N_NAT, nat_tile)
    mab = _proj_nat(xb, cols(o_ma, 2 * D_MODEL), 2048, 1024, "proj_gate")
    kvc = _proj_chunked(xb, w_bf, 1024, 2 * kvw, o_kc // (2 * kvw))
    projt = _proj_t(w_t, xb, row_scale, 640, KT)

    ya = _mixer_a(projn, mab, conv_w, w_conv_out.astype(bf16), 512)

    half = CMP_STRIDE * HD
    w1ab = jnp.stack([jnp.concatenate([w[:half], w[half:]], axis=1) for w in (w1_k, w1_v)]).astype(bf16)
    pe2 = jnp.stack([jnp.stack([pe[:CMP_STRIDE], pe[CMP_STRIDE:]], axis=1) for pe in (pe_k, pe_v)])
    pe2 = jnp.pad(pe2, ((0, 0), (0, 0), (0, 6), (0, 0)))
    w2 = jnp.stack([w2_k, w2_v]).astype(bf16)
    cn, ct = _compress(kvc, pe2, w1ab, w2, jnp.swapaxes(w2, 1, 2))

    bias_t = _bias_tiles(rel_bias, N_TOK_BIAS, 1, 0, WINDOW // QB, WINDOW - 1, "bias_tok")
    bias_c = _bias_tiles(rel_bias, N_CMP_BIAS, CMP_STRIDE, CMP_LEN - 1, 0, -1, "bias_cmp")

    ncp = t // CMP_STRIDE
    nslc = t // SLC_LEN
    nsr = -(-nslc // SEL_CHUNK) * SEL_CHUNK
    ci = jnp.arange(ncp)[None, :] * CMP_STRIDE
    sj = jnp.arange(nslc)[:, None] * SLC_LEN
    ov_t = ((ci < sj + SLC_LEN) & (ci + CMP_LEN > sj)).astype(bf16)
    pos = jnp.arange(min(t, SEL_CHUNK * SLC_LEN))[:, None] // SLC_LEN
    ind = jnp.where(pos == jnp.arange(SEL_CHUNK)[None, :], SEL_BIG, 0.0).astype(bf16)

    o1, nsel_t = _cmp_attention(projn, projt, cn, ct, bias_c, bias_t, ov_t, t, nsr, CMP_NQ)
    o2 = _slc_attention(projn, projt, nsel_t, bias_t, ind, o1, t, SLC_NQ)

    x1b, r2 = _post_attention(o2, ya, mab, x2d, p2d, w_attn_out.astype(bf16), w_mix_out.astype(bf16),
                              w_ple.astype(bf16), w_ple_gate.astype(bf16),
                              ln1_g[None, :], ln1_b[None, :], alpha, 256)
    return _mlp(x1b, r2, w_mlp_up.astype(bf16), w_mlp_down.astype(bf16), ln2_g[None, :], ln2_b[None, :], 1024, 512)


def kernel(x, p, w_in, conv_w, cmp_pe_k, cmp_w1_k, cmp_w2_k, cmp_pe_v, cmp_w1_v, cmp_w2_v, w_conv_out, w_attn_out, w_mix_out, ln1_g, ln1_b, w_mlp_up, w_mlp_down, w_ple, w_ple_gate, ln2_g, ln2_b, rel_bias):
    bsz, t, d = x.shape
    assert bsz == 1
    depth = w_in.shape[0]
    alpha = (2 * depth) ** 0.25
    x2d = x[0]
    for i in range(depth):
        x2d = _layer(x2d, p[i, 0], w_in[i], conv_w[i], cmp_pe_k[i], cmp_w1_k[i], cmp_w2_k[i],
                     cmp_pe_v[i], cmp_w1_v[i], cmp_w2_v[i], w_conv_out[i], w_attn_out[i], w_mix_out[i],
                     ln1_g[i], ln1_b[i], w_mlp_up[i], w_mlp_down[i], w_ple[i], w_ple_gate[i],
                     ln2_g[i], ln2_b[i], rel_bias, alpha)
    return x2d[None]
```

```python
import functools
import math

import jax
import jax.numpy as jnp
from jax import lax
from jax.experimental import pallas as pl
from jax.experimental.pallas import tpu as pltpu

f32 = jnp.float32
bf16 = jnp.bfloat16
i32 = jnp.int32

D_MODEL = 2048
N_HEADS = 16
N_KV = 4
GROUP = N_HEADS // N_KV
HD = 128
CMP_LEN = 32
CMP_STRIDE = 16
CMP_HIDDEN = 2 * HD
SLC_LEN = 64
SLC_TOPN = 16
WINDOW = 512
QB = 128
D_FF = 4 * D_MODEL
PLE_DIM = 256
REL_BUCKETS = 32
REL_EXACT = REL_BUCKETS // 2
REL_MAX_DIST = 4096
LN_EPS = 1e-5
NEG = -1e30
FORCE_SCORE = 1e9
LOG2E = 1.4426950408889634

LANES = 128
KT = 512
SEL_BIG = 2.0 ** 100
SEL_CHUNK = 128
SLC_NQ = 1
CMP_NQ = 2
N_TOK_BIAS = 25
N_CMP_BIAS = 40
VMEM_LIMIT = 56 * 1024 * 1024

_C_BG, _C_CG, _C_HX = 0, 16, 32
_C_KS, _C_KW = 48, 52
_N_NAT = 56 * LANES
_R_Q, _R_VS, _R_VW, _R_G = 0, 16, 20, 24
_N_T = 25 * LANES


def _dot(a, b):
    return jnp.dot(a, b, preferred_element_type=f32)


def _dot_nt(a, b):
    return lax.dot_general(a, b, (((1,), (1,)), ((), ())), preferred_element_type=f32)


def _params(sem, vmem=VMEM_LIMIT):
    return pltpu.CompilerParams(dimension_semantics=sem, vmem_limit_bytes=vmem)


def _mm_kernel(x_ref, w_ref, o_ref):
    o_ref[...] = _dot_nt(x_ref[...], w_ref[...]).astype(o_ref.dtype)


def _proj_nat(xb, wt, tm, tn, name, n=None, col_tile=lambda j: j):
    m, k = xb.shape
    n = wt.shape[0] if n is None else n
    return pl.pallas_call(
        _mm_kernel,
        grid=(m // tm, n // tn),
        in_specs=[pl.BlockSpec((tm, k), lambda i, j: (i, 0)),
                  pl.BlockSpec((tn, k), lambda i, j: (col_tile(j), 0))],
        out_specs=pl.BlockSpec((tm, tn), lambda i, j: (i, j)),
        out_shape=jax.ShapeDtypeStruct((m, n), bf16),
        compiler_params=_params(("parallel", "arbitrary")),
        name=name,
    )(xb, wt)


def _mmt_kernel(w_ref, x_ref, s_ref, o_ref):
    o_ref[0] = (_dot_nt(w_ref[...], x_ref[...]) * s_ref[...]).astype(o_ref.dtype)


def _proj_t(w, xb, row_scale, tm, tn):
    n, k = w.shape
    t = xb.shape[0]
    return pl.pallas_call(
        _mmt_kernel,
        grid=(n // tm, t // tn),
        in_specs=[pl.BlockSpec((tm, k), lambda i, j: (i, 0)),
                  pl.BlockSpec((tn, k), lambda i, j: (j, 0)),
                  pl.BlockSpec((tm, 1), lambda i, j: (i, 0))],
        out_specs=pl.BlockSpec((1, tm, tn), lambda i, j: (j, i, 0)),
        out_shape=jax.ShapeDtypeStruct((t // tn, n, tn), bf16),
        compiler_params=_params(("parallel", "arbitrary")),
        name="proj_t",
    )(w, xb, row_scale)


def _cast_kernel(x_ref, o_ref):
    o_ref[...] = x_ref[...].astype(o_ref.dtype)


def _to_bf16(w, tm, name):
    r, c = w.shape
    return pl.pallas_call(
        _cast_kernel,
        grid=(r // tm,),
        in_specs=[pl.BlockSpec((tm, c), lambda i: (i, 0))],
        out_specs=pl.BlockSpec((tm, c), lambda i: (i, 0)),
        out_shape=jax.ShapeDtypeStruct((r, c), bf16),
        compiler_params=_params(("parallel",)),
        name=name,
    )(w)


def _ya_kernel(bg_ref, cg_ref, hx_ref, cgh_ref, hxh_ref, cw_ref, w_ref, ma_ref, o_ref, a_ref):
    i = pl.program_id(0)
    tm = bg_ref.shape[0]
    cc = 512
    row = lax.broadcasted_iota(i32, (tm, cc), 0)
    for c in range(D_MODEL // cc):
        sl = slice(c * cc, (c + 1) * cc)
        u = cg_ref[:, sl].astype(f32) * hx_ref[:, sl].astype(f32)
        uh = cgh_ref[:, sl].astype(f32) * hxh_ref[:, sl].astype(f32)
        uh = jnp.where(i > 0, uh, 0.0)
        u1 = jnp.where(row == 0, uh[15:16, :], pltpu.roll(u, 1, 0))
        u2 = jnp.where(row == 0, uh[14:15, :], jnp.where(row == 1, uh[15:16, :], pltpu.roll(u, 2, 0)))
        cw = cw_ref[:, sl]
        conv = cw[0:1, :] * u2 + cw[1:2, :] * u1 + cw[2:3, :] * u
        a_ref[:, sl] = (bg_ref[:, sl].astype(f32) * conv).astype(bf16)
    y = _dot(a_ref[...], w_ref[...])
    o_ref[...] = jax.nn.sigmoid(ma_ref[...].astype(f32)) * y


def _mixer_a(projn, mab, conv_w, w_out, tm):
    t = projn.shape[0]
    hb = tm // 16
    wide = lambda c: pl.BlockSpec((tm, D_MODEL), lambda i, c=c: (i, c))
    halo = lambda c: pl.BlockSpec((16, D_MODEL), lambda i, c=c: (jnp.maximum(i * hb - 1, 0), c))
    return pl.pallas_call(
        _ya_kernel,
        grid=(t // tm,),
        in_specs=[wide(0), wide(1), wide(2), halo(1), halo(2),
                  pl.BlockSpec((3, D_MODEL), lambda i: (0, 0)),
                  pl.BlockSpec((D_MODEL, D_MODEL), lambda i: (0, 0)),
                  wide(0)],
        out_specs=pl.BlockSpec((tm, D_MODEL), lambda i: (i, 0)),
        out_shape=jax.ShapeDtypeStruct((t, D_MODEL), f32),
        scratch_shapes=[pltpu.VMEM((tm, D_MODEL), bf16)],
        compiler_params=_params(("parallel",)),
        name="mixer_a",
    )(projn, projn, projn, projn, projn, conv_w, w_out, mab)


def _compress_kernel(x_ref, w_ref, pe_ref, w2_ref, w2t_ref, on_ref, ot_ref, acc_ref, pet_ref):
    l = pl.program_id(1)
    nl = pl.num_programs(1)
    n = x_ref.shape[0]

    @pl.when(l == 0)
    def _():
        acc_ref[...] = jnp.zeros_like(acc_ref)
        pet_ref[...] = jnp.zeros_like(pet_ref)

    wblk = w_ref[0]
    for h in range(N_KV):
        acc_ref[h] += _dot(x_ref[:, h * HD:(h + 1) * HD], wblk)
    pet_ref[...] += _dot(pe_ref[0, 0].astype(bf16), wblk)

    @pl.when(l == nl - 1)
    def _():
        r = pet_ref[...]
        peterm = r[0:1, :CMP_HIDDEN] + r[1:2, CMP_HIDDEN:]
        for h in range(N_KV):
            ab = acc_ref[h]
            hid = ab[:, :CMP_HIDDEN] + pltpu.roll(ab[:, CMP_HIDDEN:], n - 1, 0) + peterm
            gl = jax.nn.gelu(hid).astype(bf16)
            on_ref[0, h] = _dot(gl, w2_ref[0]).astype(bf16)
            ot_ref[0, h] = _dot_nt(w2t_ref[0], gl).astype(bf16)


def _chunked_mm_kernel(x_ref, w_ref, o_ref, acc_ref):
    acc = _dot_nt(x_ref[...], w_ref[...])
    n = w_ref.shape[0]
    rows = o_ref.shape[0]
    for c in range(n // LANES):
        acc_ref[c] = acc[:, c * LANES:(c + 1) * LANES]
    for l in range(CMP_STRIDE):
        for c in range(n // LANES):
            o_ref[:, l * n + c * LANES:l * n + (c + 1) * LANES] = (
                acc_ref[c, pl.ds(l, rows, stride=CMP_STRIDE), :].astype(o_ref.dtype))


def _proj_chunked(xb, w, tm, n, col_tile):
    m, k = xb.shape
    return pl.pallas_call(
        _chunked_mm_kernel,
        grid=(m // tm,),
        in_specs=[pl.BlockSpec((tm, k), lambda i: (i, 0)),
                  pl.BlockSpec((n, k), lambda i: (col_tile, 0))],
        out_specs=pl.BlockSpec((tm // CMP_STRIDE, CMP_STRIDE * n), lambda i: (i, 0)),
        out_shape=jax.ShapeDtypeStruct((m // CMP_STRIDE, CMP_STRIDE * n), bf16),
        scratch_shapes=[pltpu.VMEM((n // LANES, tm, LANES), f32)],
        compiler_params=_params(("parallel",)),
        name="proj_cmp",
    )(xb, w)


def _compress(xr, pe2, w1ab, w2, w2t):
    n = xr.shape[0]
    return pl.pallas_call(
        _compress_kernel,
        grid=(2, CMP_STRIDE),
        in_specs=[pl.BlockSpec((n, N_KV * HD), lambda kv, l: (0, 2 * l + kv)),
                  pl.BlockSpec((1, HD, 2 * CMP_HIDDEN), lambda kv, l: (kv, l, 0)),
                  pl.BlockSpec((1, 1, 8, HD), lambda kv, l: (kv, l, 0, 0)),
                  pl.BlockSpec((1, CMP_HIDDEN, HD), lambda kv, l: (kv, 0, 0)),
                  pl.BlockSpec((1, HD, CMP_HIDDEN), lambda kv, l: (kv, 0, 0))],
        out_specs=[pl.BlockSpec((1, N_KV, n, HD), lambda kv, l: (kv, 0, 0, 0)),
                   pl.BlockSpec((1, N_KV, HD, n), lambda kv, l: (kv, 0, 0, 0))],
        out_shape=[jax.ShapeDtypeStruct((2, N_KV, n, HD), bf16),
                   jax.ShapeDtypeStruct((2, N_KV, HD, n), bf16)],
        scratch_shapes=[pltpu.VMEM((N_KV, n, 2 * CMP_HIDDEN), f32),
                        pltpu.VMEM((8, 2 * CMP_HIDDEN), f32)],
        compiler_params=_params(("arbitrary", "arbitrary")),
        name="compress",
    )(xr, w1ab, pe2, w2, w2t)


def _rel_bucket(dist):
    n = jnp.maximum(dist, 0)
    nf = jnp.maximum(n, 1).astype(f32)
    large = REL_EXACT + (jnp.log(nf / REL_EXACT) / math.log(REL_MAX_DIST / REL_EXACT)
                         * (REL_BUCKETS - REL_EXACT)).astype(i32)
    large = jnp.minimum(large, REL_BUCKETS - 1)
    return jnp.where(n < REL_EXACT, n, large)


def _bias_kernel(tbl_ref, o_ref, *, kstride, koff, n_reg, edge_o, edge_max):
    hk = pl.program_id(0)
    ki = lax.broadcasted_iota(i32, (LANES, LANES), 0)
    qi = lax.broadcasted_iota(i32, (LANES, LANES), 1)
    rows = [jnp.broadcast_to(tbl_ref[pl.ds(hk * GROUP + g, 1), :], (LANES, LANES)) for g in range(GROUP)]

    def tile(o, carry):
        is_edge = o == n_reg
        dist = jnp.where(is_edge, edge_o, o) * LANES + qi - kstride * ki - koff
        valid = (dist >= 0) & (dist <= jnp.where(is_edge, edge_max, 2 ** 30))
        bucket = _rel_bucket(dist)
        for g in range(GROUP):
            acc = jnp.take_along_axis(rows[g], bucket, axis=1)
            o_ref[0, o, :, g * LANES:(g + 1) * LANES] = jnp.where(valid, acc * LOG2E, NEG)
        return carry

    lax.fori_loop(0, n_reg + 1, tile, 0)


def _bias_tiles(rel_bias, n_reg, kstride, koff, edge_o, edge_max, name):
    return pl.pallas_call(
        functools.partial(_bias_kernel, kstride=kstride, koff=koff, n_reg=n_reg, edge_o=edge_o,
                          edge_max=edge_max),
        grid=(N_KV,),
        in_specs=[pl.BlockSpec((N_HEADS, LANES), lambda hk: (0, 0))],
        out_specs=pl.BlockSpec((1, n_reg + 1, LANES, GROUP * LANES), lambda hk: (hk, 0, 0, 0)),
        out_shape=jax.ShapeDtypeStruct((N_KV, n_reg + 1, LANES, GROUP * LANES), f32),
        compiler_params=_params(("parallel",)),
        name=name,
    )(jnp.pad(rel_bias.astype(f32).T, ((0, 0), (0, LANES - REL_BUCKETS))))


def _query_lanes(q_ref, nq):
    qb = q_ref[0]
    return jnp.concatenate([qb[g * HD:(g + 1) * HD, h * QB:(h + 1) * QB]
                            for h in range(nq) for g in range(GROUP)], axis=1)


def _cmp_kernel(q_ref, kc_ref, vct_ref, bc_ref, ov_ref, g_ref, *rest, n_sel, n_steps, n_branch, nq):
    nw = nq + WINDOW // QB
    kws, vws = rest[:nw], rest[nw:2 * nw]
    bt_ref, o_ref, ns_ref = rest[2 * nw:]
    step_id = pl.program_id(1)
    b0 = step_id * nq
    nsr = ns_ref.shape[1]
    gl = GROUP * LANES
    qt = _query_lanes(q_ref, nq)
    gate = jax.nn.sigmoid(g_ref[0].astype(f32))
    cmp_per_q = QB // CMP_STRIDE
    q_per_tile = LANES // cmp_per_q

    def core(nt, rows):
        keys = nt * LANES
        s = _dot(kc_ref[0, 0, 0:keys, :], qt)
        col_parts = []
        for h in range(nq):
            parts = []
            for c in range(nt):
                o = b0 + h - q_per_tile * c
                idx = jnp.where(o < 0, N_CMP_BIAS, jnp.minimum(o, N_CMP_BIAS - 1))
                parts.append(s[c * LANES:(c + 1) * LANES, h * gl:(h + 1) * gl] + bc_ref[0, idx])
            col_parts.append(jnp.concatenate(parts, axis=0) if nt > 1 else parts[0])
        s = jnp.concatenate(col_parts, axis=1) if nq > 1 else col_parts[0]
        m = jnp.max(s, axis=0, keepdims=True)
        e = jnp.exp2(s - m).astype(bf16)
        lhs = jnp.concatenate([vct_ref[0, 0, :, 0:keys], jnp.ones((16, keys), bf16), ov_ref[0:rows, 0:keys]], axis=0)
        acc = _dot(lhs, e)
        inv = jnp.where(m > 0.5 * NEG, 1.0 / acc[HD:HD + 1], 0.0)
        oct_ = acc[0:HD] * inv
        imp_g = acc[HD + 16:] * inv

        sw = _dot(jnp.concatenate([r[...] for r in kws], axis=0), qt)
        col_parts = []
        for h in range(nq):
            parts = []
            for jw in range(nw):
                dl = h + WINDOW // QB - jw
                if dl < 0 or dl > WINDOW // QB:
                    parts.append(jnp.full((LANES, gl), NEG, f32))
                    continue
                tile = N_TOK_BIAS if dl == WINDOW // QB else dl
                before_start = jnp.where(b0 + h - dl >= 0, 0.0, NEG)
                parts.append(sw[jw * LANES:(jw + 1) * LANES, h * gl:(h + 1) * gl] + bt_ref[0, tile] + before_start)
            col_parts.append(jnp.concatenate(parts, axis=0))
        sw = jnp.concatenate(col_parts, axis=1) if nq > 1 else col_parts[0]
        pw = jnp.exp2(sw - jnp.max(sw, axis=0, keepdims=True)).astype(bf16)
        vw = jnp.concatenate([r[0] for r in vws], axis=1)
        accw = _dot(jnp.concatenate([vw, jnp.ones((8, nw * LANES), bf16)], axis=0), pw)
        owt = accw[0:HD] / accw[HD:HD + 1]
        for h in range(nq):
            qs = slice(h * QB, (h + 1) * QB)
            for g in range(GROUP):
                ls = slice(h * gl + g * LANES, h * gl + (g + 1) * LANES)
                both = oct_[:, ls] * gate[g * 3:g * 3 + 1, qs] + owt[:, ls] * gate[g * 3 + 2:g * 3 + 3, qs]
                o_ref[qs, g * HD:(g + 1) * HD] = both.T.astype(o_ref.dtype)

        imps = []
        for h in range(nq):
            ig = None
            for g in range(GROUP):
                part = imp_g[:, h * gl + g * LANES:h * gl + (g + 1) * LANES]
                ig = part if ig is None else ig + part
            imps.append(ig)
        imp = jnp.concatenate(imps, axis=1) if nq > 1 else imps[0]

        j = lax.broadcasted_iota(i32, (rows, nq * QB), 0)
        tq = b0 * QB + lax.broadcasted_iota(i32, (rows, nq * QB), 1)
        jt = jnp.right_shift(tq, SLC_LEN.bit_length() - 1)
        forced = (j == 0) | (j == jt) | (j == jt - 1)
        future = j * SLC_LEN > tq
        taken = -3e38
        work = jnp.where(forced, taken, jnp.where(future, -1.0, imp))
        jf = j.astype(f32)
        for _ in range(n_sel - 3):
            mx = jnp.max(work, axis=0, keepdims=True)
            jm = jnp.min(jnp.where(work == mx, jf, 1e9), axis=0, keepdims=True)
            work = jnp.where(jf == jm, taken, work)
        ns_ref[0, 0:rows, :] = jnp.where(future | (work != taken), 1.0, 0.0).astype(bf16)
        if nsr > rows:
            ns_ref[0, rows:, :] = jnp.ones((nsr - rows, nq * QB), bf16)

    per_branch = n_steps // n_branch
    for k in range(n_branch):
        last_b = (k + 1) * per_branch * nq - 1
        nt = last_b // q_per_tile + 1
        rows = (last_b + 1) * (QB // SLC_LEN)
        pl.when(step_id // per_branch == k)(functools.partial(core, nt, rows))


def _cmp_attention(projn, projt, cn, ct, bias_c, bias_t, ov_t, t, nsr, nq):
    nb = t // QB
    ncp = t // CMP_STRIDE
    nslc = t // SLC_LEN
    per_kt = KT // QB
    per = per_kt // nq
    nw = nq + WINDOW // QB
    n_sel = min(SLC_TOPN, nslc)
    assert n_sel > 3
    n_steps = nb // nq
    n_branch = 4 if nb % 64 == 0 else 1
    resident = dict(pipeline_mode=pl.Buffered(1))

    def kw_spec(jw):
        return pl.BlockSpec((QB, HD), lambda hk, s, jw=jw: (jnp.maximum(s * nq - WINDOW // QB + jw, 0), _C_KW + hk))

    def vw_spec(jw):
        def imap(hk, s, jw=jw):
            bb = jnp.maximum(s * nq - WINDOW // QB + jw, 0)
            return (bb // per_kt, _R_VW + hk, bb % per_kt)
        return pl.BlockSpec((1, HD, QB), imap)

    return pl.pallas_call(
        functools.partial(_cmp_kernel, n_sel=n_sel, n_steps=n_steps, n_branch=n_branch, nq=nq),
        grid=(N_KV, n_steps),
        in_specs=[pl.BlockSpec((1, GROUP * HD, nq * QB), lambda hk, s: (s // per, hk, s % per)),
                  pl.BlockSpec((1, 1, ncp, HD), lambda hk, s: (0, hk, 0, 0), **resident),
                  pl.BlockSpec((1, 1, HD, ncp), lambda hk, s: (1, hk, 0, 0), **resident),
                  pl.BlockSpec((1, N_CMP_BIAS + 1, LANES, GROUP * LANES), lambda hk, s: (hk, 0, 0, 0), **resident),
                  pl.BlockSpec((nslc, ncp), lambda hk, s: (0, 0), **resident),
                  pl.BlockSpec((1, 16, nq * QB), lambda hk, s: (s // per, _R_G * 8 + hk, s % per))]
                 + [kw_spec(jw) for jw in range(nw)]
                 + [vw_spec(jw) for jw in range(nw)]
                 + [pl.BlockSpec((1, N_TOK_BIAS + 1, LANES, GROUP * LANES), lambda hk, s: (hk, 0, 0, 0), **resident)],
        out_specs=[pl.BlockSpec((nq * QB, GROUP * HD), lambda hk, s: (s, hk)),
                   pl.BlockSpec((1, nsr, nq * QB), lambda hk, s: (hk, 0, s))],
        out_shape=[jax.ShapeDtypeStruct((t, N_HEADS * HD), f32),
                   jax.ShapeDtypeStruct((N_KV, nsr, t), bf16)],
        compiler_params=_params(("parallel", "arbitrary")),
        name="cmp_attention",
    )(projt, cn, ct, bias_c, ov_t, projt, *([projn] * nw), *([projt] * nw), bias_t)


def _slc_kernel(q_ref, ks_ref, vst_ref, ns_ref, bt_ref, ind_ref, g_ref, o1_ref, o_ref,
                qa_ref, s0_ref, s1_ref, p0_ref, p1_ref, acc_ref, st_ref, *, nq):
    b0 = pl.program_id(1) * nq
    gl = GROUP * LANES
    m_ref, al_ref, mx_ref, off_ref = (st_ref.at[pl.ds(r, 1), :] for r in range(4))
    qt = _query_lanes(q_ref, nq)
    gate = jax.nn.sigmoid(g_ref[0].astype(f32))

    n_chunks = qa_ref.shape[0]
    for jc in range(n_chunks):
        nsj = -ns_ref[0, jc * SEL_CHUNK:(jc + 1) * SEL_CHUNK, :]
        qa_ref[jc, 0:HD, :] = qt
        qa_ref[jc, HD:, :] = jnp.concatenate([nsj[:, h * QB:(h + 1) * QB]
                                              for h in range(nq) for _ in range(GROUP)], axis=1)

    n_tiles = vst_ref.shape[0]
    per_chunk = SEL_CHUNK * SLC_LEN // KT
    sub = KT // LANES

    far_bias = jnp.concatenate([bt_ref[0, N_TOK_BIAS - 1, 0:1, :]] * nq, axis=1)
    ones_rows = jnp.ones((8, KT), bf16)

    def scores(kt, s_ref, far):
        jc = kt // per_chunk
        r0 = pl.multiple_of(kt * KT, KT)
        i0 = pl.multiple_of((kt - jc * per_chunk) * KT, KT)
        kaug = jnp.concatenate([ks_ref[pl.ds(r0, KT), :], ind_ref[pl.ds(i0, KT), :]], axis=1)
        s = _dot(kaug, qa_ref[jc])
        if far:
            s_ref[...] = s
            mx_ref[...] = jnp.max(s, axis=0, keepdims=True) + far_bias
            off_ref[...] = far_bias
            return
        mxs = []
        for h in range(nq):
            cols = slice(h * gl, (h + 1) * gl)
            mx = None
            for jj in range(sub):
                rows = slice(jj * LANES, (jj + 1) * LANES)
                dl = b0 + h - (kt * sub + jj)
                sj = s[rows, cols] + bt_ref[0, jnp.clip(dl, 0, N_TOK_BIAS - 1)]
                s_ref[rows, cols] = sj
                mj = jnp.max(sj, axis=0, keepdims=True)
                mx = mj if mx is None else jnp.maximum(mx, mj)
            mxs.append(mx)
        mx_ref[...] = jnp.concatenate(mxs, axis=1) if nq > 1 else mxs[0]
        off_ref[...] = jnp.zeros_like(off_ref)

    def values(kt):
        return jnp.concatenate([vst_ref[kt], ones_rows], axis=0)

    def step(kt, s_cur, p_cur, s_nxt, p_prev, far):
        m_old = m_ref[...]
        m_new = jnp.maximum(m_old, mx_ref[...])
        shift = m_new - off_ref[...]
        al_old = al_ref[...]
        scores(jnp.minimum(kt + 1, n_tiles - 1), s_nxt, far)
        acc_ref[...] = al_old * acc_ref[...] + _dot(values(jnp.maximum(kt - 1, 0)), p_prev[...])
        p_cur[...] = jnp.exp2(s_cur[...] - shift).astype(bf16)
        m_ref[...] = m_new
        al_ref[...] = jnp.exp2(m_old - m_new)

    acc_ref[...] = jnp.zeros_like(acc_ref)
    m_ref[...] = jnp.full_like(m_ref, NEG)
    al_ref[...] = jnp.ones_like(al_ref)
    p1_ref[...] = jnp.zeros_like(p1_ref)
    scores(0, s0_ref, False)

    n_pairs = (b0 + nq - 1) // (2 * KT // QB) + 1
    n_far = jnp.maximum((b0 - (N_TOK_BIAS - 2)) // sub, 0)
    n_far_pairs = jnp.maximum((n_far - 1) // 2, 0)

    def pair(far):
        def body(i, carry):
            step(2 * i, s0_ref, p0_ref, s1_ref, p1_ref, far)
            step(2 * i + 1, s1_ref, p1_ref, s0_ref, p0_ref, far)
            return carry
        return body

    lax.fori_loop(0, n_far_pairs, pair(True), 0)
    lax.fori_loop(n_far_pairs, n_pairs, pair(False), 0)
    acc = al_ref[...] * acc_ref[...] + _dot(values(2 * n_pairs - 1), p1_ref[...])

    ost = acc[0:HD] / acc[HD:HD + 1]
    for h in range(nq):
        qs = slice(h * QB, (h + 1) * QB)
        for g in range(GROUP):
            ls = slice(h * gl + g * LANES, h * gl + (g + 1) * LANES)
            ds = slice(g * HD, (g + 1) * HD)
            o_ref[qs, ds] = ((ost[:, ls] * gate[g * 3 + 1:g * 3 + 2, qs]).T + o1_ref[qs, ds]).astype(o_ref.dtype)


def _slc_attention(projn, projt, nsel_t, bias_t, ind, o1, t, nq):
    nb = t // QB
    per = KT // QB // nq
    nsr = nsel_t.shape[1]
    lanes = nq * GROUP * LANES
    resident = dict(pipeline_mode=pl.Buffered(1))
    return pl.pallas_call(
        functools.partial(_slc_kernel, nq=nq),
        grid=(N_KV, nb // nq),
        in_specs=[pl.BlockSpec((1, GROUP * HD, nq * QB), lambda hk, bq: (bq // per, hk, bq % per)),
                  pl.BlockSpec((t, HD), lambda hk, bq: (0, _C_KS + hk), **resident),
                  pl.BlockSpec((t // KT, HD, KT), lambda hk, bq: (0, _R_VS + hk, 0), **resident),
                  pl.BlockSpec((1, nsr, nq * QB), lambda hk, bq: (hk, 0, bq)),
                  pl.BlockSpec((1, N_TOK_BIAS + 1, LANES, GROUP * LANES), lambda hk, bq: (hk, 0, 0, 0), **resident),
                  pl.BlockSpec(ind.shape, lambda hk, bq: (0, 0), **resident),
                  pl.BlockSpec((1, 16, nq * QB), lambda hk, bq: (bq // per, _R_G * 8 + hk, bq % per)),
                  pl.BlockSpec((nq * QB, GROUP * HD), lambda hk, bq: (bq, hk))],
        out_specs=pl.BlockSpec((nq * QB, GROUP * HD), lambda hk, bq: (bq, hk)),
        out_shape=jax.ShapeDtypeStruct((t, N_HEADS * HD), bf16),
        scratch_shapes=[pltpu.VMEM((nsr // SEL_CHUNK, 2 * HD, lanes), bf16),
                        pltpu.VMEM((KT, lanes), f32),
                        pltpu.VMEM((KT, lanes), f32),
                        pltpu.VMEM((KT, lanes), bf16),
                        pltpu.VMEM((KT, lanes), bf16),
                        pltpu.VMEM((HD + 8, lanes), f32),
                        pltpu.VMEM((8, lanes), f32)],
        compiler_params=_params(("parallel", "arbitrary")),
        name="slc_attention",
    )(projt, projn, projt, nsel_t, bias_t, ind, projt, o1)


def _layer_norm(z, g, b):
    mu = jnp.mean(z, axis=-1, keepdims=True)
    var = jnp.mean(jnp.square(z - mu), axis=-1, keepdims=True)
    return (z - mu) * lax.rsqrt(var + LN_EPS) * g + b


def _post_kernel(o2_ref, ya_ref, mb_ref, x_ref, p_ref, wao_ref, wmix_ref, wple_ref, wpg_ref,
                 g_ref, b_ref, x1_ref, r2_ref, *, alpha):
    yb = _dot(o2_ref[...], wao_ref[...])
    mixed = ya_ref[...] + jax.nn.sigmoid(mb_ref[...].astype(f32)) * yb
    z = alpha * x_ref[...] + _dot(mixed.astype(bf16), wmix_ref[...])
    x1 = _layer_norm(z, g_ref[...], b_ref[...])
    x1b = x1.astype(bf16)
    x1_ref[...] = x1b
    ple = _dot(p_ref[...].astype(bf16), wple_ref[...]) * jax.nn.sigmoid(_dot(x1b, wpg_ref[...]))
    r2_ref[...] = alpha * x1 + ple


def _post_attention(o2, ya, mab, x2d, p2d, wao, wmix, wple, wpg, g1, b1, alpha, tm):
    t = x2d.shape[0]
    row = lambda w: pl.BlockSpec((tm, w), lambda i: (i, 0))
    full = lambda a: pl.BlockSpec(a.shape, lambda i: (0, 0), pipeline_mode=pl.Buffered(1))
    return pl.pallas_call(
        functools.partial(_post_kernel, alpha=alpha),
        grid=(t // tm,),
        in_specs=[row(D_MODEL), row(D_MODEL),
                  pl.BlockSpec((tm, D_MODEL), lambda i: (i, 1)),
                  row(D_MODEL), row(PLE_DIM),
                  full(wao), full(wmix), full(wple), full(wpg), full(g1), full(b1)],
        out_specs=[row(D_MODEL), row(D_MODEL)],
        out_shape=[jax.ShapeDtypeStruct((t, D_MODEL), bf16),
                   jax.ShapeDtypeStruct((t, D_MODEL), f32)],
        compiler_params=_params(("parallel",)),
        name="post_attention",
    )(o2, ya, mab, x2d, p2d, wao, wmix, wple, wpg, g1, b1)


def _mlp_kernel(x1_ref, r2_ref, wu_ref, wd_ref, g_ref, b_ref, o_ref):
    k = pl.program_id(1)

    @pl.when(k == 0)
    def _():
        o_ref[...] = r2_ref[...]

    h = jnp.square(jnp.maximum(_dot(x1_ref[...], wu_ref[...]), 0.0))
    o_ref[...] += _dot(h.astype(bf16), wd_ref[...])

    @pl.when(k == pl.num_programs(1) - 1)
    def _():
        o_ref[...] = _layer_norm(o_ref[...], g_ref[...], b_ref[...])


def _mlp(x1b, r2, wu, wd, g2, b2, tm, fc):
    t = x1b.shape[0]
    return pl.pallas_call(
        _mlp_kernel,
        grid=(t // tm, D_FF // fc),
        in_specs=[pl.BlockSpec((tm, D_MODEL), lambda i, k: (i, 0)),
                  pl.BlockSpec((tm, D_MODEL), lambda i, k: (i, 0)),
                  pl.BlockSpec((D_MODEL, fc), lambda i, k: (0, k)),
                  pl.BlockSpec((fc, D_MODEL), lambda i, k: (k, 0)),
                  pl.BlockSpec((1, D_MODEL), lambda i, k: (0, 0)),
                  pl.BlockSpec((1, D_MODEL), lambda i, k: (0, 0))],
        out_specs=pl.BlockSpec((tm, D_MODEL), lambda i, k: (i, 0)),
        out_shape=jax.ShapeDtypeStruct((t, D_MODEL), f32),
        compiler_params=_params(("parallel", "arbitrary")),
        name="mlp",
    )(x1b, r2, wu, wd, g2, b2)


def _layer(x2d, p2d, w_in, conv_w, pe_k, w1_k, w2_k, pe_v, w1_v, w2_v, w_conv_out, w_attn_out,
           w_mix_out, ln1_g, ln1_b, w_mlp_up, w_mlp_down, w_ple, w_ple_gate, ln2_g, ln2_b,
           rel_bias, alpha):
    t = x2d.shape[0]
    assert t % 1024 == 0 and x2d.shape[1] == D_MODEL
    kvw = N_KV * HD
    o_q = 3 * D_MODEL
    o_kc = o_q + N_HEADS * HD
    o_ng = o_kc + 6 * kvw
    o_ma = o_ng + N_HEADS * 3

    n_in = w_in.shape[1]
    cast_rows = max(r for r in range(16, D_MODEL + 1, 16) if n_in % r == 0)
    w_bf = _to_bf16(w_in.T, cast_rows, "cast_w_in")

    def cols(a, n):
        return w_bf[a:a + n]

    tn = kvw
    conv_tiles = 3 * D_MODEL // tn
    ks_tile, kw_tile = (o_kc + 2 * kvw) // tn, (o_kc + 4 * kvw) // tn
    assert _C_KS * LANES == conv_tiles * tn and _C_KW * LANES == (conv_tiles + 1) * tn

    def nat_tile(j):
        return jnp.where(j < conv_tiles, j, jnp.where(j == conv_tiles, ks_tile, kw_tile))

    wng = jnp.pad(cols(o_ng, N_HEADS * 3).reshape(N_KV, GROUP * 3, D_MODEL), ((0, 0), (0, 16 - GROUP * 3), (0, 0)))
    w_t = jnp.concatenate(
        [cols(o_q, N_HEADS * HD), cols(o_kc + 3 * kvw, kvw), cols(o_kc + 5 * kvw, kvw),
         wng.reshape(N_KV * 16, D_MODEL),
         jnp.zeros((_N_T - _R_G * LANES - N_KV * 16, D_MODEL), bf16)], axis=0)
    row_scale = jnp.concatenate([jnp.full((N_HEADS * HD, 1), HD ** -0.5 * LOG2E, f32),
                                 jnp.ones((_N_T - N_HEADS * HD, 1), f32)], axis=0)

    xb = x2d.astype(bf16)
    projn = _proj_nat(xb, w_bf, 2048, tn, "proj_nat", _N_NAT, nat_tile)
    mab = _proj_nat(xb, cols(o_ma, 2 * D_MODEL), 2048, 1024, "proj_gate")
    kvc = _proj_chunked(xb, w_bf, 1024, 2 * kvw, o_kc // (2 * kvw))
    projt = _proj_t(w_t, xb, row_scale, _N_T, KT)

    ya = _mixer_a(projn, mab, conv_w, w_conv_out.astype(bf16), 512)

    half = CMP_STRIDE * HD
    w1ab = jnp.stack([jnp.concatenate([w[:half], w[half:]], axis=1) for w in (w1_k, w1_v)]).astype(bf16)
    pe2 = jnp.stack([jnp.stack([pe[:CMP_STRIDE], pe[CMP_STRIDE:]], axis=1) for pe in (pe_k, pe_v)])
    pe2 = jnp.pad(pe2, ((0, 0), (0, 0), (0, 6), (0, 0)))
    w2 = jnp.stack([w2_k, w2_v]).astype(bf16)
    cn, ct = _compress(kvc, pe2, w1ab, w2, jnp.swapaxes(w2, 1, 2))

    bias_t = _bias_tiles(rel_bias, N_TOK_BIAS, 1, 0, WINDOW // QB, WINDOW - 1, "bias_tok")
    bias_c = _bias_tiles(rel_bias, N_CMP_BIAS, CMP_STRIDE, CMP_LEN - 1, 0, -1, "bias_cmp")

    ncp = t // CMP_STRIDE
    nslc = t // SLC_LEN
    nsr = -(-nslc // SEL_CHUNK) * SEL_CHUNK
    ci = jnp.arange(ncp)[None, :] * CMP_STRIDE
    sj = jnp.arange(nslc)[:, None] * SLC_LEN
    ov_t = ((ci < sj + SLC_LEN) & (ci + CMP_LEN > sj)).astype(bf16)
    pos = jnp.arange(min(t, SEL_CHUNK * SLC_LEN))[:, None] // SLC_LEN
    ind = jnp.where(pos == jnp.arange(SEL_CHUNK)[None, :], SEL_BIG, 0.0).astype(bf16)

    o1, nsel_t = _cmp_attention(projn, projt, cn, ct, bias_c, bias_t, ov_t, t, nsr, CMP_NQ)
    o2 = _slc_attention(projn, projt, nsel_t, bias_t, ind, o1, t, SLC_NQ)

    x1b, r2 = _post_attention(o2, ya, mab, x2d, p2d, w_attn_out.astype(bf16), w_mix_out.astype(bf16),
                              w_ple.astype(bf16), w_ple_gate.astype(bf16),
                              ln1_g[None, :], ln1_b[None, :], alpha, 256)
    return _mlp(x1b, r2, w_mlp_up.astype(bf16), w_mlp_down.astype(bf16), ln2_g[None, :], ln2_b[None, :], 1024, 512)


def kernel(x, p, w_in, conv_w, cmp_pe_k, cmp_w1_k, cmp_w2_k, cmp_pe_v, cmp_w1_v, cmp_w2_v, w_conv_out, w_attn_out, w_mix_out, ln1_g, ln1_b, w_mlp_up, w_mlp_down, w_ple, w_ple_gate, ln2_g, ln2_b, rel_bias):
    bsz, t, d = x.shape
    assert bsz == 1
    depth = w_in.shape[0]
    alpha = (2 * depth) ** 0.25
    x2d = x[0]
    for i in range(depth):
        x2d = _layer(x2d, p[i, 0], w_in[i], conv_w[i], cmp_pe_k[i], cmp_w1_k[i], cmp_w2_k[i],
                     cmp_pe_v[i], cmp_w1_v[i], cmp_w2_v[i], w_conv_out[i], w_attn_out[i], w_mix_out[i],
                     ln1_g[i], ln1_b[i], w_mlp_up[i], w_mlp_down[i], w_ple[i], w_ple_gate[i],
                     ln2_g[i], ln2_b[i], rel_bias, alpha)
    return x2d[None]
```

```python
import functools
import math

import jax
import jax.numpy as jnp
from jax import lax
from jax.experimental import pallas as pl
from jax.experimental.pallas import tpu as pltpu

f32 = jnp.float32
bf16 = jnp.bfloat16
i32 = jnp.int32

D_MODEL = 2048
N_HEADS = 16
N_KV = 4
GROUP = N_HEADS // N_KV
HD = 128
CMP_LEN = 32
CMP_STRIDE = 16
CMP_HIDDEN = 2 * HD
SLC_LEN = 64
SLC_TOPN = 16
WINDOW = 512
QB = 128
D_FF = 4 * D_MODEL
PLE_DIM = 256
REL_BUCKETS = 32
REL_EXACT = REL_BUCKETS // 2
REL_MAX_DIST = 4096
LN_EPS = 1e-5
NEG = -1e30
FORCE_SCORE = 1e9
LOG2E = 1.4426950408889634

LANES = 128
KT = 512
SEL_BIG = 2.0 ** 100
SEL_CHUNK = 128
SLC_NQ = 1
CMP_NQ = 2
N_TOK_BIAS = 25
N_CMP_BIAS = 40
VMEM_LIMIT = 56 * 1024 * 1024

_C_BG, _C_CG, _C_HX = 0, 16, 32
_C_KS, _C_KW = 48, 52
_N_NAT = 56 * LANES
_R_Q, _R_VS, _R_VW, _R_G = 0, 16, 20, 24
_N_T = 25 * LANES


def _dot(a, b):
    return jnp.dot(a, b, preferred_element_type=f32)


def _dot_nt(a, b):
    return lax.dot_general(a, b, (((1,), (1,)), ((), ())), preferred_element_type=f32)


def _params(sem, vmem=VMEM_LIMIT):
    return pltpu.CompilerParams(dimension_semantics=sem, vmem_limit_bytes=vmem)


def _mm_kernel(x_ref, w_ref, o_ref):
    o_ref[...] = _dot_nt(x_ref[...], w_ref[...]).astype(o_ref.dtype)


def _proj_nat(xb, wt, tm, tn, name, n=None, col_tile=lambda j: j):
    m, k = xb.shape
    n = wt.shape[0] if n is None else n
    return pl.pallas_call(
        _mm_kernel,
        grid=(m // tm, n // tn),
        in_specs=[pl.BlockSpec((tm, k), lambda i, j: (i, 0)),
                  pl.BlockSpec((tn, k), lambda i, j: (col_tile(j), 0))],
        out_specs=pl.BlockSpec((tm, tn), lambda i, j: (i, j)),
        out_shape=jax.ShapeDtypeStruct((m, n), bf16),
        compiler_params=_params(("parallel", "arbitrary")),
        name=name,
    )(xb, wt)


def _mmt_kernel(w_ref, x_ref, s_ref, o_ref):
    o_ref[0] = (_dot_nt(w_ref[...], x_ref[...]) * s_ref[...]).astype(o_ref.dtype)


def _proj_t(w, xb, row_scale, tm, tn):
    n, k = w.shape
    t = xb.shape[0]
    return pl.pallas_call(
        _mmt_kernel,
        grid=(n // tm, t // tn),
        in_specs=[pl.BlockSpec((tm, k), lambda i, j: (i, 0)),
                  pl.BlockSpec((tn, k), lambda i, j: (j, 0)),
                  pl.BlockSpec((tm, 1), lambda i, j: (i, 0))],
        out_specs=pl.BlockSpec((1, tm, tn), lambda i, j: (j, i, 0)),
        out_shape=jax.ShapeDtypeStruct((t // tn, n, tn), bf16),
        compiler_params=_params(("parallel", "arbitrary")),
        name="proj_t",
    )(w, xb, row_scale)


def _cast_kernel(x_ref, o_ref):
    o_ref[...] = x_ref[...].astype(o_ref.dtype)


def _to_bf16(w, tm, name):
    r, c = w.shape
    return pl.pallas_call(
        _cast_kernel,
        grid=(r // tm,),
        in_specs=[pl.BlockSpec((tm, c), lambda i: (i, 0))],
        out_specs=pl.BlockSpec((tm, c), lambda i: (i, 0)),
        out_shape=jax.ShapeDtypeStruct((r, c), bf16),
        compiler_params=_params(("parallel",)),
        name=name,
    )(w)


def _ya_kernel(bg_ref, cg_ref, hx_ref, cgh_ref, hxh_ref, cw_ref, w_ref, ma_ref, o_ref, a_ref):
    i = pl.program_id(0)
    tm = bg_ref.shape[0]
    cc = 512
    row = lax.broadcasted_iota(i32, (tm, cc), 0)
    for c in range(D_MODEL // cc):
        sl = slice(c * cc, (c + 1) * cc)
        u = cg_ref[:, sl].astype(f32) * hx_ref[:, sl].astype(f32)
        uh = cgh_ref[:, sl].astype(f32) * hxh_ref[:, sl].astype(f32)
        uh = jnp.where(i > 0, uh, 0.0)
        u1 = jnp.where(row == 0, uh[15:16, :], pltpu.roll(u, 1, 0))
        u2 = jnp.where(row == 0, uh[14:15, :], jnp.where(row == 1, uh[15:16, :], pltpu.roll(u, 2, 0)))
        cw = cw_ref[:, sl]
        conv = cw[0:1, :] * u2 + cw[1:2, :] * u1 + cw[2:3, :] * u
        a_ref[:, sl] = (bg_ref[:, sl].astype(f32) * conv).astype(bf16)
    y = _dot(a_ref[...], w_ref[...])
    o_ref[...] = (jax.nn.sigmoid(ma_ref[...].astype(f32)) * y).astype(o_ref.dtype)


def _mixer_a(projn, mab, conv_w, w_out, tm):
    t = projn.shape[0]
    hb = tm // 16
    wide = lambda c: pl.BlockSpec((tm, D_MODEL), lambda i, c=c: (i, c))
    halo = lambda c: pl.BlockSpec((16, D_MODEL), lambda i, c=c: (jnp.maximum(i * hb - 1, 0), c))
    return pl.pallas_call(
        _ya_kernel,
        grid=(t // tm,),
        in_specs=[wide(0), wide(1), wide(2), halo(1), halo(2),
                  pl.BlockSpec((3, D_MODEL), lambda i: (0, 0)),
                  pl.BlockSpec((D_MODEL, D_MODEL), lambda i: (0, 0)),
                  wide(0)],
        out_specs=pl.BlockSpec((tm, D_MODEL), lambda i: (i, 0)),
        out_shape=jax.ShapeDtypeStruct((t, D_MODEL), bf16),
        scratch_shapes=[pltpu.VMEM((tm, D_MODEL), bf16)],
        compiler_params=_params(("parallel",)),
        name="mixer_a",
    )(projn, projn, projn, projn, projn, conv_w, w_out, mab)


def _compress_kernel(x_ref, w_ref, pe_ref, w2_ref, w2t_ref, on_ref, ot_ref, acc_ref, pet_ref):
    l = pl.program_id(1)
    nl = pl.num_programs(1)
    n = x_ref.shape[0]

    @pl.when(l == 0)
    def _():
        acc_ref[...] = jnp.zeros_like(acc_ref)
        pet_ref[...] = jnp.zeros_like(pet_ref)

    wblk = w_ref[0]
    for h in range(N_KV):
        acc_ref[h] += _dot(x_ref[:, h * HD:(h + 1) * HD], wblk)
    pet_ref[...] += _dot(pe_ref[0, 0].astype(bf16), wblk)

    @pl.when(l == nl - 1)
    def _():
        r = pet_ref[...]
        peterm = r[0:1, :CMP_HIDDEN] + r[1:2, CMP_HIDDEN:]
        for h in range(N_KV):
            ab = acc_ref[h]
            hid = ab[:, :CMP_HIDDEN] + pltpu.roll(ab[:, CMP_HIDDEN:], n - 1, 0) + peterm
            gl = jax.nn.gelu(hid).astype(bf16)
            on_ref[0, h] = _dot(gl, w2_ref[0]).astype(bf16)
            ot_ref[0, h] = _dot_nt(w2t_ref[0], gl).astype(bf16)


def _chunked_mm_kernel(x_ref, w_ref, o_ref, acc_ref):
    acc = _dot_nt(x_ref[...], w_ref[...])
    n = w_ref.shape[0]
    rows = o_ref.shape[0]
    for c in range(n // LANES):
        acc_ref[c] = acc[:, c * LANES:(c + 1) * LANES]
    for l in range(CMP_STRIDE):
        for c in range(n // LANES):
            o_ref[:, l * n + c * LANES:l * n + (c + 1) * LANES] = (
                acc_ref[c, pl.ds(l, rows, stride=CMP_STRIDE), :].astype(o_ref.dtype))


def _proj_chunked(xb, w, tm, n, col_tile):
    m, k = xb.shape
    return pl.pallas_call(
        _chunked_mm_kernel,
        grid=(m // tm,),
        in_specs=[pl.BlockSpec((tm, k), lambda i: (i, 0)),
                  pl.BlockSpec((n, k), lambda i: (col_tile, 0))],
        out_specs=pl.BlockSpec((tm // CMP_STRIDE, CMP_STRIDE * n), lambda i: (i, 0)),
        out_shape=jax.ShapeDtypeStruct((m // CMP_STRIDE, CMP_STRIDE * n), bf16),
        scratch_shapes=[pltpu.VMEM((n // LANES, tm, LANES), f32)],
        compiler_params=_params(("parallel",)),
        name="proj_cmp",
    )(xb, w)


def _compress(xr, pe2, w1ab, w2, w2t):
    n = xr.shape[0]
    return pl.pallas_call(
        _compress_kernel,
        grid=(2, CMP_STRIDE),
        in_specs=[pl.BlockSpec((n, N_KV * HD), lambda kv, l: (0, 2 * l + kv)),
                  pl.BlockSpec((1, HD, 2 * CMP_HIDDEN), lambda kv, l: (kv, l, 0)),
                  pl.BlockSpec((1, 1, 8, HD), lambda kv, l: (kv, l, 0, 0)),
                  pl.BlockSpec((1, CMP_HIDDEN, HD), lambda kv, l: (kv, 0, 0)),
                  pl.BlockSpec((1, HD, CMP_HIDDEN), lambda kv, l: (kv, 0, 0))],
        out_specs=[pl.BlockSpec((1, N_KV, n, HD), lambda kv, l: (kv, 0, 0, 0)),
                   pl.BlockSpec((1, N_KV, HD, n), lambda kv, l: (kv, 0, 0, 0))],
        out_shape=[jax.ShapeDtypeStruct((2, N_KV, n, HD), bf16),
                   jax.ShapeDtypeStruct((2, N_KV, HD, n), bf16)],
        scratch_shapes=[pltpu.VMEM((N_KV, n, 2 * CMP_HIDDEN), f32),
                        pltpu.VMEM((8, 2 * CMP_HIDDEN), f32)],
        compiler_params=_params(("arbitrary", "arbitrary")),
        name="compress",
    )(xr, w1ab, pe2, w2, w2t)


def _rel_bucket(dist):
    n = jnp.maximum(dist, 0)
    nf = jnp.maximum(n, 1).astype(f32)
    large = REL_EXACT + (jnp.log(nf / REL_EXACT) / math.log(REL_MAX_DIST / REL_EXACT)
                         * (REL_BUCKETS - REL_EXACT)).astype(i32)
    large = jnp.minimum(large, REL_BUCKETS - 1)
    return jnp.where(n < REL_EXACT, n, large)


def _bias_kernel(tbl_ref, o_ref, *, kstride, koff, n_reg, edge_o, edge_max):
    hk = pl.program_id(0)
    ki = lax.broadcasted_iota(i32, (LANES, LANES), 0)
    qi = lax.broadcasted_iota(i32, (LANES, LANES), 1)
    rows = [jnp.broadcast_to(tbl_ref[pl.ds(hk * GROUP + g, 1), :], (LANES, LANES)) for g in range(GROUP)]

    def tile(o, carry):
        is_edge = o == n_reg
        dist = jnp.where(is_edge, edge_o, o) * LANES + qi - kstride * ki - koff
        valid = (dist >= 0) & (dist <= jnp.where(is_edge, edge_max, 2 ** 30))
        bucket = _rel_bucket(dist)
        for g in range(GROUP):
            acc = jnp.take_along_axis(rows[g], bucket, axis=1)
            o_ref[0, o, :, g * LANES:(g + 1) * LANES] = jnp.where(valid, acc * LOG2E, NEG)
        return carry

    lax.fori_loop(0, n_reg + 1, tile, 0)


def _bias_tiles(rel_bias, n_reg, kstride, koff, edge_o, edge_max, name):
    return pl.pallas_call(
        functools.partial(_bias_kernel, kstride=kstride, koff=koff, n_reg=n_reg, edge_o=edge_o,
                          edge_max=edge_max),
        grid=(N_KV,),
        in_specs=[pl.BlockSpec((N_HEADS, LANES), lambda hk: (0, 0))],
        out_specs=pl.BlockSpec((1, n_reg + 1, LANES, GROUP * LANES), lambda hk: (hk, 0, 0, 0)),
        out_shape=jax.ShapeDtypeStruct((N_KV, n_reg + 1, LANES, GROUP * LANES), f32),
        compiler_params=_params(("parallel",)),
        name=name,
    )(jnp.pad(rel_bias.astype(f32).T, ((0, 0), (0, LANES - REL_BUCKETS))))


def _query_lanes(q_ref, nq):
    qb = q_ref[0]
    return jnp.concatenate([qb[g * HD:(g + 1) * HD, h * QB:(h + 1) * QB]
                            for h in range(nq) for g in range(GROUP)], axis=1)


def _cmp_kernel(q_ref, kc_ref, vct_ref, bc_ref, ov_ref, g_ref, *rest, n_sel, n_steps, n_branch, nq):
    nw = nq + WINDOW // QB
    kws, vws = rest[:nw], rest[nw:2 * nw]
    bt_ref, o_ref, ns_ref = rest[2 * nw:]
    step_id = pl.program_id(1)
    b0 = step_id * nq
    nsr = ns_ref.shape[1]
    gl = GROUP * LANES
    qt = _query_lanes(q_ref, nq)
    gate = jax.nn.sigmoid(g_ref[0].astype(f32))
    cmp_per_q = QB // CMP_STRIDE
    q_per_tile = LANES // cmp_per_q

    def core(nt, rows):
        keys = nt * LANES
        s = _dot(kc_ref[0, 0, 0:keys, :], qt)
        col_parts = []
        for h in range(nq):
            parts = []
            for c in range(nt):
                o = b0 + h - q_per_tile * c
                idx = jnp.where(o < 0, N_CMP_BIAS, jnp.minimum(o, N_CMP_BIAS - 1))
                parts.append(s[c * LANES:(c + 1) * LANES, h * gl:(h + 1) * gl] + bc_ref[0, idx])
            col_parts.append(jnp.concatenate(parts, axis=0) if nt > 1 else parts[0])
        s = jnp.concatenate(col_parts, axis=1) if nq > 1 else col_parts[0]
        m = jnp.max(s, axis=0, keepdims=True)
        e = jnp.exp2(s - m).astype(bf16)
        lhs = jnp.concatenate([vct_ref[0, 0, :, 0:keys], jnp.ones((16, keys), bf16), ov_ref[0:rows, 0:keys]], axis=0)
        acc = _dot(lhs, e)
        inv = jnp.where(m > 0.5 * NEG, 1.0 / acc[HD:HD + 1], 0.0)
        oct_ = acc[0:HD] * inv
        imp_g = acc[HD + 16:] * inv

        sw = _dot(jnp.concatenate([r[...] for r in kws], axis=0), qt)
        col_parts = []
        for h in range(nq):
            parts = []
            for jw in range(nw):
                dl = h + WINDOW // QB - jw
                if dl < 0 or dl > WINDOW // QB:
                    parts.append(jnp.full((LANES, gl), NEG, f32))
                    continue
                tile = N_TOK_BIAS if dl == WINDOW // QB else dl
                before_start = jnp.where(b0 + h - dl >= 0, 0.0, NEG)
                parts.append(sw[jw * LANES:(jw + 1) * LANES, h * gl:(h + 1) * gl] + bt_ref[0, tile] + before_start)
            col_parts.append(jnp.concatenate(parts, axis=0))
        sw = jnp.concatenate(col_parts, axis=1) if nq > 1 else col_parts[0]
        pw = jnp.exp2(sw - jnp.max(sw, axis=0, keepdims=True)).astype(bf16)
        vw = jnp.concatenate([r[0] for r in vws], axis=1)
        accw = _dot(jnp.concatenate([vw, jnp.ones((8, nw * LANES), bf16)], axis=0), pw)
        owt = accw[0:HD] / accw[HD:HD + 1]
        for h in range(nq):
            qs = slice(h * QB, (h + 1) * QB)
            for g in range(GROUP):
                ls = slice(h * gl + g * LANES, h * gl + (g + 1) * LANES)
                both = oct_[:, ls] * gate[g * 3:g * 3 + 1, qs] + owt[:, ls] * gate[g * 3 + 2:g * 3 + 3, qs]
                o_ref[qs, g * HD:(g + 1) * HD] = both.T.astype(o_ref.dtype)

        imps = []
        for h in range(nq):
            ig = None
            for g in range(GROUP):
                part = imp_g[:, h * gl + g * LANES:h * gl + (g + 1) * LANES]
                ig = part if ig is None else ig + part
            imps.append(ig)
        imp = jnp.concatenate(imps, axis=1) if nq > 1 else imps[0]

        j = lax.broadcasted_iota(i32, (rows, nq * QB), 0)
        tq = b0 * QB + lax.broadcasted_iota(i32, (rows, nq * QB), 1)
        jt = jnp.right_shift(tq, SLC_LEN.bit_length() - 1)
        forced = (j == 0) | (j == jt) | (j == jt - 1)
        future = j * SLC_LEN > tq
        taken = -3e38
        work = jnp.where(forced, taken, jnp.where(future, -1.0, imp))
        jf = j.astype(f32)
        for _ in range(n_sel - 3):
            mx = jnp.max(work, axis=0, keepdims=True)
            jm = jnp.min(jnp.where(work == mx, jf, 1e9), axis=0, keepdims=True)
            work = jnp.where(jf == jm, taken, work)
        ns_ref[0, 0:rows, :] = jnp.where(future | (work != taken), 1.0, 0.0).astype(bf16)
        if nsr > rows:
            ns_ref[0, rows:, :] = jnp.ones((nsr - rows, nq * QB), bf16)

    per_branch = n_steps // n_branch
    for k in range(n_branch):
        last_b = (k + 1) * per_branch * nq - 1
        nt = last_b // q_per_tile + 1
        rows = (last_b + 1) * (QB // SLC_LEN)
        pl.when(step_id // per_branch == k)(functools.partial(core, nt, rows))


def _cmp_attention(projn, projt, cn, ct, bias_c, bias_t, ov_t, t, nsr, nq):
    nb = t // QB
    ncp = t // CMP_STRIDE
    nslc = t // SLC_LEN
    per_kt = KT // QB
    per = per_kt // nq
    nw = nq + WINDOW // QB
    n_sel = min(SLC_TOPN, nslc)
    assert n_sel > 3
    n_steps = nb // nq
    n_branch = 8 if nb % 128 == 0 else (4 if nb % 64 == 0 else 1)
    resident = dict(pipeline_mode=pl.Buffered(1))

    def kw_spec(jw):
        return pl.BlockSpec((QB, HD), lambda hk, s, jw=jw: (jnp.maximum(s * nq - WINDOW // QB + jw, 0), _C_KW + hk))

    def vw_spec(jw):
        def imap(hk, s, jw=jw):
            bb = jnp.maximum(s * nq - WINDOW // QB + jw, 0)
            return (bb // per_kt, _R_VW + hk, bb % per_kt)
        return pl.BlockSpec((1, HD, QB), imap)

    return pl.pallas_call(
        functools.partial(_cmp_kernel, n_sel=n_sel, n_steps=n_steps, n_branch=n_branch, nq=nq),
        grid=(N_KV, n_steps),
        in_specs=[pl.BlockSpec((1, GROUP * HD, nq * QB), lambda hk, s: (s // per, hk, s % per)),
                  pl.BlockSpec((1, 1, ncp, HD), lambda hk, s: (0, hk, 0, 0), **resident),
                  pl.BlockSpec((1, 1, HD, ncp), lambda hk, s: (1, hk, 0, 0), **resident),
                  pl.BlockSpec((1, N_CMP_BIAS + 1, LANES, GROUP * LANES), lambda hk, s: (hk, 0, 0, 0), **resident),
                  pl.BlockSpec((nslc, ncp), lambda hk, s: (0, 0), **resident),
                  pl.BlockSpec((1, 16, nq * QB), lambda hk, s: (s // per, _R_G * 8 + hk, s % per))]
                 + [kw_spec(jw) for jw in range(nw)]
                 + [vw_spec(jw) for jw in range(nw)]
                 + [pl.BlockSpec((1, N_TOK_BIAS + 1, LANES, GROUP * LANES), lambda hk, s: (hk, 0, 0, 0), **resident)],
        out_specs=[pl.BlockSpec((nq * QB, GROUP * HD), lambda hk, s: (s, hk)),
                   pl.BlockSpec((1, nsr, nq * QB), lambda hk, s: (hk, 0, s))],
        out_shape=[jax.ShapeDtypeStruct((t, N_HEADS * HD), bf16),
                   jax.ShapeDtypeStruct((N_KV, nsr, t), bf16)],
        compiler_params=_params(("parallel", "arbitrary")),
        name="cmp_attention",
    )(projt, cn, ct, bias_c, ov_t, projt, *([projn] * nw), *([projt] * nw), bias_t)


def _slc_kernel(q_ref, ks_ref, vst_ref, ns_ref, bt_ref, ind_ref, g_ref, o1_ref, o_ref,
                qa_ref, s0_ref, s1_ref, p0_ref, p1_ref, acc_ref, st_ref, *, nq):
    b0 = pl.program_id(1) * nq
    gl = GROUP * LANES
    m_ref, al_ref, mx_ref, off_ref = (st_ref.at[pl.ds(r, 1), :] for r in range(4))
    qt = _query_lanes(q_ref, nq)
    gate = jax.nn.sigmoid(g_ref[0].astype(f32))

    n_chunks = qa_ref.shape[0]
    for jc in range(n_chunks):
        nsj = -ns_ref[0, jc * SEL_CHUNK:(jc + 1) * SEL_CHUNK, :]
        qa_ref[jc, 0:HD, :] = qt
        qa_ref[jc, HD:, :] = jnp.concatenate([nsj[:, h * QB:(h + 1) * QB]
                                              for h in range(nq) for _ in range(GROUP)], axis=1)

    n_tiles = vst_ref.shape[0]
    per_chunk = SEL_CHUNK * SLC_LEN // KT
    sub = KT // LANES

    far_bias = jnp.concatenate([bt_ref[0, N_TOK_BIAS - 1, 0:1, :]] * nq, axis=1)
    ones_rows = jnp.ones((8, KT), bf16)

    def scores(kt, s_ref, far):
        jc = kt // per_chunk
        r0 = pl.multiple_of(kt * KT, KT)
        i0 = pl.multiple_of((kt - jc * per_chunk) * KT, KT)
        kaug = jnp.concatenate([ks_ref[pl.ds(r0, KT), :], ind_ref[pl.ds(i0, KT), :]], axis=1)
        s = _dot(kaug, qa_ref[jc])
        if far:
            s_ref[...] = s
            mx_ref[...] = jnp.max(s, axis=0, keepdims=True) + far_bias
            off_ref[...] = far_bias
            return
        mxs = []
        for h in range(nq):
            cols = slice(h * gl, (h + 1) * gl)
            mx = None
            for jj in range(sub):
                rows = slice(jj * LANES, (jj + 1) * LANES)
                dl = b0 + h - (kt * sub + jj)
                sj = s[rows, cols] + bt_ref[0, jnp.clip(dl, 0, N_TOK_BIAS - 1)]
                s_ref[rows, cols] = sj
                mj = jnp.max(sj, axis=0, keepdims=True)
                mx = mj if mx is None else jnp.maximum(mx, mj)
            mxs.append(mx)
        mx_ref[...] = jnp.concatenate(mxs, axis=1) if nq > 1 else mxs[0]
        off_ref[...] = jnp.zeros_like(off_ref)

    def values(kt):
        return jnp.concatenate([vst_ref[kt], ones_rows], axis=0)

    def step(kt, s_cur, p_cur, s_nxt, p_prev, far):
        m_old = m_ref[...]
        m_new = jnp.maximum(m_old, mx_ref[...])
        shift = m_new - off_ref[...]
        al_old = al_ref[...]
        scores(jnp.minimum(kt + 1, n_tiles - 1), s_nxt, far)
        acc_ref[...] = al_old * acc_ref[...] + _dot(values(jnp.maximum(kt - 1, 0)), p_prev[...])
        p_cur[...] = jnp.exp2(s_cur[...] - shift).astype(bf16)
        m_ref[...] = m_new
        al_ref[...] = jnp.exp2(m_old - m_new)

    acc_ref[...] = jnp.zeros_like(acc_ref)
    m_ref[...] = jnp.full_like(m_ref, NEG)
    al_ref[...] = jnp.ones_like(al_ref)
    p1_ref[...] = jnp.zeros_like(p1_ref)
    scores(0, s0_ref, False)

    n_pairs = (b0 + nq - 1) // (2 * KT // QB) + 1
    n_far = jnp.maximum((b0 - (N_TOK_BIAS - 2)) // sub, 0)
    n_far_pairs = jnp.maximum((n_far - 1) // 2, 0)

    def pair(far):
        def body(i, carry):
            step(2 * i, s0_ref, p0_ref, s1_ref, p1_ref, far)
            step(2 * i + 1, s1_ref, p1_ref, s0_ref, p0_ref, far)
            return carry
        return body

    lax.fori_loop(0, n_far_pairs, pair(True), 0)
    lax.fori_loop(n_far_pairs, n_pairs, pair(False), 0)
    acc = al_ref[...] * acc_ref[...] + _dot(values(2 * n_pairs - 1), p1_ref[...])

    ost = acc[0:HD] / acc[HD:HD + 1]
    for h in range(nq):
        qs = slice(h * QB, (h + 1) * QB)
        for g in range(GROUP):
            ls = slice(h * gl + g * LANES, h * gl + (g + 1) * LANES)
            ds = slice(g * HD, (g + 1) * HD)
            o_ref[qs, ds] = ((ost[:, ls] * gate[g * 3 + 1:g * 3 + 2, qs]).T
                             + o1_ref[qs, ds].astype(f32)).astype(o_ref.dtype)


def _slc_attention(projn, projt, nsel_t, bias_t, ind, o1, t, nq):
    nb = t // QB
    per = KT // QB // nq
    nsr = nsel_t.shape[1]
    lanes = nq * GROUP * LANES
    resident = dict(pipeline_mode=pl.Buffered(1))
    return pl.pallas_call(
        functools.partial(_slc_kernel, nq=nq),
        grid=(N_KV, nb // nq),
        in_specs=[pl.BlockSpec((1, GROUP * HD, nq * QB), lambda hk, bq: (bq // per, hk, bq % per)),
                  pl.BlockSpec((t, HD), lambda hk, bq: (0, _C_KS + hk), **resident),
                  pl.BlockSpec((t // KT, HD, KT), lambda hk, bq: (0, _R_VS + hk, 0), **resident),
                  pl.BlockSpec((1, nsr, nq * QB), lambda hk, bq: (hk, 0, bq)),
                  pl.BlockSpec((1, N_TOK_BIAS + 1, LANES, GROUP * LANES), lambda hk, bq: (hk, 0, 0, 0), **resident),
                  pl.BlockSpec(ind.shape, lambda hk, bq: (0, 0), **resident),
                  pl.BlockSpec((1, 16, nq * QB), lambda hk, bq: (bq // per, _R_G * 8 + hk, bq % per)),
                  pl.BlockSpec((nq * QB, GROUP * HD), lambda hk, bq: (bq, hk))],
        out_specs=pl.BlockSpec((nq * QB, GROUP * HD), lambda hk, bq: (bq, hk)),
        out_shape=jax.ShapeDtypeStruct((t, N_HEADS * HD), bf16),
        scratch_shapes=[pltpu.VMEM((nsr // SEL_CHUNK, 2 * HD, lanes), bf16),
                        pltpu.VMEM((KT, lanes), f32),
                        pltpu.VMEM((KT, lanes), f32),
                        pltpu.VMEM((KT, lanes), bf16),
                        pltpu.VMEM((KT, lanes), bf16),
                        pltpu.VMEM((HD + 8, lanes), f32),
                        pltpu.VMEM((8, lanes), f32)],
        compiler_params=_params(("parallel", "arbitrary")),
        name="slc_attention",
    )(projt, projn, projt, nsel_t, bias_t, ind, projt, o1)


def _layer_norm(z, g, b):
    mu = jnp.mean(z, axis=-1, keepdims=True)
    var = jnp.mean(jnp.square(z - mu), axis=-1, keepdims=True)
    return (z - mu) * lax.rsqrt(var + LN_EPS) * g + b


def _post_kernel(o2_ref, ya_ref, mb_ref, x_ref, p_ref, wao_ref, wmix_ref, wple_ref, wpg_ref,
                 g_ref, b_ref, x1_ref, r2_ref, *, alpha):
    yb = _dot(o2_ref[...], wao_ref[...])
    mixed = ya_ref[...].astype(f32) + jax.nn.sigmoid(mb_ref[...].astype(f32)) * yb
    z = alpha * x_ref[...] + _dot(mixed.astype(bf16), wmix_ref[...])
    x1 = _layer_norm(z, g_ref[...], b_ref[...])
    x1b = x1.astype(bf16)
    x1_ref[...] = x1b
    ple = _dot(p_ref[...].astype(bf16), wple_ref[...]) * jax.nn.sigmoid(_dot(x1b, wpg_ref[...]))
    r2_ref[...] = alpha * x1 + ple


def _post_attention(o2, ya, mab, x2d, p2d, wao, wmix, wple, wpg, g1, b1, alpha, tm):
    t = x2d.shape[0]
    row = lambda w: pl.BlockSpec((tm, w), lambda i: (i, 0))
    full = lambda a: pl.BlockSpec(a.shape, lambda i: (0, 0), pipeline_mode=pl.Buffered(1))
    return pl.pallas_call(
        functools.partial(_post_kernel, alpha=alpha),
        grid=(t // tm,),
        in_specs=[row(D_MODEL), row(D_MODEL),
                  pl.BlockSpec((tm, D_MODEL), lambda i: (i, 1)),
                  row(D_MODEL), row(PLE_DIM),
                  full(wao), full(wmix), full(wple), full(wpg), full(g1), full(b1)],
        out_specs=[row(D_MODEL), row(D_MODEL)],
        out_shape=[jax.ShapeDtypeStruct((t, D_MODEL), bf16),
                   jax.ShapeDtypeStruct((t, D_MODEL), f32)],
        compiler_params=_params(("parallel",)),
        name="post_attention",
    )(o2, ya, mab, x2d, p2d, wao, wmix, wple, wpg, g1, b1)


def _mlp_kernel(x1_ref, r2_ref, wu_ref, wd_ref, g_ref, b_ref, o_ref):
    k = pl.program_id(1)

    @pl.when(k == 0)
    def _():
        o_ref[...] = r2_ref[...]

    h = jnp.square(jnp.maximum(_dot(x1_ref[...], wu_ref[...]), 0.0))
    o_ref[...] += _dot(h.astype(bf16), wd_ref[...])

    @pl.when(k == pl.num_programs(1) - 1)
    def _():
        o_ref[...] = _layer_norm(o_ref[...], g_ref[...], b_ref[...])


def _mlp(x1b, r2, wu, wd, g2, b2, tm, fc):
    t = x1b.shape[0]
    return pl.pallas_call(
        _mlp_kernel,
        grid=(t // tm, D_FF // fc),
        in_specs=[pl.BlockSpec((tm, D_MODEL), lambda i, k: (i, 0)),
                  pl.BlockSpec((tm, D_MODEL), lambda i, k: (i, 0)),
                  pl.BlockSpec((D_MODEL, fc), lambda i, k: (0, k)),
                  pl.BlockSpec((fc, D_MODEL), lambda i, k: (k, 0)),
                  pl.BlockSpec((1, D_MODEL), lambda i, k: (0, 0)),
                  pl.BlockSpec((1, D_MODEL), lambda i, k: (0, 0))],
        out_specs=pl.BlockSpec((tm, D_MODEL), lambda i, k: (i, 0)),
        out_shape=jax.ShapeDtypeStruct((t, D_MODEL), f32),
        compiler_params=_params(("parallel", "arbitrary")),
        name="mlp",
    )(x1b, r2, wu, wd, g2, b2)


def _layer(x2d, p2d, w_in, conv_w, pe_k, w1_k, w2_k, pe_v, w1_v, w2_v, w_conv_out, w_attn_out,
           w_mix_out, ln1_g, ln1_b, w_mlp_up, w_mlp_down, w_ple, w_ple_gate, ln2_g, ln2_b,
           rel_bias, alpha):
    t = x2d.shape[0]
    assert t % 1024 == 0 and x2d.shape[1] == D_MODEL
    kvw = N_KV * HD
    o_q = 3 * D_MODEL
    o_kc = o_q + N_HEADS * HD
    o_ng = o_kc + 6 * kvw
    o_ma = o_ng + N_HEADS * 3

    n_in = w_in.shape[1]
    cast_rows = max(r for r in range(16, D_MODEL + 1, 16) if n_in % r == 0)
    w_bf = _to_bf16(w_in.T, cast_rows, "cast_w_in")

    def cols(a, n):
        return w_bf[a:a + n]

    tn = kvw
    conv_tiles = 3 * D_MODEL // tn
    ks_tile, kw_tile = (o_kc + 2 * kvw) // tn, (o_kc + 4 * kvw) // tn
    assert _C_KS * LANES == conv_tiles * tn and _C_KW * LANES == (conv_tiles + 1) * tn

    def nat_tile(j):
        return jnp.where(j < conv_tiles, j, jnp.where(j == conv_tiles, ks_tile, kw_tile))

    wng = jnp.pad(cols(o_ng, N_HEADS * 3).reshape(N_KV, GROUP * 3, D_MODEL), ((0, 0), (0, 16 - GROUP * 3), (0, 0)))
    w_t = jnp.concatenate(
        [cols(o_q, N_HEADS * HD), cols(o_kc + 3 * kvw, kvw), cols(o_kc + 5 * kvw, kvw),
         wng.reshape(N_KV * 16, D_MODEL),
         jnp.zeros((_N_T - _R_G * LANES - N_KV * 16, D_MODEL), bf16)], axis=0)
    row_scale = jnp.concatenate([jnp.full((N_HEADS * HD, 1), HD ** -0.5 * LOG2E, f32),
                                 jnp.ones((_N_T - N_HEADS * HD, 1), f32)], axis=0)

    xb = x2d.astype(bf16)
    projn = _proj_nat(xb, w_bf, 2048, tn, "proj_nat", _N_NAT, nat_tile)
    mab = _proj_nat(xb, cols(o_ma, 2 * D_MODEL), 2048, 1024, "proj_gate")
    kvc = _proj_chunked(xb, w_bf, 1024, 2 * kvw, o_kc // (2 * kvw))
    projt = _proj_t(w_t, xb, row_scale, _N_T, KT)

    ya = _mixer_a(projn, mab, conv_w, w_conv_out.astype(bf16), 512)

    half = CMP_STRIDE * HD
    w1ab = jnp.stack([jnp.concatenate([w[:half], w[half:]], axis=1) for w in (w1_k, w1_v)]).astype(bf16)
    pe2 = jnp.stack([jnp.stack([pe[:CMP_STRIDE], pe[CMP_STRIDE:]], axis=1) for pe in (pe_k, pe_v)])
    pe2 = jnp.pad(pe2, ((0, 0), (0, 0), (0, 6), (0, 0)))
    w2 = jnp.stack([w2_k, w2_v]).astype(bf16)
    cn, ct = _compress(kvc, pe2, w1ab, w2, jnp.swapaxes(w2, 1, 2))

    bias_t = _bias_tiles(rel_bias, N_TOK_BIAS, 1, 0, WINDOW // QB, WINDOW - 1, "bias_tok")
    bias_c = _bias_tiles(rel_bias, N_CMP_BIAS, CMP_STRIDE, CMP_LEN - 1, 0, -1, "bias_cmp")

    ncp = t // CMP_STRIDE
    nslc = t // SLC_LEN
    nsr = -(-nslc // SEL_CHUNK) * SEL_CHUNK
    ci = jnp.arange(ncp)[None, :] * CMP_STRIDE
    sj = jnp.arange(nslc)[:, None] * SLC_LEN
    ov_t = ((ci < sj + SLC_LEN) & (ci + CMP_LEN > sj)).astype(bf16)
    pos = jnp.arange(min(t, SEL_CHUNK * SLC_LEN))[:, None] // SLC_LEN
    ind = jnp.where(pos == jnp.arange(SEL_CHUNK)[None, :], SEL_BIG, 0.0).astype(bf16)

    o1, nsel_t = _cmp_attention(projn, projt, cn, ct, bias_c, bias_t, ov_t, t, nsr, CMP_NQ)
    o2 = _slc_attention(projn, projt, nsel_t, bias_t, ind, o1, t, SLC_NQ)

    x1b, r2 = _post_attention(o2, ya, mab, x2d, p2d, w_attn_out.astype(bf16), w_mix_out.astype(bf16),
                              w_ple.astype(bf16), w_ple_gate.astype(bf16),
                              ln1_g[None, :], ln1_b[None, :], alpha, 256)
    return _mlp(x1b, r2, w_mlp_up.astype(bf16), w_mlp_down.astype(bf16), ln2_g[None, :], ln2_b[None, :], 1024, 512)


def kernel(x, p, w_in, conv_w, cmp_pe_k, cmp_w1_k, cmp_w2_k, cmp_pe_v, cmp_w1_v, cmp_w2_v, w_conv_out, w_attn_out, w_mix_out, ln1_g, ln1_b, w_mlp_up, w_mlp_down, w_ple, w_ple_gate, ln2_g, ln2_b, rel_bias):
    bsz, t, d = x.shape
    assert bsz == 1
    depth = w_in.shape[0]
    alpha = (2 * depth) ** 0.25
    x2d = x[0]
    for i in range(depth):
        x2d = _layer(x2d, p[i, 0], w_in[i], conv_w[i], cmp_pe_k[i], cmp_w1_k[i], cmp_w2_k[i],
                     cmp_pe_v[i], cmp_w1_v[i], cmp_w2_v[i], w_conv_out[i], w_attn_out[i], w_mix_out[i],
                     ln1_g[i], ln1_b[i], w_mlp_up[i], w_mlp_down[i], w_ple[i], w_ple_gate[i],
                     ln2_g[i], ln2_b[i], rel_bias, alpha)
    return x2d[None]
```

```python
import functools
import math

import jax
import jax.numpy as jnp
from jax import lax
from jax.experimental import pallas as pl
from jax.experimental.pallas import tpu as pltpu

f32 = jnp.float32
bf16 = jnp.bfloat16
i32 = jnp.int32

D_MODEL = 2048
N_HEADS = 16
N_KV = 4
GROUP = N_HEADS // N_KV
HD = 128
CMP_LEN = 32
CMP_STRIDE = 16
CMP_HIDDEN = 2 * HD
SLC_LEN = 64
SLC_TOPN = 16
WINDOW = 512
QB = 128
D_FF = 4 * D_MODEL
PLE_DIM = 256
REL_BUCKETS = 32
REL_EXACT = REL_BUCKETS // 2
REL_MAX_DIST = 4096
LN_EPS = 1e-5
NEG = -1e30
LOG2E = 1.4426950408889634

LANES = 128
KT = 512
SEL_BIG = 2.0 ** 100
SEL_CHUNK = 128
SLC_NQ = 1
CMP_NQ = 2
N_TOK_BIAS = 25
N_CMP_BIAS = 40
VMEM_LIMIT = 56 * 1024 * 1024

_C_KS, _C_KW = 48, 52
_N_NAT = 56 * LANES
_R_VS, _R_VW, _R_G = 16, 20, 24
_N_T = 25 * LANES


def _dot(a, b):
    return jnp.dot(a, b, preferred_element_type=f32)


def _dot_nt(a, b):
    return lax.dot_general(a, b, (((1,), (1,)), ((), ())), preferred_element_type=f32)


def _params(sem, vmem=VMEM_LIMIT):
    return pltpu.CompilerParams(dimension_semantics=sem, vmem_limit_bytes=vmem)


def _mm_kernel(x_ref, w_ref, o_ref):
    o_ref[...] = _dot_nt(x_ref[...], w_ref[...]).astype(o_ref.dtype)


def _proj_nat(xb, wt, tm, tn, name, n=None, col_tile=lambda j: j):
    m, k = xb.shape
    n = wt.shape[0] if n is None else n
    return pl.pallas_call(
        _mm_kernel,
        grid=(m // tm, n // tn),
        in_specs=[pl.BlockSpec((tm, k), lambda i, j: (i, 0)),
                  pl.BlockSpec((tn, k), lambda i, j: (col_tile(j), 0))],
        out_specs=pl.BlockSpec((tm, tn), lambda i, j: (i, j)),
        out_shape=jax.ShapeDtypeStruct((m, n), bf16),
        compiler_params=_params(("parallel", "arbitrary")),
        name=name,
    )(xb, wt)


def _mmt_kernel(w_ref, x_ref, s_ref, o_ref):
    o_ref[0] = (_dot_nt(w_ref[...], x_ref[...]) * s_ref[...]).astype(o_ref.dtype)


def _proj_t(w, xb, row_scale, tm, tn):
    n, k = w.shape
    t = xb.shape[0]
    return pl.pallas_call(
        _mmt_kernel,
        grid=(n // tm, t // tn),
        in_specs=[pl.BlockSpec((tm, k), lambda i, j: (i, 0)),
                  pl.BlockSpec((tn, k), lambda i, j: (j, 0)),
                  pl.BlockSpec((tm, 1), lambda i, j: (i, 0))],
        out_specs=pl.BlockSpec((1, tm, tn), lambda i, j: (j, i, 0)),
        out_shape=jax.ShapeDtypeStruct((t // tn, n, tn), bf16),
        compiler_params=_params(("parallel", "arbitrary")),
        name="proj_t",
    )(w, xb, row_scale)


def _cast_kernel(x_ref, o_ref):
    o_ref[...] = x_ref[...].astype(o_ref.dtype)


def _to_bf16(w, tm, name):
    r, c = w.shape
    return pl.pallas_call(
        _cast_kernel,
        grid=(r // tm,),
        in_specs=[pl.BlockSpec((tm, c), lambda i: (i, 0))],
        out_specs=pl.BlockSpec((tm, c), lambda i: (i, 0)),
        out_shape=jax.ShapeDtypeStruct((r, c), bf16),
        compiler_params=_params(("parallel",)),
        name=name,
    )(w)


def _ya_kernel(bg_ref, cg_ref, hx_ref, cgh_ref, hxh_ref, cw_ref, w_ref, ma_ref, o_ref, a_ref):
    i = pl.program_id(0)
    tm = bg_ref.shape[0]
    cc = 512
    row = lax.broadcasted_iota(i32, (tm, cc), 0)
    for c in range(D_MODEL // cc):
        sl = slice(c * cc, (c + 1) * cc)
        u = cg_ref[:, sl].astype(f32) * hx_ref[:, sl].astype(f32)
        uh = cgh_ref[:, sl].astype(f32) * hxh_ref[:, sl].astype(f32)
        uh = jnp.where(i > 0, uh, 0.0)
        u1 = jnp.where(row == 0, uh[15:16, :], pltpu.roll(u, 1, 0))
        u2 = jnp.where(row == 0, uh[14:15, :], jnp.where(row == 1, uh[15:16, :], pltpu.roll(u, 2, 0)))
        cw = cw_ref[:, sl]
        conv = cw[0:1, :] * u2 + cw[1:2, :] * u1 + cw[2:3, :] * u
        a_ref[:, sl] = (bg_ref[:, sl].astype(f32) * conv).astype(bf16)
    y = _dot(a_ref[...], w_ref[...])
    o_ref[...] = jax.nn.sigmoid(ma_ref[...].astype(f32)) * y


def _mixer_a(projn, mab, conv_w, w_out, tm):
    t = projn.shape[0]
    hb = tm // 16
    wide = lambda c: pl.BlockSpec((tm, D_MODEL), lambda i, c=c: (i, c))
    halo = lambda c: pl.BlockSpec((16, D_MODEL), lambda i, c=c: (jnp.maximum(i * hb - 1, 0), c))
    return pl.pallas_call(
        _ya_kernel,
        grid=(t // tm,),
        in_specs=[wide(0), wide(1), wide(2), halo(1), halo(2),
                  pl.BlockSpec((3, D_MODEL), lambda i: (0, 0)),
                  pl.BlockSpec((D_MODEL, D_MODEL), lambda i: (0, 0)),
                  wide(0)],
        out_specs=pl.BlockSpec((tm, D_MODEL), lambda i: (i, 0)),
        out_shape=jax.ShapeDtypeStruct((t, D_MODEL), f32),
        scratch_shapes=[pltpu.VMEM((tm, D_MODEL), bf16)],
        compiler_params=_params(("parallel",)),
        name="mixer_a",
    )(projn, projn, projn, projn, projn, conv_w, w_out, mab)


def _compress_kernel(x_ref, w_ref, pe_ref, w2_ref, w2t_ref, on_ref, ot_ref, acc_ref, pet_ref):
    l = pl.program_id(1)
    nl = pl.num_programs(1)
    n = x_ref.shape[0]

    @pl.when(l == 0)
    def _():
        acc_ref[...] = jnp.zeros_like(acc_ref)
        pet_ref[...] = jnp.zeros_like(pet_ref)

    wblk = w_ref[0]
    for h in range(N_KV):
        acc_ref[h] += _dot(x_ref[:, h * HD:(h + 1) * HD], wblk)
    pet_ref[...] += _dot(pe_ref[0, 0].astype(bf16), wblk)

    @pl.when(l == nl - 1)
    def _():
        r = pet_ref[...]
        peterm = r[0:1, :CMP_HIDDEN] + r[1:2, CMP_HIDDEN:]
        for h in range(N_KV):
            ab = acc_ref[h]
            hid = ab[:, :CMP_HIDDEN] + pltpu.roll(ab[:, CMP_HIDDEN:], n - 1, 0) + peterm
            gl = jax.nn.gelu(hid).astype(bf16)
            on_ref[0, h] = _dot(gl, w2_ref[0]).astype(bf16)
            ot_ref[0, h] = _dot_nt(w2t_ref[0], gl).astype(bf16)


def _chunked_mm_kernel(x_ref, w_ref, o_ref, acc_ref):
    acc = _dot_nt(x_ref[...], w_ref[...])
    n = w_ref.shape[0]
    rows = o_ref.shape[0]
    for c in range(n // LANES):
        acc_ref[c] = acc[:, c * LANES:(c + 1) * LANES]
    for l in range(CMP_STRIDE):
        for c in range(n // LANES):
            o_ref[:, l * n + c * LANES:l * n + (c + 1) * LANES] = (
                acc_ref[c, pl.ds(l, rows, stride=CMP_STRIDE), :].astype(o_ref.dtype))


def _proj_chunked(xb, w, tm, n, col_tile):
    m, k = xb.shape
    return pl.pallas_call(
        _chunked_mm_kernel,
        grid=(m // tm,),
        in_specs=[pl.BlockSpec((tm, k), lambda i: (i, 0)),
                  pl.BlockSpec((n, k), lambda i: (col_tile, 0))],
        out_specs=pl.BlockSpec((tm // CMP_STRIDE, CMP_STRIDE * n), lambda i: (i, 0)),
        out_shape=jax.ShapeDtypeStruct((m // CMP_STRIDE, CMP_STRIDE * n), bf16),
        scratch_shapes=[pltpu.VMEM((n // LANES, tm, LANES), f32)],
        compiler_params=_params(("parallel",)),
        name="proj_cmp",
    )(xb, w)


def _compress(xr, pe2, w1ab, w2, w2t):
    n = xr.shape[0]
    return pl.pallas_call(
        _compress_kernel,
        grid=(2, CMP_STRIDE),
        in_specs=[pl.BlockSpec((n, N_KV * HD), lambda kv, l: (0, 2 * l + kv)),
                  pl.BlockSpec((1, HD, 2 * CMP_HIDDEN), lambda kv, l: (kv, l, 0)),
                  pl.BlockSpec((1, 1, 8, HD), lambda kv, l: (kv, l, 0, 0)),
                  pl.BlockSpec((1, CMP_HIDDEN, HD), lambda kv, l: (kv, 0, 0)),
                  pl.BlockSpec((1, HD, CMP_HIDDEN), lambda kv, l: (kv, 0, 0))],
        out_specs=[pl.BlockSpec((1, N_KV, n, HD), lambda kv, l: (kv, 0, 0, 0)),
                   pl.BlockSpec((1, N_KV, HD, n), lambda kv, l: (kv, 0, 0, 0))],
        out_shape=[jax.ShapeDtypeStruct((2, N_KV, n, HD), bf16),
                   jax.ShapeDtypeStruct((2, N_KV, HD, n), bf16)],
        scratch_shapes=[pltpu.VMEM((N_KV, n, 2 * CMP_HIDDEN), f32),
                        pltpu.VMEM((8, 2 * CMP_HIDDEN), f32)],
        compiler_params=_params(("arbitrary", "arbitrary")),
        name="compress",
    )(xr, w1ab, pe2, w2, w2t)


def _rel_bucket(dist):
    n = jnp.maximum(dist, 0)
    nf = jnp.maximum(n, 1).astype(f32)
    large = REL_EXACT + (jnp.log(nf / REL_EXACT) / math.log(REL_MAX_DIST / REL_EXACT)
                         * (REL_BUCKETS - REL_EXACT)).astype(i32)
    large = jnp.minimum(large, REL_BUCKETS - 1)
    return jnp.where(n < REL_EXACT, n, large)


def _bias_kernel(tbl_ref, o_ref, *, kstride, koff, n_reg, edge_o, edge_max):
    hk = pl.program_id(0)
    ki = lax.broadcasted_iota(i32, (LANES, LANES), 0)
    qi = lax.broadcasted_iota(i32, (LANES, LANES), 1)
    rows = [jnp.broadcast_to(tbl_ref[pl.ds(hk * GROUP + g, 1), :], (LANES, LANES)) for g in range(GROUP)]

    def tile(o, carry):
        is_edge = o == n_reg
        dist = jnp.where(is_edge, edge_o, o) * LANES + qi - kstride * ki - koff
        valid = (dist >= 0) & (dist <= jnp.where(is_edge, edge_max, 2 ** 30))
        bucket = _rel_bucket(dist)
        for g in range(GROUP):
            acc = jnp.take_along_axis(rows[g], bucket, axis=1)
            o_ref[0, o, :, g * LANES:(g + 1) * LANES] = jnp.where(valid, acc * LOG2E, NEG)
        return carry

    lax.fori_loop(0, n_reg + 1, tile, 0)


def _bias_tiles(rel_bias, n_reg, kstride, koff, edge_o, edge_max, name):
    return pl.pallas_call(
        functools.partial(_bias_kernel, kstride=kstride, koff=koff, n_reg=n_reg, edge_o=edge_o,
                          edge_max=edge_max),
        grid=(N_KV,),
        in_specs=[pl.BlockSpec((N_HEADS, LANES), lambda hk: (0, 0))],
        out_specs=pl.BlockSpec((1, n_reg + 1, LANES, GROUP * LANES), lambda hk: (hk, 0, 0, 0)),
        out_shape=jax.ShapeDtypeStruct((N_KV, n_reg + 1, LANES, GROUP * LANES), f32),
        compiler_params=_params(("parallel",)),
        name=name,
    )(jnp.pad(rel_bias.astype(f32).T, ((0, 0), (0, LANES - REL_BUCKETS))))


def _query_lanes(q_ref, nq):
    qb = q_ref[0]
    return jnp.concatenate([qb[g * HD:(g + 1) * HD, h * QB:(h + 1) * QB]
                            for h in range(nq) for g in range(GROUP)], axis=1)


def _cmp_kernel(q_ref, kc_ref, vct_ref, bc_ref, ov_ref, g_ref, *rest, n_sel, n_steps, n_branch, nq):
    nw = nq + WINDOW // QB
    kws, vws = rest[:nw], rest[nw:2 * nw]
    bt_ref, o_ref, ns_ref = rest[2 * nw:]
    step_id = pl.program_id(1)
    b0 = step_id * nq
    nsr = ns_ref.shape[1]
    gl = GROUP * LANES
    qt = _query_lanes(q_ref, nq)
    gate = jax.nn.sigmoid(g_ref[0].astype(f32))
    cmp_per_q = QB // CMP_STRIDE
    q_per_tile = LANES // cmp_per_q

    def core(nt, rows):
        keys = nt * LANES
        s = _dot(kc_ref[0, 0, 0:keys, :], qt)
        col_parts = []
        for h in range(nq):
            parts = []
            for c in range(nt):
                o = b0 + h - q_per_tile * c
                idx = jnp.where(o < 0, N_CMP_BIAS, jnp.minimum(o, N_CMP_BIAS - 1))
                parts.append(s[c * LANES:(c + 1) * LANES, h * gl:(h + 1) * gl] + bc_ref[0, idx])
            col_parts.append(jnp.concatenate(parts, axis=0) if nt > 1 else parts[0])
        s = jnp.concatenate(col_parts, axis=1) if nq > 1 else col_parts[0]
        m = jnp.max(s, axis=0, keepdims=True)
        e = jnp.exp2(s - m).astype(bf16)
        lhs = jnp.concatenate([vct_ref[0, 0, :, 0:keys], jnp.ones((16, keys), bf16), ov_ref[0:rows, 0:keys]], axis=0)
        acc = _dot(lhs, e)
        inv = jnp.where(m > 0.5 * NEG, 1.0 / acc[HD:HD + 1], 0.0)
        oct_ = acc[0:HD] * inv
        imp_g = acc[HD + 16:] * inv

        sw = _dot(jnp.concatenate([r[...] for r in kws], axis=0), qt)
        col_parts = []
        for h in range(nq):
            parts = []
            for jw in range(nw):
                dl = h + WINDOW // QB - jw
                if dl < 0 or dl > WINDOW // QB:
                    parts.append(jnp.full((LANES, gl), NEG, f32))
                    continue
                tile = N_TOK_BIAS if dl == WINDOW // QB else dl
                before_start = jnp.where(b0 + h - dl >= 0, 0.0, NEG)
                parts.append(sw[jw * LANES:(jw + 1) * LANES, h * gl:(h + 1) * gl] + bt_ref[0, tile] + before_start)
            col_parts.append(jnp.concatenate(parts, axis=0))
        sw = jnp.concatenate(col_parts, axis=1) if nq > 1 else col_parts[0]
        pw = jnp.exp2(sw - jnp.max(sw, axis=0, keepdims=True)).astype(bf16)
        vw = jnp.concatenate([r[0] for r in vws], axis=1)
        accw = _dot(jnp.concatenate([vw, jnp.ones((8, nw * LANES), bf16)], axis=0), pw)
        owt = accw[0:HD] / accw[HD:HD + 1]
        for h in range(nq):
            qs = slice(h * QB, (h + 1) * QB)
            for g in range(GROUP):
                ls = slice(h * gl + g * LANES, h * gl + (g + 1) * LANES)
                both = oct_[:, ls] * gate[g * 3:g * 3 + 1, qs] + owt[:, ls] * gate[g * 3 + 2:g * 3 + 3, qs]
                o_ref[qs, g * HD:(g + 1) * HD] = both.T.astype(o_ref.dtype)

        imps = []
        for h in range(nq):
            ig = None
            for g in range(GROUP):
                part = imp_g[:, h * gl + g * LANES:h * gl + (g + 1) * LANES]
                ig = part if ig is None else ig + part
            imps.append(ig)
        imp = jnp.concatenate(imps, axis=1) if nq > 1 else imps[0]

        j = lax.broadcasted_iota(i32, (rows, nq * QB), 0)
        tq = b0 * QB + lax.broadcasted_iota(i32, (rows, nq * QB), 1)
        jt = jnp.right_shift(tq, SLC_LEN.bit_length() - 1)
        forced = (j == 0) | (j == jt) | (j == jt - 1)
        future = j * SLC_LEN > tq
        taken = -3e38
        work = jnp.where(forced, taken, jnp.where(future, -1.0, imp))
        jf = j.astype(f32)
        for _ in range(n_sel - 3):
            mx = jnp.max(work, axis=0, keepdims=True)
            jm = jnp.min(jnp.where(work == mx, jf, 1e9), axis=0, keepdims=True)
            work = jnp.where(jf == jm, taken, work)
        ns_ref[0, 0:rows, :] = jnp.where(future | (work != taken), 1.0, 0.0).astype(bf16)
        if nsr > rows:
            ns_ref[0, rows:, :] = jnp.ones((nsr - rows, nq * QB), bf16)

    per_branch = n_steps // n_branch
    for k in range(n_branch):
        last_b = (k + 1) * per_branch * nq - 1
        nt = last_b // q_per_tile + 1
        rows = (last_b + 1) * (QB // SLC_LEN)
        pl.when(step_id // per_branch == k)(functools.partial(core, nt, rows))


def _cmp_attention(projn, projt, cn, ct, bias_c, bias_t, ov_t, t, nsr, nq):
    nb = t // QB
    ncp = t // CMP_STRIDE
    nslc = t // SLC_LEN
    per_kt = KT // QB
    per = per_kt // nq
    nw = nq + WINDOW // QB
    n_sel = min(SLC_TOPN, nslc)
    assert n_sel > 3
    n_steps = nb // nq
    n_branch = 8 if nb % 128 == 0 else (4 if nb % 64 == 0 else 1)
    resident = dict(pipeline_mode=pl.Buffered(1))

    def kw_spec(jw):
        return pl.BlockSpec((QB, HD), lambda hk, s, jw=jw: (jnp.maximum(s * nq - WINDOW // QB + jw, 0), _C_KW + hk))

    def vw_spec(jw):
        def imap(hk, s, jw=jw):
            bb = jnp.maximum(s * nq - WINDOW // QB + jw, 0)
            return (bb // per_kt, _R_VW + hk, bb % per_kt)
        return pl.BlockSpec((1, HD, QB), imap)

    return pl.pallas_call(
        functools.partial(_cmp_kernel, n_sel=n_sel, n_steps=n_steps, n_branch=n_branch, nq=nq),
        grid=(N_KV, n_steps),
        in_specs=[pl.BlockSpec((1, GROUP * HD, nq * QB), lambda hk, s: (s // per, hk, s % per)),
                  pl.BlockSpec((1, 1, ncp, HD), lambda hk, s: (0, hk, 0, 0), **resident),
                  pl.BlockSpec((1, 1, HD, ncp), lambda hk, s: (1, hk, 0, 0), **resident),
                  pl.BlockSpec((1, N_CMP_BIAS + 1, LANES, GROUP * LANES), lambda hk, s: (hk, 0, 0, 0), **resident),
                  pl.BlockSpec((nslc, ncp), lambda hk, s: (0, 0), **resident),
                  pl.BlockSpec((1, 16, nq * QB), lambda hk, s: (s // per, _R_G * 8 + hk, s % per))]
                 + [kw_spec(jw) for jw in range(nw)]
                 + [vw_spec(jw) for jw in range(nw)]
                 + [pl.BlockSpec((1, N_TOK_BIAS + 1, LANES, GROUP * LANES), lambda hk, s: (hk, 0, 0, 0), **resident)],
        out_specs=[pl.BlockSpec((nq * QB, GROUP * HD), lambda hk, s: (s, hk)),
                   pl.BlockSpec((1, nsr, nq * QB), lambda hk, s: (hk, 0, s))],
        out_shape=[jax.ShapeDtypeStruct((t, N_HEADS * HD), f32),
                   jax.ShapeDtypeStruct((N_KV, nsr, t), bf16)],
        compiler_params=_params(("parallel", "arbitrary")),
        name="cmp_attention",
    )(projt, cn, ct, bias_c, ov_t, projt, *([projn] * nw), *([projt] * nw), bias_t)


def _slc_kernel(q_ref, ks_ref, vst_ref, ns_ref, bt_ref, ind_ref, o1_ref, g_ref, o_ref,
                qa_ref, s0_ref, s1_ref, p0_ref, p1_ref, acc_ref, st_ref, *, nq):
    b0 = pl.program_id(1) * nq
    gl = GROUP * LANES
    m_ref, al_ref, mx_ref, off_ref = (st_ref.at[pl.ds(r, 1), :] for r in range(4))
    qt = _query_lanes(q_ref, nq)
    gate = jax.nn.sigmoid(g_ref[0].astype(f32))

    n_chunks = qa_ref.shape[0]
    for jc in range(n_chunks):
        nsj = -ns_ref[0, jc * SEL_CHUNK:(jc + 1) * SEL_CHUNK, :]
        qa_ref[jc, 0:HD, :] = qt
        qa_ref[jc, HD:, :] = jnp.concatenate([nsj[:, h * QB:(h + 1) * QB]
                                              for h in range(nq) for _ in range(GROUP)], axis=1)

    n_tiles = vst_ref.shape[0]
    per_chunk = SEL_CHUNK * SLC_LEN // KT
    sub = KT // LANES

    far_bias = jnp.concatenate([bt_ref[0, N_TOK_BIAS - 1, 0:1, :]] * nq, axis=1)
    ones_rows = jnp.ones((8, KT), bf16)

    def scores(kt, s_ref, far):
        jc = kt // per_chunk
        r0 = pl.multiple_of(kt * KT, KT)
        i0 = pl.multiple_of((kt - jc * per_chunk) * KT, KT)
        kaug = jnp.concatenate([ks_ref[pl.ds(r0, KT), :], ind_ref[pl.ds(i0, KT), :]], axis=1)
        s = _dot(kaug, qa_ref[jc])
        if far:
            s_ref[...] = s
            mx_ref[...] = jnp.max(s, axis=0, keepdims=True) + far_bias
            off_ref[...] = far_bias
            return
        mxs = []
        for h in range(nq):
            cols = slice(h * gl, (h + 1) * gl)
            mx = None
            for jj in range(sub):
                rows = slice(jj * LANES, (jj + 1) * LANES)
                dl = b0 + h - (kt * sub + jj)
                sj = s[rows, cols] + bt_ref[0, jnp.clip(dl, 0, N_TOK_BIAS - 1)]
                s_ref[rows, cols] = sj
                mj = jnp.max(sj, axis=0, keepdims=True)
                mx = mj if mx is None else jnp.maximum(mx, mj)
            mxs.append(mx)
        mx_ref[...] = jnp.concatenate(mxs, axis=1) if nq > 1 else mxs[0]
        off_ref[...] = jnp.zeros_like(off_ref)

    def values(kt):
        return jnp.concatenate([vst_ref[kt], ones_rows], axis=0)

    def step(kt, s_cur, p_cur, s_nxt, p_prev, far):
        m_old = m_ref[...]
        m_new = jnp.maximum(m_old, mx_ref[...])
        shift = m_new - off_ref[...]
        al_old = al_ref[...]
        scores(jnp.minimum(kt + 1, n_tiles - 1), s_nxt, far)
        acc_ref[...] = al_old * acc_ref[...] + _dot(values(jnp.maximum(kt - 1, 0)), p_prev[...])
        p_cur[...] = jnp.exp2(s_cur[...] - shift).astype(bf16)
        m_ref[...] = m_new
        al_ref[...] = jnp.exp2(m_old - m_new)

    acc_ref[...] = jnp.zeros_like(acc_ref)
    m_ref[...] = jnp.full_like(m_ref, NEG)
    al_ref[...] = jnp.ones_like(al_ref)
    p1_ref[...] = jnp.zeros_like(p1_ref)
    scores(0, s0_ref, False)

    n_pairs = (b0 + nq - 1) // (2 * KT // QB) + 1
    n_far = jnp.maximum((b0 - (N_TOK_BIAS - 2)) // sub, 0)
    n_far_pairs = jnp.maximum((n_far - 1) // 2, 0)

    def pair(far):
        def body(i, carry):
            step(2 * i, s0_ref, p0_ref, s1_ref, p1_ref, far)
            step(2 * i + 1, s1_ref, p1_ref, s0_ref, p0_ref, far)
            return carry
        return body

    lax.fori_loop(0, n_far_pairs, pair(True), 0)
    lax.fori_loop(n_far_pairs, n_pairs, pair(False), 0)
    acc = al_ref[...] * acc_ref[...] + _dot(values(2 * n_pairs - 1), p1_ref[...])

    ost = acc[0:HD] / acc[HD:HD + 1]
    for h in range(nq):
        qs = slice(h * QB, (h + 1) * QB)
        for g in range(GROUP):
            ls = slice(h * gl + g * LANES, h * gl + (g + 1) * LANES)
            ds = slice(g * HD, (g + 1) * HD)
            o_ref[qs, ds] = ((ost[:, ls] * gate[g * 3 + 1:g * 3 + 2, qs]).T + o1_ref[qs, ds]).astype(o_ref.dtype)


def _slc_attention(projn, projt, nsel_t, bias_t, ind, o1, t, nq):
    nb = t // QB
    per = KT // QB // nq
    nsr = nsel_t.shape[1]
    lanes = nq * GROUP * LANES
    resident = dict(pipeline_mode=pl.Buffered(1))
    return pl.pallas_call(
        functools.partial(_slc_kernel, nq=nq),
        grid=(N_KV, nb // nq),
        in_specs=[pl.BlockSpec((1, GROUP * HD, nq * QB), lambda hk, bq: (bq // per, hk, bq % per)),
                  pl.BlockSpec((t, HD), lambda hk, bq: (0, _C_KS + hk), **resident),
                  pl.BlockSpec((t // KT, HD, KT), lambda hk, bq: (0, _R_VS + hk, 0), **resident),
                  pl.BlockSpec((1, nsr, nq * QB), lambda hk, bq: (hk, 0, bq)),
                  pl.BlockSpec((1, N_TOK_BIAS + 1, LANES, GROUP * LANES), lambda hk, bq: (hk, 0, 0, 0), **resident),
                  pl.BlockSpec(ind.shape, lambda hk, bq: (0, 0), **resident),
                  pl.BlockSpec((nq * QB, GROUP * HD), lambda hk, bq: (bq, hk)),
                  pl.BlockSpec((1, 16, nq * QB), lambda hk, bq: (bq // per, _R_G * 8 + hk, bq % per))],
        out_specs=pl.BlockSpec((nq * QB, GROUP * HD), lambda hk, bq: (bq, hk)),
        out_shape=jax.ShapeDtypeStruct((t, N_HEADS * HD), bf16),
        scratch_shapes=[pltpu.VMEM((nsr // SEL_CHUNK, 2 * HD, lanes), bf16),
                        pltpu.VMEM((KT, lanes), f32),
                        pltpu.VMEM((KT, lanes), f32),
                        pltpu.VMEM((KT, lanes), bf16),
                        pltpu.VMEM((KT, lanes), bf16),
                        pltpu.VMEM((HD + 8, lanes), f32),
                        pltpu.VMEM((8, lanes), f32)],
        compiler_params=_params(("parallel", "arbitrary")),
        name="slc_attention",
    )(projt, projn, projt, nsel_t, bias_t, ind, o1, projt)


def _layer_norm(z, g, b):
    mu = jnp.mean(z, axis=-1, keepdims=True)
    var = jnp.mean(jnp.square(z - mu), axis=-1, keepdims=True)
    return (z - mu) * lax.rsqrt(var + LN_EPS) * g + b


def _post_kernel(o2_ref, ya_ref, mb_ref, x_ref, p_ref, wao_ref, wmix_ref, wple_ref, wpg_ref,
                 g_ref, b_ref, x1_ref, r2_ref, *, alpha):
    yb = _dot(o2_ref[...], wao_ref[...])
    mixed = ya_ref[...] + jax.nn.sigmoid(mb_ref[...].astype(f32)) * yb
    z = alpha * x_ref[...] + _dot(mixed.astype(bf16), wmix_ref[...])
    x1 = _layer_norm(z, g_ref[...], b_ref[...])
    x1b = x1.astype(bf16)
    x1_ref[...] = x1b
    ple = _dot(p_ref[...].astype(bf16), wple_ref[...]) * jax.nn.sigmoid(_dot(x1b, wpg_ref[...]))
    r2_ref[...] = alpha * x1 + ple


def _post_attention(o2, ya, mab, x2d, p2d, wao, wmix, wple, wpg, g1, b1, alpha, tm):
    t = x2d.shape[0]
    row = lambda w: pl.BlockSpec((tm, w), lambda i: (i, 0))
    full = lambda a: pl.BlockSpec(a.shape, lambda i: (0, 0), pipeline_mode=pl.Buffered(1))
    return pl.pallas_call(
        functools.partial(_post_kernel, alpha=alpha),
        grid=(t // tm,),
        in_specs=[row(D_MODEL), row(D_MODEL),
                  pl.BlockSpec((tm, D_MODEL), lambda i: (i, 1)),
                  row(D_MODEL), row(PLE_DIM),
                  full(wao), full(wmix), full(wple), full(wpg), full(g1), full(b1)],
        out_specs=[row(D_MODEL), row(D_MODEL)],
        out_shape=[jax.ShapeDtypeStruct((t, D_MODEL), bf16),
                   jax.ShapeDtypeStruct((t, D_MODEL), f32)],
        compiler_params=_params(("parallel",)),
        name="post_attention",
    )(o2, ya, mab, x2d, p2d, wao, wmix, wple, wpg, g1, b1)


def _mlp_kernel(x1_ref, r2_ref, wu_ref, wd_ref, g_ref, b_ref, o_ref):
    k = pl.program_id(1)

    @pl.when(k == 0)
    def _():
        o_ref[...] = r2_ref[...]

    h = jnp.square(jnp.maximum(_dot(x1_ref[...], wu_ref[...]), 0.0))
    o_ref[...] += _dot(h.astype(bf16), wd_ref[...])

    @pl.when(k == pl.num_programs(1) - 1)
    def _():
        o_ref[...] = _layer_norm(o_ref[...], g_ref[...], b_ref[...])


def _mlp(x1b, r2, wu, wd, g2, b2, tm, fc):
    t = x1b.shape[0]
    return pl.pallas_call(
        _mlp_kernel,
        grid=(t // tm, D_FF // fc),
        in_specs=[pl.BlockSpec((tm, D_MODEL), lambda i, k: (i, 0)),
                  pl.BlockSpec((tm, D_MODEL), lambda i, k: (i, 0)),
                  pl.BlockSpec((D_MODEL, fc), lambda i, k: (0, k)),
                  pl.BlockSpec((fc, D_MODEL), lambda i, k: (k, 0)),
                  pl.BlockSpec((1, D_MODEL), lambda i, k: (0, 0)),
                  pl.BlockSpec((1, D_MODEL), lambda i, k: (0, 0))],
        out_specs=pl.BlockSpec((tm, D_MODEL), lambda i, k: (i, 0)),
        out_shape=jax.ShapeDtypeStruct((t, D_MODEL), f32),
        compiler_params=_params(("parallel", "arbitrary")),
        name="mlp",
    )(x1b, r2, wu, wd, g2, b2)


def _layer(x2d, p2d, w_in, conv_w, pe_k, w1_k, w2_k, pe_v, w1_v, w2_v, w_conv_out, w_attn_out,
           w_mix_out, ln1_g, ln1_b, w_mlp_up, w_mlp_down, w_ple, w_ple_gate, ln2_g, ln2_b,
           rel_bias, alpha):
    t = x2d.shape[0]
    assert t % 1024 == 0 and x2d.shape[1] == D_MODEL
    kvw = N_KV * HD
    o_q = 3 * D_MODEL
    o_kc = o_q + N_HEADS * HD
    o_ng = o_kc + 6 * kvw
    o_ma = o_ng + N_HEADS * 3

    n_in = w_in.shape[1]
    cast_rows = max(r for r in range(16, D_MODEL + 1, 16) if n_in % r == 0)
    w_bf = _to_bf16(w_in.T, cast_rows, "cast_w_in")

    def cols(a, n):
        return w_bf[a:a + n]

    tn = kvw
    conv_tiles = 3 * D_MODEL // tn
    ks_tile, kw_tile = (o_kc + 2 * kvw) // tn, (o_kc + 4 * kvw) // tn
    assert _C_KS * LANES == conv_tiles * tn and _C_KW * LANES == (conv_tiles + 1) * tn

    def nat_tile(j):
        return jnp.where(j < conv_tiles, j, jnp.where(j == conv_tiles, ks_tile, kw_tile))

    wng = jnp.pad(cols(o_ng, N_HEADS * 3).reshape(N_KV, GROUP * 3, D_MODEL), ((0, 0), (0, 16 - GROUP * 3), (0, 0)))
    w_t = jnp.concatenate(
        [cols(o_q, N_HEADS * HD), cols(o_kc + 3 * kvw, kvw), cols(o_kc + 5 * kvw, kvw),
         wng.reshape(N_KV * 16, D_MODEL),
         jnp.zeros((_N_T - _R_G * LANES - N_KV * 16, D_MODEL), bf16)], axis=0)
    row_scale = jnp.concatenate([jnp.full((N_HEADS * HD, 1), HD ** -0.5 * LOG2E, f32),
                                 jnp.ones((_N_T - N_HEADS * HD, 1), f32)], axis=0)

    xb = x2d.astype(bf16)
    projn = _proj_nat(xb, w_bf, 2048, tn, "proj_nat", _N_NAT, nat_tile)
    mab = _proj_nat(xb, cols(o_ma, 2 * D_MODEL), 2048, 1024, "proj_gate")
    kvc = _proj_chunked(xb, w_bf, 1024, 2 * kvw, o_kc // (2 * kvw))
    projt = _proj_t(w_t, xb, row_scale, _N_T, KT)

    ya = _mixer_a(projn, mab, conv_w, w_conv_out.astype(bf16), 512)

    half = CMP_STRIDE * HD
    w1ab = jnp.stack([jnp.concatenate([w[:half], w[half:]], axis=1) for w in (w1_k, w1_v)]).astype(bf16)
    pe2 = jnp.stack([jnp.stack([pe[:CMP_STRIDE], pe[CMP_STRIDE:]], axis=1) for pe in (pe_k, pe_v)])
    pe2 = jnp.pad(pe2, ((0, 0), (0, 0), (0, 6), (0, 0)))
    w2 = jnp.stack([w2_k, w2_v]).astype(bf16)
    cn, ct = _compress(kvc, pe2, w1ab, w2, jnp.swapaxes(w2, 1, 2))

    bias_t = _bias_tiles(rel_bias, N_TOK_BIAS, 1, 0, WINDOW // QB, WINDOW - 1, "bias_tok")
    bias_c = _bias_tiles(rel_bias, N_CMP_BIAS, CMP_STRIDE, CMP_LEN - 1, 0, -1, "bias_cmp")

    ncp = t // CMP_STRIDE
    nslc = t // SLC_LEN
    nsr = -(-nslc // SEL_CHUNK) * SEL_CHUNK
    ci = jnp.arange(ncp)[None, :] * CMP_STRIDE
    sj = jnp.arange(nslc)[:, None] * SLC_LEN
    ov_t = ((ci < sj + SLC_LEN) & (ci + CMP_LEN > sj)).astype(bf16)
    pos = jnp.arange(min(t, SEL_CHUNK * SLC_LEN))[:, None] // SLC_LEN
    ind = jnp.where(pos == jnp.arange(SEL_CHUNK)[None, :], SEL_BIG, 0.0).astype(bf16)

    o1, nsel_t = _cmp_attention(projn, projt, cn, ct, bias_c, bias_t, ov_t, t, nsr, CMP_NQ)
    o2 = _slc_attention(projn, projt, nsel_t, bias_t, ind, o1, t, SLC_NQ)

    x1b, r2 = _post_attention(o2, ya, mab, x2d, p2d, w_attn_out.astype(bf16), w_mix_out.astype(bf16),
                              w_ple.astype(bf16), w_ple_gate.astype(bf16),
                              ln1_g[None, :], ln1_b[None, :], alpha, 256)
    return _mlp(x1b, r2, w_mlp_up.astype(bf16), w_mlp_down.astype(bf16), ln2_g[None, :], ln2_b[None, :], 1024, 512)


def kernel(x, p, w_in, conv_w, cmp_pe_k, cmp_w1_k, cmp_w2_k, cmp_pe_v, cmp_w1_v, cmp_w2_v, w_conv_out, w_attn_out, w_mix_out, ln1_g, ln1_b, w_mlp_up, w_mlp_down, w_ple, w_ple_gate, ln2_g, ln2_b, rel_bias):
    bsz, t, d = x.shape
    assert bsz == 1
    depth = w_in.shape[0]
    alpha = (2 * depth) ** 0.25
    x2d = x[0]
    for i in range(depth):
        x2d = _layer(x2d, p[i, 0], w_in[i], conv_w[i], cmp_pe_k[i], cmp_w1_k[i], cmp_w2_k[i],
                     cmp_pe_v[i], cmp_w1_v[i], cmp_w2_v[i], w_conv_out[i], w_attn_out[i], w_mix_out[i],
                     ln1_g[i], ln1_b[i], w_mlp_up[i], w_mlp_down[i], w_ple[i], w_ple_gate[i],
                     ln2_g[i], ln2_b[i], rel_bias, alpha)
    return x2d[None]
```

```python
import functools
import math

import jax
import jax.numpy as jnp
from jax import lax
from jax.experimental import pallas as pl
from jax.experimental.pallas import tpu as pltpu

f32 = jnp.float32
bf16 = jnp.bfloat16
i32 = jnp.int32

D_MODEL = 2048
N_HEADS = 16
N_KV = 4
GROUP = N_HEADS // N_KV
HD = 128
CMP_LEN = 32
CMP_STRIDE = 16
CMP_HIDDEN = 2 * HD
SLC_LEN = 64
SLC_TOPN = 16
WINDOW = 512
QB = 128
D_FF = 4 * D_MODEL
PLE_DIM = 256
REL_BUCKETS = 32
REL_EXACT = REL_BUCKETS // 2
REL_MAX_DIST = 4096
LN_EPS = 1e-5
NEG = -1e30
LOG2E = 1.4426950408889634

LANES = 128
KT = 512
SEL_BIG = 2.0 ** 100
SEL_CHUNK = 128
SLC_NQ = 1
CMP_NQ = 2
N_TOK_BIAS = 25
N_CMP_BIAS = 40
VMEM_LIMIT = 56 * 1024 * 1024

_C_KS, _C_KW = 48, 52
_N_NAT = 56 * LANES
_R_VS, _R_VW, _R_G = 16, 20, 24
_N_T = 25 * LANES


def _dot(a, b):
    return jnp.dot(a, b, preferred_element_type=f32)


def _dot_nt(a, b):
    return lax.dot_general(a, b, (((1,), (1,)), ((), ())), preferred_element_type=f32)


def _params(sem, vmem=VMEM_LIMIT):
    return pltpu.CompilerParams(dimension_semantics=sem, vmem_limit_bytes=vmem)


def _mm_kernel(x_ref, w_ref, o_ref):
    o_ref[...] = _dot_nt(x_ref[...], w_ref[...]).astype(o_ref.dtype)


def _proj_nat(xb, wt, tm, tn, name, n=None, col_tile=lambda j: j):
    m, k = xb.shape
    n = wt.shape[0] if n is None else n
    return pl.pallas_call(
        _mm_kernel,
        grid=(m // tm, n // tn),
        in_specs=[pl.BlockSpec((tm, k), lambda i, j: (i, 0)),
                  pl.BlockSpec((tn, k), lambda i, j: (col_tile(j), 0))],
        out_specs=pl.BlockSpec((tm, tn), lambda i, j: (i, j)),
        out_shape=jax.ShapeDtypeStruct((m, n), bf16),
        compiler_params=_params(("parallel", "arbitrary")),
        name=name,
    )(xb, wt)


def _mmt_kernel(w_ref, x_ref, s_ref, o_ref):
    o_ref[0] = (_dot_nt(w_ref[...], x_ref[...]) * s_ref[...]).astype(o_ref.dtype)


def _proj_t(w, xb, row_scale, tm, tn):
    n, k = w.shape
    t = xb.shape[0]
    return pl.pallas_call(
        _mmt_kernel,
        grid=(n // tm, t // tn),
        in_specs=[pl.BlockSpec((tm, k), lambda i, j: (i, 0)),
                  pl.BlockSpec((tn, k), lambda i, j: (j, 0)),
                  pl.BlockSpec((tm, 1), lambda i, j: (i, 0))],
        out_specs=pl.BlockSpec((1, tm, tn), lambda i, j: (j, i, 0)),
        out_shape=jax.ShapeDtypeStruct((t // tn, n, tn), bf16),
        compiler_params=_params(("parallel", "arbitrary")),
        name="proj_t",
    )(w, xb, row_scale)


def _cast_kernel(x_ref, o_ref):
    o_ref[...] = x_ref[...].astype(o_ref.dtype)


def _to_bf16(w, tm, name):
    r, c = w.shape
    return pl.pallas_call(
        _cast_kernel,
        grid=(r // tm,),
        in_specs=[pl.BlockSpec((tm, c), lambda i: (i, 0))],
        out_specs=pl.BlockSpec((tm, c), lambda i: (i, 0)),
        out_shape=jax.ShapeDtypeStruct((r, c), bf16),
        compiler_params=_params(("parallel",)),
        name=name,
    )(w)


def _ya_kernel(bg_ref, cg_ref, hx_ref, cgh_ref, hxh_ref, cw_ref, w_ref, ma_ref, o_ref, a_ref):
    i = pl.program_id(0)
    tm = bg_ref.shape[0]
    cc = 512
    row = lax.broadcasted_iota(i32, (tm, cc), 0)
    for c in range(D_MODEL // cc):
        sl = slice(c * cc, (c + 1) * cc)
        u = cg_ref[:, sl].astype(f32) * hx_ref[:, sl].astype(f32)
        uh = cgh_ref[:, sl].astype(f32) * hxh_ref[:, sl].astype(f32)
        uh = jnp.where(i > 0, uh, 0.0)
        u1 = jnp.where(row == 0, uh[15:16, :], pltpu.roll(u, 1, 0))
        u2 = jnp.where(row == 0, uh[14:15, :], jnp.where(row == 1, uh[15:16, :], pltpu.roll(u, 2, 0)))
        cw = cw_ref[:, sl]
        conv = cw[0:1, :] * u2 + cw[1:2, :] * u1 + cw[2:3, :] * u
        a_ref[:, sl] = (bg_ref[:, sl].astype(f32) * conv).astype(bf16)
    y = _dot(a_ref[...], w_ref[...])
    o_ref[...] = jax.nn.sigmoid(ma_ref[...].astype(f32)) * y


def _mixer_a(projn, mab, conv_w, w_out, tm):
    t = projn.shape[0]
    hb = tm // 16
    wide = lambda c: pl.BlockSpec((tm, D_MODEL), lambda i, c=c: (i, c))
    halo = lambda c: pl.BlockSpec((16, D_MODEL), lambda i, c=c: (jnp.maximum(i * hb - 1, 0), c))
    return pl.pallas_call(
        _ya_kernel,
        grid=(t // tm,),
        in_specs=[wide(0), wide(1), wide(2), halo(1), halo(2),
                  pl.BlockSpec((3, D_MODEL), lambda i: (0, 0)),
                  pl.BlockSpec((D_MODEL, D_MODEL), lambda i: (0, 0)),
                  wide(0)],
        out_specs=pl.BlockSpec((tm, D_MODEL), lambda i: (i, 0)),
        out_shape=jax.ShapeDtypeStruct((t, D_MODEL), f32),
        scratch_shapes=[pltpu.VMEM((tm, D_MODEL), bf16)],
        compiler_params=_params(("parallel",)),
        name="mixer_a",
    )(projn, projn, projn, projn, projn, conv_w, w_out, mab)


def _compress_kernel(x_ref, w_ref, pe_ref, w2_ref, w2t_ref, on_ref, ot_ref, acc_ref, pet_ref):
    l = pl.program_id(1)
    nl = pl.num_programs(1)
    n = x_ref.shape[0]

    @pl.when(l == 0)
    def _():
        acc_ref[...] = jnp.zeros_like(acc_ref)
        pet_ref[...] = jnp.zeros_like(pet_ref)

    wblk = w_ref[0]
    for h in range(N_KV):
        acc_ref[h] += _dot(x_ref[:, h * HD:(h + 1) * HD], wblk)
    pet_ref[...] += _dot(pe_ref[0, 0].astype(bf16), wblk)

    @pl.when(l == nl - 1)
    def _():
        r = pet_ref[...]
        peterm = r[0:1, :CMP_HIDDEN] + r[1:2, CMP_HIDDEN:]
        for h in range(N_KV):
            ab = acc_ref[h]
            hid = ab[:, :CMP_HIDDEN] + pltpu.roll(ab[:, CMP_HIDDEN:], n - 1, 0) + peterm
            gl = jax.nn.gelu(hid).astype(bf16)
            on_ref[0, h] = _dot(gl, w2_ref[0]).astype(bf16)
            ot_ref[0, h] = _dot_nt(w2t_ref[0], gl).astype(bf16)


def _chunked_mm_kernel(x_ref, w_ref, o_ref, xb_ref, acc_ref):
    xb = x_ref[...].astype(bf16)
    xb_ref[...] = xb
    acc = _dot_nt(xb, w_ref[...])
    n = w_ref.shape[0]
    rows = o_ref.shape[0]
    for c in range(n // LANES):
        acc_ref[c] = acc[:, c * LANES:(c + 1) * LANES]
    for l in range(CMP_STRIDE):
        for c in range(n // LANES):
            o_ref[:, l * n + c * LANES:l * n + (c + 1) * LANES] = (
                acc_ref[c, pl.ds(l, rows, stride=CMP_STRIDE), :].astype(o_ref.dtype))


def _proj_chunked(x, w, tm, n, col_tile):
    m, k = x.shape
    return pl.pallas_call(
        _chunked_mm_kernel,
        grid=(m // tm,),
        in_specs=[pl.BlockSpec((tm, k), lambda i: (i, 0)),
                  pl.BlockSpec((n, k), lambda i: (col_tile, 0))],
        out_specs=[pl.BlockSpec((tm // CMP_STRIDE, CMP_STRIDE * n), lambda i: (i, 0)),
                   pl.BlockSpec((tm, k), lambda i: (i, 0))],
        out_shape=[jax.ShapeDtypeStruct((m // CMP_STRIDE, CMP_STRIDE * n), bf16),
                   jax.ShapeDtypeStruct((m, k), bf16)],
        scratch_shapes=[pltpu.VMEM((n // LANES, tm, LANES), f32)],
        compiler_params=_params(("parallel",)),
        name="proj_cmp",
    )(x, w)


def _compress(xr, pe2, w1ab, w2, w2t):
    n = xr.shape[0]
    return pl.pallas_call(
        _compress_kernel,
        grid=(2, CMP_STRIDE),
        in_specs=[pl.BlockSpec((n, N_KV * HD), lambda kv, l: (0, 2 * l + kv)),
                  pl.BlockSpec((1, HD, 2 * CMP_HIDDEN), lambda kv, l: (kv, l, 0)),
                  pl.BlockSpec((1, 1, 8, HD), lambda kv, l: (kv, l, 0, 0)),
                  pl.BlockSpec((1, CMP_HIDDEN, HD), lambda kv, l: (kv, 0, 0)),
                  pl.BlockSpec((1, HD, CMP_HIDDEN), lambda kv, l: (kv, 0, 0))],
        out_specs=[pl.BlockSpec((1, N_KV, n, HD), lambda kv, l: (kv, 0, 0, 0)),
                   pl.BlockSpec((1, N_KV, HD, n), lambda kv, l: (kv, 0, 0, 0))],
        out_shape=[jax.ShapeDtypeStruct((2, N_KV, n, HD), bf16),
                   jax.ShapeDtypeStruct((2, N_KV, HD, n), bf16)],
        scratch_shapes=[pltpu.VMEM((N_KV, n, 2 * CMP_HIDDEN), f32),
                        pltpu.VMEM((8, 2 * CMP_HIDDEN), f32)],
        compiler_params=_params(("arbitrary", "arbitrary")),
        name="compress",
    )(xr, w1ab, pe2, w2, w2t)


def _rel_bucket(dist):
    n = jnp.maximum(dist, 0)
    nf = jnp.maximum(n, 1).astype(f32)
    large = REL_EXACT + (jnp.log(nf / REL_EXACT) / math.log(REL_MAX_DIST / REL_EXACT)
                         * (REL_BUCKETS - REL_EXACT)).astype(i32)
    large = jnp.minimum(large, REL_BUCKETS - 1)
    return jnp.where(n < REL_EXACT, n, large)


def _bias_kernel(tbl_ref, o_ref, *, kstride, koff, n_reg, edge_o, edge_max):
    hk = pl.program_id(0)
    ki = lax.broadcasted_iota(i32, (LANES, LANES), 0)
    qi = lax.broadcasted_iota(i32, (LANES, LANES), 1)
    rows = [jnp.broadcast_to(tbl_ref[pl.ds(hk * GROUP + g, 1), :], (LANES, LANES)) for g in range(GROUP)]

    def tile(o, carry):
        is_edge = o == n_reg
        dist = jnp.where(is_edge, edge_o, o) * LANES + qi - kstride * ki - koff
        valid = (dist >= 0) & (dist <= jnp.where(is_edge, edge_max, 2 ** 30))
        bucket = _rel_bucket(dist)
        for g in range(GROUP):
            acc = jnp.take_along_axis(rows[g], bucket, axis=1)
            o_ref[0, o, :, g * LANES:(g + 1) * LANES] = jnp.where(valid, acc * LOG2E, NEG)
        return carry

    lax.fori_loop(0, n_reg + 1, tile, 0)


def _bias_tiles(rel_bias, n_reg, kstride, koff, edge_o, edge_max, name):
    return pl.pallas_call(
        functools.partial(_bias_kernel, kstride=kstride, koff=koff, n_reg=n_reg, edge_o=edge_o,
                          edge_max=edge_max),
        grid=(N_KV,),
        in_specs=[pl.BlockSpec((N_HEADS, LANES), lambda hk: (0, 0))],
        out_specs=pl.BlockSpec((1, n_reg + 1, LANES, GROUP * LANES), lambda hk: (hk, 0, 0, 0)),
        out_shape=jax.ShapeDtypeStruct((N_KV, n_reg + 1, LANES, GROUP * LANES), f32),
        compiler_params=_params(("parallel",)),
        name=name,
    )(jnp.pad(rel_bias.astype(f32).T, ((0, 0), (0, LANES - REL_BUCKETS))))


def _query_lanes(q_ref, nq):
    qb = q_ref[0]
    return jnp.concatenate([qb[g * HD:(g + 1) * HD, h * QB:(h + 1) * QB]
                            for h in range(nq) for g in range(GROUP)], axis=1)


def _cmp_kernel(q_ref, kc_ref, vct_ref, bc_ref, ov_ref, g_ref, *rest, n_sel, n_steps, n_branch, nq):
    nw = nq + WINDOW // QB
    kws, vws = rest[:nw], rest[nw:2 * nw]
    bt_ref, o_ref, ns_ref, sc_ref = rest[2 * nw:]
    step_id = pl.program_id(1)
    b0 = step_id * nq
    nsr = ns_ref.shape[1]
    gl = GROUP * LANES
    qt = _query_lanes(q_ref, nq)
    gate = jax.nn.sigmoid(g_ref[0].astype(f32))
    cmp_per_q = QB // CMP_STRIDE
    q_per_tile = LANES // cmp_per_q

    def core(nt, rows):
        keys = nt * LANES
        s = _dot(kc_ref[0, 0, 0:keys, :], qt)
        m = None
        for h in range(nq):
            mh = None
            for c in range(nt):
                o = b0 + h - q_per_tile * c
                idx = jnp.where(o < 0, N_CMP_BIAS, jnp.minimum(o, N_CMP_BIAS - 1))
                sc = s[c * LANES:(c + 1) * LANES, h * gl:(h + 1) * gl] + bc_ref[0, idx]
                sc_ref[c * LANES:(c + 1) * LANES, h * gl:(h + 1) * gl] = sc
                mc = jnp.max(sc, axis=0, keepdims=True)
                mh = mc if mh is None else jnp.maximum(mh, mc)
            m = mh if m is None else jnp.concatenate([m, mh], axis=1)
        e = jnp.exp2(sc_ref[0:keys, :] - m).astype(bf16)
        lhs = jnp.concatenate([vct_ref[0, 0, :, 0:keys], jnp.ones((16, keys), bf16), ov_ref[0:rows, 0:keys]], axis=0)
        acc = _dot(lhs, e)
        inv = jnp.where(m > 0.5 * NEG, 1.0 / acc[HD:HD + 1], 0.0)
        oct_ = acc[0:HD] * inv
        imp_g = acc[HD + 16:] * inv

        sw = _dot(jnp.concatenate([r[...] for r in kws], axis=0), qt)
        col_parts = []
        for h in range(nq):
            parts = []
            for jw in range(nw):
                dl = h + WINDOW // QB - jw
                if dl < 0 or dl > WINDOW // QB:
                    parts.append(jnp.full((LANES, gl), NEG, f32))
                    continue
                tile = N_TOK_BIAS if dl == WINDOW // QB else dl
                before_start = jnp.where(b0 + h - dl >= 0, 0.0, NEG)
                parts.append(sw[jw * LANES:(jw + 1) * LANES, h * gl:(h + 1) * gl] + bt_ref[0, tile] + before_start)
            col_parts.append(jnp.concatenate(parts, axis=0))
        sw = jnp.concatenate(col_parts, axis=1) if nq > 1 else col_parts[0]
        pw = jnp.exp2(sw - jnp.max(sw, axis=0, keepdims=True)).astype(bf16)
        vw = jnp.concatenate([r[0] for r in vws], axis=1)
        accw = _dot(jnp.concatenate([vw, jnp.ones((8, nw * LANES), bf16)], axis=0), pw)
        owt = accw[0:HD] / accw[HD:HD + 1]
        for h in range(nq):
            qs = slice(h * QB, (h + 1) * QB)
            for g in range(GROUP):
                ls = slice(h * gl + g * LANES, h * gl + (g + 1) * LANES)
                both = oct_[:, ls] * gate[g * 3:g * 3 + 1, qs] + owt[:, ls] * gate[g * 3 + 2:g * 3 + 3, qs]
                o_ref[qs, g * HD:(g + 1) * HD] = both.T.astype(o_ref.dtype)

        imps = []
        for h in range(nq):
            ig = None
            for g in range(GROUP):
                part = imp_g[:, h * gl + g * LANES:h * gl + (g + 1) * LANES]
                ig = part if ig is None else ig + part
            imps.append(ig)
        imp = jnp.concatenate(imps, axis=1) if nq > 1 else imps[0]

        j = lax.broadcasted_iota(i32, (rows, nq * QB), 0)
        tq = b0 * QB + lax.broadcasted_iota(i32, (rows, nq * QB), 1)
        jt = jnp.right_shift(tq, SLC_LEN.bit_length() - 1)
        forced = (j == 0) | (j == jt) | (j == jt - 1)
        future = j * SLC_LEN > tq
        taken = -3e38
        work = jnp.where(forced, taken, jnp.where(future, -1.0, imp))
        jf = j.astype(f32)
        for _ in range(n_sel - 3):
            mx = jnp.max(work, axis=0, keepdims=True)
            jm = jnp.min(jnp.where(work == mx, jf, 1e9), axis=0, keepdims=True)
            work = jnp.where(jf == jm, taken, work)
        ns_ref[0, 0:rows, :] = jnp.where(future | (work != taken), 1.0, 0.0).astype(bf16)
        if nsr > rows:
            ns_ref[0, rows:, :] = jnp.ones((nsr - rows, nq * QB), bf16)

    per_branch = n_steps // n_branch
    for k in range(n_branch):
        last_b = (k + 1) * per_branch * nq - 1
        nt = last_b // q_per_tile + 1
        rows = (last_b + 1) * (QB // SLC_LEN)
        pl.when(step_id // per_branch == k)(functools.partial(core, nt, rows))


def _cmp_attention(projn, projt, cn, ct, bias_c, bias_t, ov_t, t, nsr, nq):
    nb = t // QB
    ncp = t // CMP_STRIDE
    nslc = t // SLC_LEN
    per_kt = KT // QB
    per = per_kt // nq
    nw = nq + WINDOW // QB
    n_sel = min(SLC_TOPN, nslc)
    assert n_sel > 3
    n_steps = nb // nq
    n_branch = 8 if nb % 128 == 0 else (4 if nb % 64 == 0 else 1)
    resident = dict(pipeline_mode=pl.Buffered(1))

    def kw_spec(jw):
        return pl.BlockSpec((QB, HD), lambda hk, s, jw=jw: (jnp.maximum(s * nq - WINDOW // QB + jw, 0), _C_KW + hk))

    def vw_spec(jw):
        def imap(hk, s, jw=jw):
            bb = jnp.maximum(s * nq - WINDOW // QB + jw, 0)
            return (bb // per_kt, _R_VW + hk, bb % per_kt)
        return pl.BlockSpec((1, HD, QB), imap)

    return pl.pallas_call(
        functools.partial(_cmp_kernel, n_sel=n_sel, n_steps=n_steps, n_branch=n_branch, nq=nq),
        grid=(N_KV, n_steps),
        in_specs=[pl.BlockSpec((1, GROUP * HD, nq * QB), lambda hk, s: (s // per, hk, s % per)),
                  pl.BlockSpec((1, 1, ncp, HD), lambda hk, s: (0, hk, 0, 0), **resident),
                  pl.BlockSpec((1, 1, HD, ncp), lambda hk, s: (1, hk, 0, 0), **resident),
                  pl.BlockSpec((1, N_CMP_BIAS + 1, LANES, GROUP * LANES), lambda hk, s: (hk, 0, 0, 0), **resident),
                  pl.BlockSpec((nslc, ncp), lambda hk, s: (0, 0), **resident),
                  pl.BlockSpec((1, 16, nq * QB), lambda hk, s: (s // per, _R_G * 8 + hk, s % per))]
                 + [kw_spec(jw) for jw in range(nw)]
                 + [vw_spec(jw) for jw in range(nw)]
                 + [pl.BlockSpec((1, N_TOK_BIAS + 1, LANES, GROUP * LANES), lambda hk, s: (hk, 0, 0, 0), **resident)],
        out_specs=[pl.BlockSpec((nq * QB, GROUP * HD), lambda hk, s: (s, hk)),
                   pl.BlockSpec((1, nsr, nq * QB), lambda hk, s: (hk, 0, s))],
        out_shape=[jax.ShapeDtypeStruct((t, N_HEADS * HD), f32),
                   jax.ShapeDtypeStruct((N_KV, nsr, t), bf16)],
        scratch_shapes=[pltpu.VMEM((ncp, nq * GROUP * LANES), f32)],
        compiler_params=_params(("parallel", "arbitrary")),
        name="cmp_attention",
    )(projt, cn, ct, bias_c, ov_t, projt, *([projn] * nw), *([projt] * nw), bias_t)


def _slc_kernel(q_ref, ks_ref, vst_ref, ns_ref, bt_ref, ind_ref, o1_ref, g_ref, o_ref,
                qa_ref, s0_ref, s1_ref, p0_ref, p1_ref, acc_ref, st_ref, *, nq):
    b0 = pl.program_id(1) * nq
    gl = GROUP * LANES
    m_ref, al_ref, mx_ref, off_ref = (st_ref.at[pl.ds(r, 1), :] for r in range(4))
    qt = _query_lanes(q_ref, nq)
    gate = jax.nn.sigmoid(g_ref[0].astype(f32))

    n_chunks = qa_ref.shape[0]
    for jc in range(n_chunks):
        nsj = -ns_ref[0, jc * SEL_CHUNK:(jc + 1) * SEL_CHUNK, :]
        qa_ref[jc, 0:HD, :] = qt
        qa_ref[jc, HD:, :] = jnp.concatenate([nsj[:, h * QB:(h + 1) * QB]
                                              for h in range(nq) for _ in range(GROUP)], axis=1)

    n_tiles = vst_ref.shape[0]
    per_chunk = SEL_CHUNK * SLC_LEN // KT
    sub = KT // LANES

    far_bias = jnp.concatenate([bt_ref[0, N_TOK_BIAS - 1, 0:1, :]] * nq, axis=1)
    ones_rows = jnp.ones((8, KT), bf16)

    def scores(kt, s_ref, far):
        jc = kt // per_chunk
        r0 = pl.multiple_of(kt * KT, KT)
        i0 = pl.multiple_of((kt - jc * per_chunk) * KT, KT)
        kaug = jnp.concatenate([ks_ref[pl.ds(r0, KT), :], ind_ref[pl.ds(i0, KT), :]], axis=1)
        s = _dot(kaug, qa_ref[jc])
        if far:
            s_ref[...] = s
            mx_ref[...] = jnp.max(s, axis=0, keepdims=True) + far_bias
            off_ref[...] = far_bias
            return
        mxs = []
        for h in range(nq):
            cols = slice(h * gl, (h + 1) * gl)
            mx = None
            for jj in range(sub):
                rows = slice(jj * LANES, (jj + 1) * LANES)
                dl = b0 + h - (kt * sub + jj)
                sj = s[rows, cols] + bt_ref[0, jnp.clip(dl, 0, N_TOK_BIAS - 1)]
                s_ref[rows, cols] = sj
                mj = jnp.max(sj, axis=0, keepdims=True)
                mx = mj if mx is None else jnp.maximum(mx, mj)
            mxs.append(mx)
        mx_ref[...] = jnp.concatenate(mxs, axis=1) if nq > 1 else mxs[0]
        off_ref[...] = jnp.zeros_like(off_ref)

    def values(kt):
        return jnp.concatenate([vst_ref[kt], ones_rows], axis=0)

    def step(kt, s_cur, p_cur, s_nxt, p_prev, far):
        m_old = m_ref[...]
        m_new = jnp.maximum(m_old, mx_ref[...])
        shift = m_new - off_ref[...]
        al_old = al_ref[...]
        scores(jnp.minimum(kt + 1, n_tiles - 1), s_nxt, far)
        acc_ref[...] = al_old * acc_ref[...] + _dot(values(jnp.maximum(kt - 1, 0)), p_prev[...])
        p_cur[...] = jnp.exp2(s_cur[...] - shift).astype(bf16)
        m_ref[...] = m_new
        al_ref[...] = jnp.exp2(m_old - m_new)

    acc_ref[...] = jnp.zeros_like(acc_ref)
    m_ref[...] = jnp.full_like(m_ref, NEG)
    al_ref[...] = jnp.ones_like(al_ref)
    p1_ref[...] = jnp.zeros_like(p1_ref)
    scores(0, s0_ref, False)

    n_pairs = (b0 + nq - 1) // (2 * KT // QB) + 1
    n_far = jnp.maximum((b0 - (N_TOK_BIAS - 2)) // sub, 0)
    n_far_pairs = jnp.maximum((n_far - 1) // 2, 0)

    def pair(far):
        def body(i, carry):
            step(2 * i, s0_ref, p0_ref, s1_ref, p1_ref, far)
            step(2 * i + 1, s1_ref, p1_ref, s0_ref, p0_ref, far)
            return carry
        return body

    lax.fori_loop(0, n_far_pairs, pair(True), 0)
    lax.fori_loop(n_far_pairs, n_pairs, pair(False), 0)
    acc = al_ref[...] * acc_ref[...] + _dot(values(2 * n_pairs - 1), p1_ref[...])

    ost = acc[0:HD] / acc[HD:HD + 1]
    for h in range(nq):
        qs = slice(h * QB, (h + 1) * QB)
        for g in range(GROUP):
            ls = slice(h * gl + g * LANES, h * gl + (g + 1) * LANES)
            ds = slice(g * HD, (g + 1) * HD)
            o_ref[qs, ds] = ((ost[:, ls] * gate[g * 3 + 1:g * 3 + 2, qs]).T + o1_ref[qs, ds]).astype(o_ref.dtype)


def _slc_attention(projn, projt, nsel_t, bias_t, ind, o1, t, nq):
    nb = t // QB
    per = KT // QB // nq
    nsr = nsel_t.shape[1]
    lanes = nq * GROUP * LANES
    resident = dict(pipeline_mode=pl.Buffered(1))
    return pl.pallas_call(
        functools.partial(_slc_kernel, nq=nq),
        grid=(N_KV, nb // nq),
        in_specs=[pl.BlockSpec((1, GROUP * HD, nq * QB), lambda hk, bq: (bq // per, hk, bq % per)),
                  pl.BlockSpec((t, HD), lambda hk, bq: (0, _C_KS + hk), **resident),
                  pl.BlockSpec((t // KT, HD, KT), lambda hk, bq: (0, _R_VS + hk, 0), **resident),
                  pl.BlockSpec((1, nsr, nq * QB), lambda hk, bq: (hk, 0, bq)),
                  pl.BlockSpec((1, N_TOK_BIAS + 1, LANES, GROUP * LANES), lambda hk, bq: (hk, 0, 0, 0), **resident),
                  pl.BlockSpec(ind.shape, lambda hk, bq: (0, 0), **resident),
                  pl.BlockSpec((nq * QB, GROUP * HD), lambda hk, bq: (bq, hk)),
                  pl.BlockSpec((1, 16, nq * QB), lambda hk, bq: (bq // per, _R_G * 8 + hk, bq % per))],
        out_specs=pl.BlockSpec((nq * QB, GROUP * HD), lambda hk, bq: (bq, hk)),
        out_shape=jax.ShapeDtypeStruct((t, N_HEADS * HD), bf16),
        scratch_shapes=[pltpu.VMEM((nsr // SEL_CHUNK, 2 * HD, lanes), bf16),
                        pltpu.VMEM((KT, lanes), f32),
                        pltpu.VMEM((KT, lanes), f32),
                        pltpu.VMEM((KT, lanes), bf16),
                        pltpu.VMEM((KT, lanes), bf16),
                        pltpu.VMEM((HD + 8, lanes), f32),
                        pltpu.VMEM((8, lanes), f32)],
        compiler_params=_params(("parallel", "arbitrary")),
        name="slc_attention",
    )(projt, projn, projt, nsel_t, bias_t, ind, o1, projt)


def _layer_norm(z, g, b):
    mu = jnp.mean(z, axis=-1, keepdims=True)
    var = jnp.mean(jnp.square(z - mu), axis=-1, keepdims=True)
    return (z - mu) * lax.rsqrt(var + LN_EPS) * g + b


def _post_kernel(o2_ref, ya_ref, mb_ref, x_ref, p_ref, wao_ref, wmix_ref, wple_ref, wpg_ref,
                 g_ref, b_ref, x1_ref, r2_ref, *, alpha):
    yb = _dot(o2_ref[...], wao_ref[...])
    mixed = ya_ref[...] + jax.nn.sigmoid(mb_ref[...].astype(f32)) * yb
    z = alpha * x_ref[...] + _dot(mixed.astype(bf16), wmix_ref[...])
    x1 = _layer_norm(z, g_ref[...], b_ref[...])
    x1b = x1.astype(bf16)
    x1_ref[...] = x1b
    ple = _dot(p_ref[...].astype(bf16), wple_ref[...]) * jax.nn.sigmoid(_dot(x1b, wpg_ref[...]))
    r2_ref[...] = alpha * x1 + ple


def _post_attention(o2, ya, mab, x2d, p2d, wao, wmix, wple, wpg, g1, b1, alpha, tm):
    t = x2d.shape[0]
    row = lambda w: pl.BlockSpec((tm, w), lambda i: (i, 0))
    full = lambda a: pl.BlockSpec(a.shape, lambda i: (0, 0), pipeline_mode=pl.Buffered(1))
    return pl.pallas_call(
        functools.partial(_post_kernel, alpha=alpha),
        grid=(t // tm,),
        in_specs=[row(D_MODEL), row(D_MODEL),
                  pl.BlockSpec((tm, D_MODEL), lambda i: (i, 1)),
                  row(D_MODEL), row(PLE_DIM),
                  full(wao), full(wmix), full(wple), full(wpg), full(g1), full(b1)],
        out_specs=[row(D_MODEL), row(D_MODEL)],
        out_shape=[jax.ShapeDtypeStruct((t, D_MODEL), bf16),
                   jax.ShapeDtypeStruct((t, D_MODEL), f32)],
        compiler_params=_params(("parallel",)),
        name="post_attention",
    )(o2, ya, mab, x2d, p2d, wao, wmix, wple, wpg, g1, b1)


def _mlp_kernel(x1_ref, r2_ref, wu_ref, wd_ref, g_ref, b_ref, o_ref):
    k = pl.program_id(1)

    @pl.when(k == 0)
    def _():
        o_ref[...] = r2_ref[...]

    h = jnp.square(jnp.maximum(_dot(x1_ref[...], wu_ref[...]), 0.0))
    o_ref[...] += _dot(h.astype(bf16), wd_ref[...])

    @pl.when(k == pl.num_programs(1) - 1)
    def _():
        o_ref[...] = _layer_norm(o_ref[...], g_ref[...], b_ref[...])


def _mlp(x1b, r2, wu, wd, g2, b2, tm, fc):
    t = x1b.shape[0]
    return pl.pallas_call(
        _mlp_kernel,
        grid=(t // tm, D_FF // fc),
        in_specs=[pl.BlockSpec((tm, D_MODEL), lambda i, k: (i, 0)),
                  pl.BlockSpec((tm, D_MODEL), lambda i, k: (i, 0)),
                  pl.BlockSpec((D_MODEL, fc), lambda i, k: (0, k)),
                  pl.BlockSpec((fc, D_MODEL), lambda i, k: (k, 0)),
                  pl.BlockSpec((1, D_MODEL), lambda i, k: (0, 0)),
                  pl.BlockSpec((1, D_MODEL), lambda i, k: (0, 0))],
        out_specs=pl.BlockSpec((tm, D_MODEL), lambda i, k: (i, 0)),
        out_shape=jax.ShapeDtypeStruct((t, D_MODEL), f32),
        compiler_params=_params(("parallel", "arbitrary")),
        name="mlp",
    )(x1b, r2, wu, wd, g2, b2)


def _layer(x2d, p2d, w_in, conv_w, pe_k, w1_k, w2_k, pe_v, w1_v, w2_v, w_conv_out, w_attn_out,
           w_mix_out, ln1_g, ln1_b, w_mlp_up, w_mlp_down, w_ple, w_ple_gate, ln2_g, ln2_b,
           rel_bias, alpha):
    t = x2d.shape[0]
    assert t % 1024 == 0 and x2d.shape[1] == D_MODEL
    kvw = N_KV * HD
    o_q = 3 * D_MODEL
    o_kc = o_q + N_HEADS * HD
    o_ng = o_kc + 6 * kvw
    o_ma = o_ng + N_HEADS * 3

    n_in = w_in.shape[1]
    cast_rows = max(r for r in range(16, D_MODEL + 1, 16) if n_in % r == 0)
    w_bf = _to_bf16(w_in.T, cast_rows, "cast_w_in")

    def cols(a, n):
        return w_bf[a:a + n]

    tn = kvw
    conv_tiles = 3 * D_MODEL // tn
    ks_tile, kw_tile = (o_kc + 2 * kvw) // tn, (o_kc + 4 * kvw) // tn
    assert _C_KS * LANES == conv_tiles * tn and _C_KW * LANES == (conv_tiles + 1) * tn

    def nat_tile(j):
        return jnp.where(j < conv_tiles, j, jnp.where(j == conv_tiles, ks_tile, kw_tile))

    wng = jnp.pad(cols(o_ng, N_HEADS * 3).reshape(N_KV, GROUP * 3, D_MODEL), ((0, 0), (0, 16 - GROUP * 3), (0, 0)))
    w_t = jnp.concatenate(
        [cols(o_q, N_HEADS * HD), cols(o_kc + 3 * kvw, kvw), cols(o_kc + 5 * kvw, kvw),
         wng.reshape(N_KV * 16, D_MODEL),
         jnp.zeros((_N_T - _R_G * LANES - N_KV * 16, D_MODEL), bf16)], axis=0)
    row_scale = jnp.concatenate([jnp.full((N_HEADS * HD, 1), HD ** -0.5 * LOG2E, f32),
                                 jnp.ones((_N_T - N_HEADS * HD, 1), f32)], axis=0)

    kvc, xb = _proj_chunked(x2d, w_bf, 1024, 2 * kvw, o_kc // (2 * kvw))
    projn = _proj_nat(xb, w_bf, 2048, tn, "proj_nat", _N_NAT, nat_tile)
    mab = _proj_nat(xb, cols(o_ma, 2 * D_MODEL), 2048, 1024, "proj_gate")
    projt = _proj_t(w_t, xb, row_scale, _N_T, KT)

    ya = _mixer_a(projn, mab, conv_w, w_conv_out.astype(bf16), 512)

    half = CMP_STRIDE * HD
    w1ab = jnp.stack([jnp.concatenate([w[:half], w[half:]], axis=1) for w in (w1_k, w1_v)]).astype(bf16)
    pe2 = jnp.stack([jnp.stack([pe[:CMP_STRIDE], pe[CMP_STRIDE:]], axis=1) for pe in (pe_k, pe_v)])
    pe2 = jnp.pad(pe2, ((0, 0), (0, 0), (0, 6), (0, 0)))
    w2 = jnp.stack([w2_k, w2_v]).astype(bf16)
    cn, ct = _compress(kvc, pe2, w1ab, w2, jnp.swapaxes(w2, 1, 2))

    bias_t = _bias_tiles(rel_bias, N_TOK_BIAS, 1, 0, WINDOW // QB, WINDOW - 1, "bias_tok")
    bias_c = _bias_tiles(rel_bias, N_CMP_BIAS, CMP_STRIDE, CMP_LEN - 1, 0, -1, "bias_cmp")

    ncp = t // CMP_STRIDE
    nslc = t // SLC_LEN
    nsr = -(-nslc // SEL_CHUNK) * SEL_CHUNK
    ci = jnp.arange(ncp)[None, :] * CMP_STRIDE
    sj = jnp.arange(nslc)[:, None] * SLC_LEN
    ov_t = ((ci < sj + SLC_LEN) & (ci + CMP_LEN > sj)).astype(bf16)
    pos = jnp.arange(min(t, SEL_CHUNK * SLC_LEN))[:, None] // SLC_LEN
    ind = jnp.where(pos == jnp.arange(SEL_CHUNK)[None, :], SEL_BIG, 0.0).astype(bf16)

    o1, nsel_t = _cmp_attention(projn, projt, cn, ct, bias_c, bias_t, ov_t, t, nsr, CMP_NQ)
    o2 = _slc_attention(projn, projt, nsel_t, bias_t, ind, o1, t, SLC_NQ)

    x1b, r2 = _post_attention(o2, ya, mab, x2d, p2d, w_attn_out.astype(bf16), w_mix_out.astype(bf16),
                              w_ple.astype(bf16), w_ple_gate.astype(bf16),
                              ln1_g[None, :], ln1_b[None, :], alpha, 256)
    return _mlp(x1b, r2, w_mlp_up.astype(bf16), w_mlp_down.astype(bf16), ln2_g[None, :], ln2_b[None, :], 1024, 512)


def kernel(x, p, w_in, conv_w, cmp_pe_k, cmp_w1_k, cmp_w2_k, cmp_pe_v, cmp_w1_v, cmp_w2_v, w_conv_out, w_attn_out, w_mix_out, ln1_g, ln1_b, w_mlp_up, w_mlp_down, w_ple, w_ple_gate, ln2_g, ln2_b, rel_bias):
    bsz, t, d = x.shape
    assert bsz == 1
    depth = w_in.shape[0]
    alpha = (2 * depth) ** 0.25
    x2d = x[0]
    for i in range(depth):
        x2d = _layer(x2d, p[i, 0], w_in[i], conv_w[i], cmp_pe_k[i], cmp_w1_k[i], cmp_w2_k[i],
                     cmp_pe_v[i], cmp_w1_v[i], cmp_w2_v[i], w_conv_out[i], w_attn_out[i], w_mix_out[i],
                     ln1_g[i], ln1_b[i], w_mlp_up[i], w_mlp_down[i], w_ple[i], w_ple_gate[i],
                     ln2_g[i], ln2_b[i], rel_bias, alpha)
    return x2d[None]
```

```python
import functools
import math

import jax
import jax.numpy as jnp
from jax import lax
from jax.experimental import pallas as pl
from jax.experimental.pallas import tpu as pltpu

f32 = jnp.float32
bf16 = jnp.bfloat16
i32 = jnp.int32

D_MODEL = 2048
N_HEADS = 16
N_KV = 4
GROUP = N_HEADS // N_KV
HD = 128
CMP_LEN = 32
CMP_STRIDE = 16
CMP_HIDDEN = 2 * HD
SLC_LEN = 64
SLC_TOPN = 16
WINDOW = 512
QB = 128
D_FF = 4 * D_MODEL
PLE_DIM = 256
REL_BUCKETS = 32
REL_EXACT = REL_BUCKETS // 2
REL_MAX_DIST = 4096
LN_EPS = 1e-5
NEG = -1e30
LOG2E = 1.4426950408889634

LANES = 128
KT = 512
SEL_BIG = 2.0 ** 100
SEL_CHUNK = 128
SLC_NQ = 1
CMP_NQ = 2
N_TOK_BIAS = 25
N_CMP_BIAS = 40
VMEM_LIMIT = 56 * 1024 * 1024

_C_KS, _C_KW = 48, 52
_N_NAT = 56 * LANES
_R_VS, _R_VW, _R_G = 16, 20, 24
_N_T = 25 * LANES


def _dot(a, b):
    return jnp.dot(a, b, preferred_element_type=f32)


def _dot_nt(a, b):
    return lax.dot_general(a, b, (((1,), (1,)), ((), ())), preferred_element_type=f32)


def _params(sem, vmem=VMEM_LIMIT):
    return pltpu.CompilerParams(dimension_semantics=sem, vmem_limit_bytes=vmem)


def _mm_kernel(x_ref, w_ref, o_ref):
    o_ref[...] = _dot_nt(x_ref[...], w_ref[...]).astype(o_ref.dtype)


def _proj_nat(xb, wt, tm, tn, name, n=None, col_tile=lambda j: j):
    m, k = xb.shape
    n = wt.shape[0] if n is None else n
    return pl.pallas_call(
        _mm_kernel,
        grid=(m // tm, n // tn),
        in_specs=[pl.BlockSpec((tm, k), lambda i, j: (i, 0)),
                  pl.BlockSpec((tn, k), lambda i, j: (col_tile(j), 0))],
        out_specs=pl.BlockSpec((tm, tn), lambda i, j: (i, j)),
        out_shape=jax.ShapeDtypeStruct((m, n), bf16),
        compiler_params=_params(("parallel", "arbitrary")),
        name=name,
    )(xb, wt)


def _mmt_kernel(w_ref, x_ref, s_ref, o_ref):
    o_ref[0] = (_dot_nt(w_ref[...], x_ref[...]) * s_ref[...]).astype(o_ref.dtype)


def _proj_t(w, xb, row_scale, tm, tn):
    n, k = w.shape
    t = xb.shape[0]
    return pl.pallas_call(
        _mmt_kernel,
        grid=(n // tm, t // tn),
        in_specs=[pl.BlockSpec((tm, k), lambda i, j: (i, 0)),
                  pl.BlockSpec((tn, k), lambda i, j: (j, 0)),
                  pl.BlockSpec((tm, 1), lambda i, j: (i, 0))],
        out_specs=pl.BlockSpec((1, tm, tn), lambda i, j: (j, i, 0)),
        out_shape=jax.ShapeDtypeStruct((t // tn, n, tn), bf16),
        compiler_params=_params(("parallel", "arbitrary")),
        name="proj_t",
    )(w, xb, row_scale)


def _cast_kernel(x_ref, o_ref):
    o_ref[...] = x_ref[...].astype(o_ref.dtype)


def _to_bf16(w, tm, name):
    r, c = w.shape
    return pl.pallas_call(
        _cast_kernel,
        grid=(r // tm,),
        in_specs=[pl.BlockSpec((tm, c), lambda i: (i, 0))],
        out_specs=pl.BlockSpec((tm, c), lambda i: (i, 0)),
        out_shape=jax.ShapeDtypeStruct((r, c), bf16),
        compiler_params=_params(("parallel",)),
        name=name,
    )(w)


def _ya_kernel(bg_ref, cg_ref, hx_ref, cgh_ref, hxh_ref, cw_ref, w_ref, ma_ref, o_ref, a_ref):
    i = pl.program_id(0)
    tm = bg_ref.shape[0]
    cc = 512
    row = lax.broadcasted_iota(i32, (tm, cc), 0)
    for c in range(D_MODEL // cc):
        sl = slice(c * cc, (c + 1) * cc)
        u = cg_ref[:, sl].astype(f32) * hx_ref[:, sl].astype(f32)
        uh = cgh_ref[:, sl].astype(f32) * hxh_ref[:, sl].astype(f32)
        uh = jnp.where(i > 0, uh, 0.0)
        u1 = jnp.where(row == 0, uh[15:16, :], pltpu.roll(u, 1, 0))
        u2 = jnp.where(row == 0, uh[14:15, :], jnp.where(row == 1, uh[15:16, :], pltpu.roll(u, 2, 0)))
        cw = cw_ref[:, sl]
        conv = cw[0:1, :] * u2 + cw[1:2, :] * u1 + cw[2:3, :] * u
        a_ref[:, sl] = (bg_ref[:, sl].astype(f32) * conv).astype(bf16)
    y = _dot(a_ref[...], w_ref[...])
    o_ref[...] = jax.nn.sigmoid(ma_ref[...].astype(f32)) * y


def _mixer_a(projn, mab, conv_w, w_out, tm):
    t = projn.shape[0]
    hb = tm // 16
    wide = lambda c: pl.BlockSpec((tm, D_MODEL), lambda i, c=c: (i, c))
    halo = lambda c: pl.BlockSpec((16, D_MODEL), lambda i, c=c: (jnp.maximum(i * hb - 1, 0), c))
    return pl.pallas_call(
        _ya_kernel,
        grid=(t // tm,),
        in_specs=[wide(0), wide(1), wide(2), halo(1), halo(2),
                  pl.BlockSpec((3, D_MODEL), lambda i: (0, 0)),
                  pl.BlockSpec((D_MODEL, D_MODEL), lambda i: (0, 0)),
                  wide(0)],
        out_specs=pl.BlockSpec((tm, D_MODEL), lambda i: (i, 0)),
        out_shape=jax.ShapeDtypeStruct((t, D_MODEL), f32),
        scratch_shapes=[pltpu.VMEM((tm, D_MODEL), bf16)],
        compiler_params=_params(("parallel",)),
        name="mixer_a",
    )(projn, projn, projn, projn, projn, conv_w, w_out, mab)


def _compress_kernel(x_ref, w_ref, pe_ref, w2_ref, w2t_ref, on_ref, ot_ref, acc_ref, pet_ref):
    l = pl.program_id(1)
    nl = pl.num_programs(1)
    n = x_ref.shape[0]

    @pl.when(l == 0)
    def _():
        acc_ref[...] = jnp.zeros_like(acc_ref)
        pet_ref[...] = jnp.zeros_like(pet_ref)

    wblk = w_ref[0]
    for h in range(N_KV):
        acc_ref[h] += _dot(x_ref[:, h * HD:(h + 1) * HD], wblk)
    pet_ref[...] += _dot(pe_ref[0, 0].astype(bf16), wblk)

    @pl.when(l == nl - 1)
    def _():
        r = pet_ref[...]
        peterm = r[0:1, :CMP_HIDDEN] + r[1:2, CMP_HIDDEN:]
        for h in range(N_KV):
            ab = acc_ref[h]
            hid = ab[:, :CMP_HIDDEN] + pltpu.roll(ab[:, CMP_HIDDEN:], n - 1, 0) + peterm
            gl = jax.nn.gelu(hid).astype(bf16)
            on_ref[0, h] = _dot(gl, w2_ref[0]).astype(bf16)
            ot_ref[0, h] = _dot_nt(w2t_ref[0], gl).astype(bf16)


def _chunked_mm_kernel(x_ref, w_ref, o_ref, xb_ref, acc_ref):
    xb = x_ref[...].astype(bf16)
    xb_ref[...] = xb
    acc = _dot_nt(xb, w_ref[...])
    n = w_ref.shape[0]
    rows = o_ref.shape[0]
    for c in range(n // LANES):
        acc_ref[c] = acc[:, c * LANES:(c + 1) * LANES]
    for l in range(CMP_STRIDE):
        for c in range(n // LANES):
            o_ref[:, l * n + c * LANES:l * n + (c + 1) * LANES] = (
                acc_ref[c, pl.ds(l, rows, stride=CMP_STRIDE), :].astype(o_ref.dtype))


def _proj_chunked(x, w, tm, n, col_tile):
    m, k = x.shape
    return pl.pallas_call(
        _chunked_mm_kernel,
        grid=(m // tm,),
        in_specs=[pl.BlockSpec((tm, k), lambda i: (i, 0)),
                  pl.BlockSpec((n, k), lambda i: (col_tile, 0))],
        out_specs=[pl.BlockSpec((tm // CMP_STRIDE, CMP_STRIDE * n), lambda i: (i, 0)),
                   pl.BlockSpec((tm, k), lambda i: (i, 0))],
        out_shape=[jax.ShapeDtypeStruct((m // CMP_STRIDE, CMP_STRIDE * n), bf16),
                   jax.ShapeDtypeStruct((m, k), bf16)],
        scratch_shapes=[pltpu.VMEM((n // LANES, tm, LANES), f32)],
        compiler_params=_params(("parallel",)),
        name="proj_cmp",
    )(x, w)


def _compress(xr, pe2, w1ab, w2, w2t):
    n = xr.shape[0]
    return pl.pallas_call(
        _compress_kernel,
        grid=(2, CMP_STRIDE),
        in_specs=[pl.BlockSpec((n, N_KV * HD), lambda kv, l: (0, 2 * l + kv)),
                  pl.BlockSpec((1, HD, 2 * CMP_HIDDEN), lambda kv, l: (kv, l, 0)),
                  pl.BlockSpec((1, 1, 8, HD), lambda kv, l: (kv, l, 0, 0)),
                  pl.BlockSpec((1, CMP_HIDDEN, HD), lambda kv, l: (kv, 0, 0)),
                  pl.BlockSpec((1, HD, CMP_HIDDEN), lambda kv, l: (kv, 0, 0))],
        out_specs=[pl.BlockSpec((1, N_KV, n, HD), lambda kv, l: (kv, 0, 0, 0)),
                   pl.BlockSpec((1, N_KV, HD, n), lambda kv, l: (kv, 0, 0, 0))],
        out_shape=[jax.ShapeDtypeStruct((2, N_KV, n, HD), bf16),
                   jax.ShapeDtypeStruct((2, N_KV, HD, n), bf16)],
        scratch_shapes=[pltpu.VMEM((N_KV, n, 2 * CMP_HIDDEN), f32),
                        pltpu.VMEM((8, 2 * CMP_HIDDEN), f32)],
        compiler_params=_params(("arbitrary", "arbitrary")),
        name="compress",
    )(xr, w1ab, pe2, w2, w2t)


def _rel_bucket(dist):
    n = jnp.maximum(dist, 0)
    nf = jnp.maximum(n, 1).astype(f32)
    large = REL_EXACT + (jnp.log(nf / REL_EXACT) / math.log(REL_MAX_DIST / REL_EXACT)
                         * (REL_BUCKETS - REL_EXACT)).astype(i32)
    large = jnp.minimum(large, REL_BUCKETS - 1)
    return jnp.where(n < REL_EXACT, n, large)


def _bias_kernel(tbl_ref, o_ref, *, kstride, koff, n_reg, edge_o, edge_max):
    hk = pl.program_id(0)
    ki = lax.broadcasted_iota(i32, (LANES, LANES), 0)
    qi = lax.broadcasted_iota(i32, (LANES, LANES), 1)
    rows = [jnp.broadcast_to(tbl_ref[pl.ds(hk * GROUP + g, 1), :], (LANES, LANES)) for g in range(GROUP)]

    def tile(o, carry):
        is_edge = o == n_reg
        dist = jnp.where(is_edge, edge_o, o) * LANES + qi - kstride * ki - koff
        valid = (dist >= 0) & (dist <= jnp.where(is_edge, edge_max, 2 ** 30))
        bucket = _rel_bucket(dist)
        for g in range(GROUP):
            acc = jnp.take_along_axis(rows[g], bucket, axis=1)
            o_ref[0, o, :, g * LANES:(g + 1) * LANES] = jnp.where(valid, acc * LOG2E, NEG)
        return carry

    lax.fori_loop(0, n_reg + 1, tile, 0)


def _bias_tiles(rel_bias, n_reg, kstride, koff, edge_o, edge_max, name):
    return pl.pallas_call(
        functools.partial(_bias_kernel, kstride=kstride, koff=koff, n_reg=n_reg, edge_o=edge_o,
                          edge_max=edge_max),
        grid=(N_KV,),
        in_specs=[pl.BlockSpec((N_HEADS, LANES), lambda hk: (0, 0))],
        out_specs=pl.BlockSpec((1, n_reg + 1, LANES, GROUP * LANES), lambda hk: (hk, 0, 0, 0)),
        out_shape=jax.ShapeDtypeStruct((N_KV, n_reg + 1, LANES, GROUP * LANES), f32),
        compiler_params=_params(("parallel",)),
        name=name,
    )(jnp.pad(rel_bias.astype(f32).T, ((0, 0), (0, LANES - REL_BUCKETS))))


def _query_lanes(q_ref, nq):
    qb = q_ref[0]
    return jnp.concatenate([qb[g * HD:(g + 1) * HD, h * QB:(h + 1) * QB]
                            for h in range(nq) for g in range(GROUP)], axis=1)


def _cmp_kernel(q_ref, kc_ref, vct_ref, bc_ref, ov_ref, g_ref, *rest, n_sel, n_steps, n_branch, nq):
    nw = nq + WINDOW // QB
    kws, vws = rest[:nw], rest[nw:2 * nw]
    bt_ref, o_ref, ns_ref, sc_ref = rest[2 * nw:]
    step_id = pl.program_id(1)
    b0 = step_id * nq
    nsr = ns_ref.shape[1]
    gl = GROUP * LANES
    qt = _query_lanes(q_ref, nq)
    gate = jax.nn.sigmoid(g_ref[0].astype(f32))
    cmp_per_q = QB // CMP_STRIDE
    q_per_tile = LANES // cmp_per_q

    def core(nt, rows):
        keys = nt * LANES
        s = _dot(kc_ref[0, 0, 0:keys, :], qt)
        m = None
        for h in range(nq):
            mh = None
            for c in range(nt):
                o = b0 + h - q_per_tile * c
                idx = jnp.where(o < 0, N_CMP_BIAS, jnp.minimum(o, N_CMP_BIAS - 1))
                sc = s[c * LANES:(c + 1) * LANES, h * gl:(h + 1) * gl] + bc_ref[0, idx]
                sc_ref[c * LANES:(c + 1) * LANES, h * gl:(h + 1) * gl] = sc
                mc = jnp.max(sc, axis=0, keepdims=True)
                mh = mc if mh is None else jnp.maximum(mh, mc)
            m = mh if m is None else jnp.concatenate([m, mh], axis=1)
        e = jnp.exp2(sc_ref[0:keys, :] - m).astype(bf16)
        lhs = jnp.concatenate([vct_ref[0, 0, :, 0:keys], jnp.ones((16, keys), bf16), ov_ref[0:rows, 0:keys]], axis=0)
        acc = _dot(lhs, e)
        inv = jnp.where(m > 0.5 * NEG, 1.0 / acc[HD:HD + 1], 0.0)
        oct_ = acc[0:HD] * inv
        imp_g = acc[HD + 16:] * inv

        sw = _dot(jnp.concatenate([r[...] for r in kws], axis=0), qt)
        mw = None
        for h in range(nq):
            mh = None
            for jw in range(nw):
                rows_w, cols_w = slice(jw * LANES, (jw + 1) * LANES), slice(h * gl, (h + 1) * gl)
                dl = h + WINDOW // QB - jw
                if dl < 0 or dl > WINDOW // QB:
                    sc_ref[rows_w, cols_w] = jnp.full((LANES, gl), NEG, f32)
                    continue
                tile = N_TOK_BIAS if dl == WINDOW // QB else dl
                before_start = jnp.where(b0 + h - dl >= 0, 0.0, NEG)
                sc = sw[rows_w, cols_w] + bt_ref[0, tile] + before_start
                sc_ref[rows_w, cols_w] = sc
                mc = jnp.max(sc, axis=0, keepdims=True)
                mh = mc if mh is None else jnp.maximum(mh, mc)
            mw = mh if mw is None else jnp.concatenate([mw, mh], axis=1)
        pw = jnp.exp2(sc_ref[0:nw * LANES, :] - mw).astype(bf16)
        vw = jnp.concatenate([r[0] for r in vws], axis=1)
        accw = _dot(jnp.concatenate([vw, jnp.ones((8, nw * LANES), bf16)], axis=0), pw)
        owt = accw[0:HD] / accw[HD:HD + 1]
        for h in range(nq):
            qs = slice(h * QB, (h + 1) * QB)
            for g in range(GROUP):
                ls = slice(h * gl + g * LANES, h * gl + (g + 1) * LANES)
                both = oct_[:, ls] * gate[g * 3:g * 3 + 1, qs] + owt[:, ls] * gate[g * 3 + 2:g * 3 + 3, qs]
                o_ref[qs, g * HD:(g + 1) * HD] = both.T.astype(o_ref.dtype)

        imps = []
        for h in range(nq):
            ig = None
            for g in range(GROUP):
                part = imp_g[:, h * gl + g * LANES:h * gl + (g + 1) * LANES]
                ig = part if ig is None else ig + part
            imps.append(ig)
        imp = jnp.concatenate(imps, axis=1) if nq > 1 else imps[0]

        j = lax.broadcasted_iota(i32, (rows, nq * QB), 0)
        tq = b0 * QB + lax.broadcasted_iota(i32, (rows, nq * QB), 1)
        jt = jnp.right_shift(tq, SLC_LEN.bit_length() - 1)
        forced = (j == 0) | (j == jt) | (j == jt - 1)
        future = j * SLC_LEN > tq
        taken = -3e38
        work = jnp.where(forced, taken, jnp.where(future, -1.0, imp))
        jf = j.astype(f32)
        for _ in range(n_sel - 3):
            mx = jnp.max(work, axis=0, keepdims=True)
            jm = jnp.min(jnp.where(work == mx, jf, 1e9), axis=0, keepdims=True)
            work = jnp.where(jf == jm, taken, work)
        ns_ref[0, 0:rows, :] = jnp.where(future | (work != taken), 1.0, 0.0).astype(bf16)
        if nsr > rows:
            ns_ref[0, rows:, :] = jnp.ones((nsr - rows, nq * QB), bf16)

    per_branch = n_steps // n_branch
    for k in range(n_branch):
        last_b = (k + 1) * per_branch * nq - 1
        nt = last_b // q_per_tile + 1
        rows = (last_b + 1) * (QB // SLC_LEN)
        pl.when(step_id // per_branch == k)(functools.partial(core, nt, rows))


def _cmp_attention(projn, projt, cn, ct, bias_c, bias_t, ov_t, t, nsr, nq):
    nb = t // QB
    ncp = t // CMP_STRIDE
    nslc = t // SLC_LEN
    per_kt = KT // QB
    per = per_kt // nq
    nw = nq + WINDOW // QB
    n_sel = min(SLC_TOPN, nslc)
    assert n_sel > 3
    n_steps = nb // nq
    n_branch = 8 if nb % 128 == 0 else (4 if nb % 64 == 0 else 1)
    resident = dict(pipeline_mode=pl.Buffered(1))

    def kw_spec(jw):
        return pl.BlockSpec((QB, HD), lambda hk, s, jw=jw: (jnp.maximum(s * nq - WINDOW // QB + jw, 0), _C_KW + hk))

    def vw_spec(jw):
        def imap(hk, s, jw=jw):
            bb = jnp.maximum(s * nq - WINDOW // QB + jw, 0)
            return (bb // per_kt, _R_VW + hk, bb % per_kt)
        return pl.BlockSpec((1, HD, QB), imap)

    return pl.pallas_call(
        functools.partial(_cmp_kernel, n_sel=n_sel, n_steps=n_steps, n_branch=n_branch, nq=nq),
        grid=(N_KV, n_steps),
        in_specs=[pl.BlockSpec((1, GROUP * HD, nq * QB), lambda hk, s: (s // per, hk, s % per)),
                  pl.BlockSpec((1, 1, ncp, HD), lambda hk, s: (0, hk, 0, 0), **resident),
                  pl.BlockSpec((1, 1, HD, ncp), lambda hk, s: (1, hk, 0, 0), **resident),
                  pl.BlockSpec((1, N_CMP_BIAS + 1, LANES, GROUP * LANES), lambda hk, s: (hk, 0, 0, 0), **resident),
                  pl.BlockSpec((nslc, ncp), lambda hk, s: (0, 0), **resident),
                  pl.BlockSpec((1, 16, nq * QB), lambda hk, s: (s // per, _R_G * 8 + hk, s % per))]
                 + [kw_spec(jw) for jw in range(nw)]
                 + [vw_spec(jw) for jw in range(nw)]
                 + [pl.BlockSpec((1, N_TOK_BIAS + 1, LANES, GROUP * LANES), lambda hk, s: (hk, 0, 0, 0), **resident)],
        out_specs=[pl.BlockSpec((nq * QB, GROUP * HD), lambda hk, s: (s, hk)),
                   pl.BlockSpec((1, nsr, nq * QB), lambda hk, s: (hk, 0, s))],
        out_shape=[jax.ShapeDtypeStruct((t, N_HEADS * HD), f32),
                   jax.ShapeDtypeStruct((N_KV, nsr, t), bf16)],
        scratch_shapes=[pltpu.VMEM((max(ncp, nw * LANES), nq * GROUP * LANES), f32)],
        compiler_params=_params(("parallel", "arbitrary")),
        name="cmp_attention",
    )(projt, cn, ct, bias_c, ov_t, projt, *([projn] * nw), *([projt] * nw), bias_t)


def _slc_kernel(q_ref, ks_ref, vst_ref, ns_ref, bt_ref, ind_ref, o1_ref, g_ref, o_ref,
                qa_ref, s0_ref, s1_ref, p0_ref, p1_ref, acc_ref, st_ref, *, nq):
    b0 = pl.program_id(1) * nq
    gl = GROUP * LANES
    m_ref, al_ref, mx_ref, off_ref = (st_ref.at[pl.ds(r, 1), :] for r in range(4))
    qt = _query_lanes(q_ref, nq)
    gate = jax.nn.sigmoid(g_ref[0].astype(f32))

    n_chunks = qa_ref.shape[0]
    for jc in range(n_chunks):
        nsj = -ns_ref[0, jc * SEL_CHUNK:(jc + 1) * SEL_CHUNK, :]
        qa_ref[jc, 0:HD, :] = qt
        qa_ref[jc, HD:, :] = jnp.concatenate([nsj[:, h * QB:(h + 1) * QB]
                                              for h in range(nq) for _ in range(GROUP)], axis=1)

    n_tiles = vst_ref.shape[0]
    per_chunk = SEL_CHUNK * SLC_LEN // KT
    sub = KT // LANES

    far_bias = jnp.concatenate([bt_ref[0, N_TOK_BIAS - 1, 0:1, :]] * nq, axis=1)
    ones_rows = jnp.ones((8, KT), bf16)

    def scores(kt, s_ref, far):
        jc = kt // per_chunk
        r0 = pl.multiple_of(kt * KT, KT)
        i0 = pl.multiple_of((kt - jc * per_chunk) * KT, KT)
        kaug = jnp.concatenate([ks_ref[pl.ds(r0, KT), :], ind_ref[pl.ds(i0, KT), :]], axis=1)
        s = _dot(kaug, qa_ref[jc])
        if far:
            s_ref[...] = s
            mx_ref[...] = jnp.max(s, axis=0, keepdims=True) + far_bias
            off_ref[...] = far_bias
            return
        mxs = []
        for h in range(nq):
            cols = slice(h * gl, (h + 1) * gl)
            mx = None
            for jj in range(sub):
                rows = slice(jj * LANES, (jj + 1) * LANES)
                dl = b0 + h - (kt * sub + jj)
                sj = s[rows, cols] + bt_ref[0, jnp.clip(dl, 0, N_TOK_BIAS - 1)]
                s_ref[rows, cols] = sj
                mj = jnp.max(sj, axis=0, keepdims=True)
                mx = mj if mx is None else jnp.maximum(mx, mj)
            mxs.append(mx)
        mx_ref[...] = jnp.concatenate(mxs, axis=1) if nq > 1 else mxs[0]
        off_ref[...] = jnp.zeros_like(off_ref)

    def values(kt):
        return jnp.concatenate([vst_ref[kt], ones_rows], axis=0)

    def step(kt, s_cur, p_cur, s_nxt, p_prev, far):
        m_old = m_ref[...]
        m_new = jnp.maximum(m_old, mx_ref[...])
        shift = m_new - off_ref[...]
        al_old = al_ref[...]
        scores(jnp.minimum(kt + 1, n_tiles - 1), s_nxt, far)
        acc_ref[...] = al_old * acc_ref[...] + _dot(values(jnp.maximum(kt - 1, 0)), p_prev[...])
        p_cur[...] = jnp.exp2(s_cur[...] - shift).astype(bf16)
        m_ref[...] = m_new
        al_ref[...] = jnp.exp2(m_old - m_new)

    acc_ref[...] = jnp.zeros_like(acc_ref)
    m_ref[...] = jnp.full_like(m_ref, NEG)
    al_ref[...] = jnp.ones_like(al_ref)
    p1_ref[...] = jnp.zeros_like(p1_ref)
    scores(0, s0_ref, False)

    n_pairs = (b0 + nq - 1) // (2 * KT // QB) + 1
    n_far = jnp.maximum((b0 - (N_TOK_BIAS - 2)) // sub, 0)
    n_far_pairs = jnp.maximum((n_far - 1) // 2, 0)

    def pair(far):
        def body(i, carry):
            step(2 * i, s0_ref, p0_ref, s1_ref, p1_ref, far)
            step(2 * i + 1, s1_ref, p1_ref, s0_ref, p0_ref, far)
            return carry
        return body

    lax.fori_loop(0, n_far_pairs, pair(True), 0)
    lax.fori_loop(n_far_pairs, n_pairs, pair(False), 0)
    acc = al_ref[...] * acc_ref[...] + _dot(values(2 * n_pairs - 1), p1_ref[...])

    ost = acc[0:HD] / acc[HD:HD + 1]
    for h in range(nq):
        qs = slice(h * QB, (h + 1) * QB)
        for g in range(GROUP):
            ls = slice(h * gl + g * LANES, h * gl + (g + 1) * LANES)
            ds = slice(g * HD, (g + 1) * HD)
            o_ref[qs, ds] = ((ost[:, ls] * gate[g * 3 + 1:g * 3 + 2, qs]).T + o1_ref[qs, ds]).astype(o_ref.dtype)


def _slc_attention(projn, projt, nsel_t, bias_t, ind, o1, t, nq):
    nb = t // QB
    per = KT // QB // nq
    nsr = nsel_t.shape[1]
    lanes = nq * GROUP * LANES
    resident = dict(pipeline_mode=pl.Buffered(1))
    return pl.pallas_call(
        functools.partial(_slc_kernel, nq=nq),
        grid=(N_KV, nb // nq),
        in_specs=[pl.BlockSpec((1, GROUP * HD, nq * QB), lambda hk, bq: (bq // per, hk, bq % per)),
                  pl.BlockSpec((t, HD), lambda hk, bq: (0, _C_KS + hk), **resident),
                  pl.BlockSpec((t // KT, HD, KT), lambda hk, bq: (0, _R_VS + hk, 0), **resident),
                  pl.BlockSpec((1, nsr, nq * QB), lambda hk, bq: (hk, 0, bq)),
                  pl.BlockSpec((1, N_TOK_BIAS + 1, LANES, GROUP * LANES), lambda hk, bq: (hk, 0, 0, 0), **resident),
                  pl.BlockSpec(ind.shape, lambda hk, bq: (0, 0), **resident),
                  pl.BlockSpec((nq * QB, GROUP * HD), lambda hk, bq: (bq, hk)),
                  pl.BlockSpec((1, 16, nq * QB), lambda hk, bq: (bq // per, _R_G * 8 + hk, bq % per))],
        out_specs=pl.BlockSpec((nq * QB, GROUP * HD), lambda hk, bq: (bq, hk)),
        out_shape=jax.ShapeDtypeStruct((t, N_HEADS * HD), bf16),
        scratch_shapes=[pltpu.VMEM((nsr // SEL_CHUNK, 2 * HD, lanes), bf16),
                        pltpu.VMEM((KT, lanes), f32),
                        pltpu.VMEM((KT, lanes), f32),
                        pltpu.VMEM((KT, lanes), bf16),
                        pltpu.VMEM((KT, lanes), bf16),
                        pltpu.VMEM((HD + 8, lanes), f32),
                        pltpu.VMEM((8, lanes), f32)],
        compiler_params=_params(("parallel", "arbitrary")),
        name="slc_attention",
    )(projt, projn, projt, nsel_t, bias_t, ind, o1, projt)


def _layer_norm(z, g, b):
    mu = jnp.mean(z, axis=-1, keepdims=True)
    var = jnp.mean(jnp.square(z - mu), axis=-1, keepdims=True)
    return (z - mu) * lax.rsqrt(var + LN_EPS) * g + b


def _post_kernel(o2_ref, ya_ref, mb_ref, x_ref, p_ref, wao_ref, wmix_ref, wple_ref, wpg_ref,
                 g_ref, b_ref, x1_ref, r2_ref, *, alpha):
    yb = _dot(o2_ref[...], wao_ref[...])
    mixed = ya_ref[...] + jax.nn.sigmoid(mb_ref[...].astype(f32)) * yb
    z = alpha * x_ref[...] + _dot(mixed.astype(bf16), wmix_ref[...])
    x1 = _layer_norm(z, g_ref[...], b_ref[...])
    x1b = x1.astype(bf16)
    x1_ref[...] = x1b
    ple = _dot(p_ref[...].astype(bf16), wple_ref[...]) * jax.nn.sigmoid(_dot(x1b, wpg_ref[...]))
    r2_ref[...] = alpha * x1 + ple


def _post_attention(o2, ya, mab, x2d, p2d, wao, wmix, wple, wpg, g1, b1, alpha, tm):
    t = x2d.shape[0]
    row = lambda w: pl.BlockSpec((tm, w), lambda i: (i, 0))
    full = lambda a: pl.BlockSpec(a.shape, lambda i: (0, 0), pipeline_mode=pl.Buffered(1))
    return pl.pallas_call(
        functools.partial(_post_kernel, alpha=alpha),
        grid=(t // tm,),
        in_specs=[row(D_MODEL), row(D_MODEL),
                  pl.BlockSpec((tm, D_MODEL), lambda i: (i, 1)),
                  row(D_MODEL), row(PLE_DIM),
                  full(wao), full(wmix), full(wple), full(wpg), full(g1), full(b1)],
        out_specs=[row(D_MODEL), row(D_MODEL)],
        out_shape=[jax.ShapeDtypeStruct((t, D_MODEL), bf16),
                   jax.ShapeDtypeStruct((t, D_MODEL), f32)],
        compiler_params=_params(("parallel",)),
        name="post_attention",
    )(o2, ya, mab, x2d, p2d, wao, wmix, wple, wpg, g1, b1)


def _mlp_kernel(x1_ref, r2_ref, wu_ref, wd_ref, g_ref, b_ref, o_ref):
    k = pl.program_id(1)

    @pl.when(k == 0)
    def _():
        o_ref[...] = r2_ref[...]

    h = jnp.square(jnp.maximum(_dot(x1_ref[...], wu_ref[...]), 0.0))
    o_ref[...] += _dot(h.astype(bf16), wd_ref[...])

    @pl.when(k == pl.num_programs(1) - 1)
    def _():
        o_ref[...] = _layer_norm(o_ref[...], g_ref[...], b_ref[...])


def _mlp(x1b, r2, wu, wd, g2, b2, tm, fc):
    t = x1b.shape[0]
    return pl.pallas_call(
        _mlp_kernel,
        grid=(t // tm, D_FF // fc),
        in_specs=[pl.BlockSpec((tm, D_MODEL), lambda i, k: (i, 0)),
                  pl.BlockSpec((tm, D_MODEL), lambda i, k: (i, 0)),
                  pl.BlockSpec((D_MODEL, fc), lambda i, k: (0, k)),
                  pl.BlockSpec((fc, D_MODEL), lambda i, k: (k, 0)),
                  pl.BlockSpec((1, D_MODEL), lambda i, k: (0, 0)),
                  pl.BlockSpec((1, D_MODEL), lambda i, k: (0, 0))],
        out_specs=pl.BlockSpec((tm, D_MODEL), lambda i, k: (i, 0)),
        out_shape=jax.ShapeDtypeStruct((t, D_MODEL), f32),
        compiler_params=_params(("parallel", "arbitrary")),
        name="mlp",
    )(x1b, r2, wu, wd, g2, b2)


def _layer(x2d, p2d, w_in, conv_w, pe_k, w1_k, w2_k, pe_v, w1_v, w2_v, w_conv_out, w_attn_out,
           w_mix_out, ln1_g, ln1_b, w_mlp_up, w_mlp_down, w_ple, w_ple_gate, ln2_g, ln2_b,
           rel_bias, alpha):
    t = x2d.shape[0]
    assert t % 1024 == 0 and x2d.shape[1] == D_MODEL
    kvw = N_KV * HD
    o_q = 3 * D_MODEL
    o_kc = o_q + N_HEADS * HD
    o_ng = o_kc + 6 * kvw
    o_ma = o_ng + N_HEADS * 3

    n_in = w_in.shape[1]
    cast_rows = max(r for r in range(16, D_MODEL + 1, 16) if n_in % r == 0)
    w_bf = _to_bf16(w_in.T, cast_rows, "cast_w_in")

    def cols(a, n):
        return w_bf[a:a + n]

    tn = kvw
    conv_tiles = 3 * D_MODEL // tn
    ks_tile, kw_tile = (o_kc + 2 * kvw) // tn, (o_kc + 4 * kvw) // tn
    assert _C_KS * LANES == conv_tiles * tn and _C_KW * LANES == (conv_tiles + 1) * tn

    def nat_tile(j):
        return jnp.where(j < conv_tiles, j, jnp.where(j == conv_tiles, ks_tile, kw_tile))

    wng = jnp.pad(cols(o_ng, N_HEADS * 3).reshape(N_KV, GROUP * 3, D_MODEL), ((0, 0), (0, 16 - GROUP * 3), (0, 0)))
    w_t = jnp.concatenate(
        [cols(o_q, N_HEADS * HD), cols(o_kc + 3 * kvw, kvw), cols(o_kc + 5 * kvw, kvw),
         wng.reshape(N_KV * 16, D_MODEL),
         jnp.zeros((_N_T - _R_G * LANES - N_KV * 16, D_MODEL), bf16)], axis=0)
    row_scale = jnp.concatenate([jnp.full((N_HEADS * HD, 1), HD ** -0.5 * LOG2E, f32),
                                 jnp.ones((_N_T - N_HEADS * HD, 1), f32)], axis=0)

    kvc, xb = _proj_chunked(x2d, w_bf, 1024, 2 * kvw, o_kc // (2 * kvw))
    projn = _proj_nat(xb, w_bf, 2048, tn, "proj_nat", _N_NAT, nat_tile)
    mab = _proj_nat(xb, cols(o_ma, 2 * D_MODEL), 2048, 1024, "proj_gate")
    projt = _proj_t(w_t, xb, row_scale, _N_T, KT)

    ya = _mixer_a(projn, mab, conv_w, w_conv_out.astype(bf16), 512)

    half = CMP_STRIDE * HD
    w1ab = jnp.stack([jnp.concatenate([w[:half], w[half:]], axis=1) for w in (w1_k, w1_v)]).astype(bf16)
    pe2 = jnp.stack([jnp.stack([pe[:CMP_STRIDE], pe[CMP_STRIDE:]], axis=1) for pe in (pe_k, pe_v)])
    pe2 = jnp.pad(pe2, ((0, 0), (0, 0), (0, 6), (0, 0)))
    w2 = jnp.stack([w2_k, w2_v]).astype(bf16)
    cn, ct = _compress(kvc, pe2, w1ab, w2, jnp.swapaxes(w2, 1, 2))

    bias_t = _bias_tiles(rel_bias, N_TOK_BIAS, 1, 0, WINDOW // QB, WINDOW - 1, "bias_tok")
    bias_c = _bias_tiles(rel_bias, N_CMP_BIAS, CMP_STRIDE, CMP_LEN - 1, 0, -1, "bias_cmp")

    ncp = t // CMP_STRIDE
    nslc = t // SLC_LEN
    nsr = -(-nslc // SEL_CHUNK) * SEL_CHUNK
    ci = jnp.arange(ncp)[None, :] * CMP_STRIDE
    sj = jnp.arange(nslc)[:, None] * SLC_LEN
    ov_t = ((ci < sj + SLC_LEN) & (ci + CMP_LEN > sj)).astype(bf16)
    pos = jnp.arange(min(t, SEL_CHUNK * SLC_LEN))[:, None] // SLC_LEN
    ind = jnp.where(pos == jnp.arange(SEL_CHUNK)[None, :], SEL_BIG, 0.0).astype(bf16)

    o1, nsel_t = _cmp_attention(projn, projt, cn, ct, bias_c, bias_t, ov_t, t, nsr, CMP_NQ)
    o2 = _slc_attention(projn, projt, nsel_t, bias_t, ind, o1, t, SLC_NQ)

    x1b, r2 = _post_attention(o2, ya, mab, x2d, p2d, w_attn_out.astype(bf16), w_mix_out.astype(bf16),
                              w_ple.astype(bf16), w_ple_gate.astype(bf16),
                              ln1_g[None, :], ln1_b[None, :], alpha, 256)
    return _mlp(x1b, r2, w_mlp_up.astype(bf16), w_mlp_down.astype(bf16), ln2_g[None, :], ln2_b[None, :], 1024, 512)


def kernel(x, p, w_in, conv_w, cmp_pe_k, cmp_w1_k, cmp_w2_k, cmp_pe_v, cmp_w1_v, cmp_w2_v, w_conv_out, w_attn_out, w_mix_out, ln1_g, ln1_b, w_mlp_up, w_mlp_down, w_ple, w_ple_gate, ln2_g, ln2_b, rel_bias):
    bsz, t, d = x.shape
    assert bsz == 1
    depth = w_in.shape[0]
    alpha = (2 * depth) ** 0.25
    x2d = x[0]
    for i in range(depth):
        x2d = _layer(x2d, p[i, 0], w_in[i], conv_w[i], cmp_pe_k[i], cmp_w1_k[i], cmp_w2_k[i],
                     cmp_pe_v[i], cmp_w1_v[i], cmp_w2_v[i], w_conv_out[i], w_attn_out[i], w_mix_out[i],
                     ln1_g[i], ln1_b[i], w_mlp_up[i], w_mlp_down[i], w_ple[i], w_ple_gate[i],
                     ln2_g[i], ln2_b[i], rel_bias, alpha)
    return x2d[None]
```

```python
import functools
import math

import jax
import jax.numpy as jnp
from jax import lax
from jax.experimental import pallas as pl
from jax.experimental.pallas import tpu as pltpu

f32 = jnp.float32
bf16 = jnp.bfloat16
i32 = jnp.int32

D_MODEL = 2048
N_HEADS = 16
N_KV = 4
GROUP = N_HEADS // N_KV
HD = 128
CMP_LEN = 32
CMP_STRIDE = 16
CMP_HIDDEN = 2 * HD
SLC_LEN = 64
SLC_TOPN = 16
WINDOW = 512
QB = 128
D_FF = 4 * D_MODEL
PLE_DIM = 256
REL_BUCKETS = 32
REL_EXACT = REL_BUCKETS // 2
REL_MAX_DIST = 4096
LN_EPS = 1e-5
NEG = -1e30
LOG2E = 1.4426950408889634

LANES = 128
KT = 512
SEL_BIG = 2.0 ** 100
SEL_CHUNK = 128
SLC_NQ = 1
CMP_NQ = 2
N_TOK_BIAS = 25
N_CMP_BIAS = 40
VMEM_LIMIT = 56 * 1024 * 1024

_C_KS, _C_KW = 48, 52
_N_NAT = 56 * LANES
_R_VS, _R_VW, _R_G = 16, 20, 24
_N_T = 25 * LANES


def _dot(a, b):
    return jnp.dot(a, b, preferred_element_type=f32)


def _dot_nt(a, b):
    return lax.dot_general(a, b, (((1,), (1,)), ((), ())), preferred_element_type=f32)


def _params(sem, vmem=VMEM_LIMIT):
    return pltpu.CompilerParams(dimension_semantics=sem, vmem_limit_bytes=vmem)


def _mm_kernel(x_ref, w_ref, o_ref):
    o_ref[...] = _dot_nt(x_ref[...], w_ref[...]).astype(o_ref.dtype)


def _proj_nat(xb, wt, tm, tn, name, n=None, col_tile=lambda j: j):
    m, k = xb.shape
    n = wt.shape[0] if n is None else n
    return pl.pallas_call(
        _mm_kernel,
        grid=(m // tm, n // tn),
        in_specs=[pl.BlockSpec((tm, k), lambda i, j: (i, 0)),
                  pl.BlockSpec((tn, k), lambda i, j: (col_tile(j), 0))],
        out_specs=pl.BlockSpec((tm, tn), lambda i, j: (i, j)),
        out_shape=jax.ShapeDtypeStruct((m, n), bf16),
        compiler_params=_params(("parallel", "arbitrary")),
        name=name,
    )(xb, wt)


def _mmt_kernel(w_ref, x_ref, s_ref, o_ref):
    o_ref[0] = (_dot_nt(w_ref[...], x_ref[...]) * s_ref[...]).astype(o_ref.dtype)


def _proj_t(w, xb, row_scale, tm, tn):
    n, k = w.shape
    t = xb.shape[0]
    return pl.pallas_call(
        _mmt_kernel,
        grid=(n // tm, t // tn),
        in_specs=[pl.BlockSpec((tm, k), lambda i, j: (i, 0)),
                  pl.BlockSpec((tn, k), lambda i, j: (j, 0)),
                  pl.BlockSpec((tm, 1), lambda i, j: (i, 0))],
        out_specs=pl.BlockSpec((1, tm, tn), lambda i, j: (j, i, 0)),
        out_shape=jax.ShapeDtypeStruct((t // tn, n, tn), bf16),
        compiler_params=_params(("parallel", "arbitrary")),
        name="proj_t",
    )(w, xb, row_scale)


def _cast_kernel(x_ref, o_ref):
    o_ref[...] = x_ref[...].astype(o_ref.dtype)


def _to_bf16(w, tm, name):
    r, c = w.shape
    return pl.pallas_call(
        _cast_kernel,
        grid=(r // tm,),
        in_specs=[pl.BlockSpec((tm, c), lambda i: (i, 0))],
        out_specs=pl.BlockSpec((tm, c), lambda i: (i, 0)),
        out_shape=jax.ShapeDtypeStruct((r, c), bf16),
        compiler_params=_params(("parallel",)),
        name=name,
    )(w)


def _ya_kernel(bg_ref, cg_ref, hx_ref, cgh_ref, hxh_ref, cw_ref, w_ref, ma_ref, o_ref, a_ref):
    i = pl.program_id(0)
    tm = bg_ref.shape[0]
    cc = 512
    row = lax.broadcasted_iota(i32, (tm, cc), 0)
    for c in range(D_MODEL // cc):
        sl = slice(c * cc, (c + 1) * cc)
        u = cg_ref[:, sl].astype(f32) * hx_ref[:, sl].astype(f32)
        uh = cgh_ref[:, sl].astype(f32) * hxh_ref[:, sl].astype(f32)
        uh = jnp.where(i > 0, uh, 0.0)
        u1 = jnp.where(row == 0, uh[15:16, :], pltpu.roll(u, 1, 0))
        u2 = jnp.where(row == 0, uh[14:15, :], jnp.where(row == 1, uh[15:16, :], pltpu.roll(u, 2, 0)))
        cw = cw_ref[:, sl]
        conv = cw[0:1, :] * u2 + cw[1:2, :] * u1 + cw[2:3, :] * u
        a_ref[:, sl] = (bg_ref[:, sl].astype(f32) * conv).astype(bf16)
    y = _dot(a_ref[...], w_ref[...])
    o_ref[...] = jax.nn.sigmoid(ma_ref[...].astype(f32)) * y


def _mixer_a(projn, mab, conv_w, w_out, tm):
    t = projn.shape[0]
    hb = tm // 16
    wide = lambda c: pl.BlockSpec((tm, D_MODEL), lambda i, c=c: (i, c))
    halo = lambda c: pl.BlockSpec((16, D_MODEL), lambda i, c=c: (jnp.maximum(i * hb - 1, 0), c))
    return pl.pallas_call(
        _ya_kernel,
        grid=(t // tm,),
        in_specs=[wide(0), wide(1), wide(2), halo(1), halo(2),
                  pl.BlockSpec((3, D_MODEL), lambda i: (0, 0)),
                  pl.BlockSpec((D_MODEL, D_MODEL), lambda i: (0, 0)),
                  wide(0)],
        out_specs=pl.BlockSpec((tm, D_MODEL), lambda i: (i, 0)),
        out_shape=jax.ShapeDtypeStruct((t, D_MODEL), f32),
        scratch_shapes=[pltpu.VMEM((tm, D_MODEL), bf16)],
        compiler_params=_params(("parallel",)),
        name="mixer_a",
    )(projn, projn, projn, projn, projn, conv_w, w_out, mab)


def _compress_kernel(x_ref, w_ref, pe_ref, w2_ref, w2t_ref, on_ref, ot_ref, acc_ref, pet_ref):
    l = pl.program_id(1)
    nl = pl.num_programs(1)
    n = x_ref.shape[0]

    @pl.when(l == 0)
    def _():
        acc_ref[...] = jnp.zeros_like(acc_ref)
        pet_ref[...] = jnp.zeros_like(pet_ref)

    wblk = w_ref[0]
    for h in range(N_KV):
        acc_ref[h] += _dot(x_ref[:, h * HD:(h + 1) * HD], wblk)
    pet_ref[...] += _dot(pe_ref[0, 0].astype(bf16), wblk)

    @pl.when(l == nl - 1)
    def _():
        r = pet_ref[...]
        peterm = r[0:1, :CMP_HIDDEN] + r[1:2, CMP_HIDDEN:]
        for h in range(N_KV):
            ab = acc_ref[h]
            hid = ab[:, :CMP_HIDDEN] + pltpu.roll(ab[:, CMP_HIDDEN:], n - 1, 0) + peterm
            gl = jax.nn.gelu(hid).astype(bf16)
            on_ref[0, h] = _dot(gl, w2_ref[0]).astype(bf16)
            ot_ref[0, h] = _dot_nt(w2t_ref[0], gl).astype(bf16)


def _chunked_mm_kernel(x_ref, w_ref, o_ref, xb_ref, acc_ref):
    xb = x_ref[...].astype(bf16)
    xb_ref[...] = xb
    acc = _dot_nt(xb, w_ref[...])
    n = w_ref.shape[0]
    rows = o_ref.shape[0]
    for c in range(n // LANES):
        acc_ref[c] = acc[:, c * LANES:(c + 1) * LANES]
    for l in range(CMP_STRIDE):
        for c in range(n // LANES):
            o_ref[:, l * n + c * LANES:l * n + (c + 1) * LANES] = (
                acc_ref[c, pl.ds(l, rows, stride=CMP_STRIDE), :].astype(o_ref.dtype))


def _proj_chunked(x, w, tm, n, col_tile):
    m, k = x.shape
    return pl.pallas_call(
        _chunked_mm_kernel,
        grid=(m // tm,),
        in_specs=[pl.BlockSpec((tm, k), lambda i: (i, 0)),
                  pl.BlockSpec((n, k), lambda i: (col_tile, 0))],
        out_specs=[pl.BlockSpec((tm // CMP_STRIDE, CMP_STRIDE * n), lambda i: (i, 0)),
                   pl.BlockSpec((tm, k), lambda i: (i, 0))],
        out_shape=[jax.ShapeDtypeStruct((m // CMP_STRIDE, CMP_STRIDE * n), bf16),
                   jax.ShapeDtypeStruct((m, k), bf16)],
        scratch_shapes=[pltpu.VMEM((n // LANES, tm, LANES), f32)],
        compiler_params=_params(("parallel",)),
        name="proj_cmp",
    )(x, w)


def _compress(xr, pe2, w1ab, w2, w2t):
    n = xr.shape[0]
    return pl.pallas_call(
        _compress_kernel,
        grid=(2, CMP_STRIDE),
        in_specs=[pl.BlockSpec((n, N_KV * HD), lambda kv, l: (0, 2 * l + kv)),
                  pl.BlockSpec((1, HD, 2 * CMP_HIDDEN), lambda kv, l: (kv, l, 0)),
                  pl.BlockSpec((1, 1, 8, HD), lambda kv, l: (kv, l, 0, 0)),
                  pl.BlockSpec((1, CMP_HIDDEN, HD), lambda kv, l: (kv, 0, 0)),
                  pl.BlockSpec((1, HD, CMP_HIDDEN), lambda kv, l: (kv, 0, 0))],
        out_specs=[pl.BlockSpec((1, N_KV, n, HD), lambda kv, l: (kv, 0, 0, 0)),
                   pl.BlockSpec((1, N_KV, HD, n), lambda kv, l: (kv, 0, 0, 0))],
        out_shape=[jax.ShapeDtypeStruct((2, N_KV, n, HD), bf16),
                   jax.ShapeDtypeStruct((2, N_KV, HD, n), bf16)],
        scratch_shapes=[pltpu.VMEM((N_KV, n, 2 * CMP_HIDDEN), f32),
                        pltpu.VMEM((8, 2 * CMP_HIDDEN), f32)],
        compiler_params=_params(("arbitrary", "arbitrary")),
        name="compress",
    )(xr, w1ab, pe2, w2, w2t)


def _rel_bucket(dist):
    n = jnp.maximum(dist, 0)
    nf = jnp.maximum(n, 1).astype(f32)
    large = REL_EXACT + (jnp.log(nf / REL_EXACT) / math.log(REL_MAX_DIST / REL_EXACT)
                         * (REL_BUCKETS - REL_EXACT)).astype(i32)
    large = jnp.minimum(large, REL_BUCKETS - 1)
    return jnp.where(n < REL_EXACT, n, large)


def _bias_kernel(tbl_ref, o_ref, *, kstride, koff, n_reg, edge_o, edge_max):
    hk = pl.program_id(0)
    ki = lax.broadcasted_iota(i32, (LANES, LANES), 0)
    qi = lax.broadcasted_iota(i32, (LANES, LANES), 1)
    rows = [jnp.broadcast_to(tbl_ref[pl.ds(hk * GROUP + g, 1), :], (LANES, LANES)) for g in range(GROUP)]

    def tile(o, carry):
        is_edge = o == n_reg
        dist = jnp.where(is_edge, edge_o, o) * LANES + qi - kstride * ki - koff
        valid = (dist >= 0) & (dist <= jnp.where(is_edge, edge_max, 2 ** 30))
        bucket = _rel_bucket(dist)
        for g in range(GROUP):
            acc = jnp.take_along_axis(rows[g], bucket, axis=1)
            o_ref[0, o, :, g * LANES:(g + 1) * LANES] = jnp.where(valid, acc * LOG2E, NEG)
        return carry

    lax.fori_loop(0, n_reg + 1, tile, 0)


def _bias_tiles(rel_bias, n_reg, kstride, koff, edge_o, edge_max, name):
    return pl.pallas_call(
        functools.partial(_bias_kernel, kstride=kstride, koff=koff, n_reg=n_reg, edge_o=edge_o,
                          edge_max=edge_max),
        grid=(N_KV,),
        in_specs=[pl.BlockSpec((N_HEADS, LANES), lambda hk: (0, 0))],
        out_specs=pl.BlockSpec((1, n_reg + 1, LANES, GROUP * LANES), lambda hk: (hk, 0, 0, 0)),
        out_shape=jax.ShapeDtypeStruct((N_KV, n_reg + 1, LANES, GROUP * LANES), f32),
        compiler_params=_params(("parallel",)),
        name=name,
    )(jnp.pad(rel_bias.astype(f32).T, ((0, 0), (0, LANES - REL_BUCKETS))))


def _query_lanes(q_ref, nq):
    qb = q_ref[0]
    return jnp.concatenate([qb[g * HD:(g + 1) * HD, h * QB:(h + 1) * QB]
                            for h in range(nq) for g in range(GROUP)], axis=1)


def _cmp_kernel(q_ref, kc_ref, vct_ref, bc_ref, ov_ref, g_ref, *rest, n_sel, n_steps, n_branch, nq):
    nw = nq + WINDOW // QB
    kws, vws = rest[:nw], rest[nw:2 * nw]
    bt_ref, o_ref, ns_ref, sc_ref = rest[2 * nw:]
    step_id = pl.program_id(1)
    b0 = step_id * nq
    nsr = ns_ref.shape[1]
    gl = GROUP * LANES
    qt = _query_lanes(q_ref, nq)
    gate = jax.nn.sigmoid(g_ref[0].astype(f32))
    cmp_per_q = QB // CMP_STRIDE
    q_per_tile = LANES // cmp_per_q

    def core(nt, rows):
        keys = nt * LANES
        s = _dot(kc_ref[0, 0, 0:keys, :], qt)
        m = None
        for h in range(nq):
            mh = None
            for c in range(nt):
                o = b0 + h - q_per_tile * c
                idx = jnp.where(o < 0, N_CMP_BIAS, jnp.minimum(o, N_CMP_BIAS - 1))
                sc = s[c * LANES:(c + 1) * LANES, h * gl:(h + 1) * gl] + bc_ref[0, idx]
                sc_ref[c * LANES:(c + 1) * LANES, h * gl:(h + 1) * gl] = sc
                mc = jnp.max(sc, axis=0, keepdims=True)
                mh = mc if mh is None else jnp.maximum(mh, mc)
            m = mh if m is None else jnp.concatenate([m, mh], axis=1)
        e = jnp.exp2(sc_ref[0:keys, :] - m).astype(bf16)
        lhs = jnp.concatenate([vct_ref[0, 0, :, 0:keys], jnp.ones((16, keys), bf16), ov_ref[0:rows, 0:keys]], axis=0)
        acc = _dot(lhs, e)
        inv = jnp.where(m > 0.5 * NEG, 1.0 / acc[HD:HD + 1], 0.0)
        oct_ = acc[0:HD] * inv
        imp_g = acc[HD + 16:] * inv

        sw = _dot(jnp.concatenate([r[...] for r in kws], axis=0), qt)
        col_parts = []
        for h in range(nq):
            parts = []
            for jw in range(nw):
                dl = h + WINDOW // QB - jw
                if dl < 0 or dl > WINDOW // QB:
                    parts.append(jnp.full((LANES, gl), NEG, f32))
                    continue
                tile = N_TOK_BIAS if dl == WINDOW // QB else dl
                before_start = jnp.where(b0 + h - dl >= 0, 0.0, NEG)
                parts.append(sw[jw * LANES:(jw + 1) * LANES, h * gl:(h + 1) * gl] + bt_ref[0, tile] + before_start)
            col_parts.append(jnp.concatenate(parts, axis=0))
        sw = jnp.concatenate(col_parts, axis=1) if nq > 1 else col_parts[0]
        pw = jnp.exp2(sw - jnp.max(sw, axis=0, keepdims=True)).astype(bf16)
        vw = jnp.concatenate([r[0] for r in vws], axis=1)
        accw = _dot(jnp.concatenate([vw, jnp.ones((8, nw * LANES), bf16)], axis=0), pw)
        owt = accw[0:HD] / accw[HD:HD + 1]
        for h in range(nq):
            qs = slice(h * QB, (h + 1) * QB)
            for g in range(GROUP):
                ls = slice(h * gl + g * LANES, h * gl + (g + 1) * LANES)
                both = oct_[:, ls] * gate[g * 3:g * 3 + 1, qs] + owt[:, ls] * gate[g * 3 + 2:g * 3 + 3, qs]
                o_ref[qs, g * HD:(g + 1) * HD] = both.T.astype(o_ref.dtype)

        imps = []
        for h in range(nq):
            ig = None
            for g in range(GROUP):
                part = imp_g[:, h * gl + g * LANES:h * gl + (g + 1) * LANES]
                ig = part if ig is None else ig + part
            imps.append(ig)
        imp = jnp.concatenate(imps, axis=1) if nq > 1 else imps[0]

        j = lax.broadcasted_iota(i32, (rows, nq * QB), 0)
        tq = b0 * QB + lax.broadcasted_iota(i32, (rows, nq * QB), 1)
        jt = jnp.right_shift(tq, SLC_LEN.bit_length() - 1)
        forced = (j == 0) | (j == jt) | (j == jt - 1)
        future = j * SLC_LEN > tq
        taken = -3e38
        work = jnp.where(forced, taken, jnp.where(future, -1.0, imp))
        jf = j.astype(f32)
        for _ in range(n_sel - 3):
            mx = jnp.max(work, axis=0, keepdims=True)
            jm = jnp.min(jnp.where(work == mx, jf, 1e9), axis=0, keepdims=True)
            work = jnp.where(jf == jm, taken, work)
        ns_ref[0, 0:rows, :] = jnp.where(future | (work != taken), 1.0, 0.0).astype(bf16)
        if nsr > rows:
            ns_ref[0, rows:, :] = jnp.ones((nsr - rows, nq * QB), bf16)

    per_branch = n_steps // n_branch
    for k in range(n_branch):
        last_b = (k + 1) * per_branch * nq - 1
        nt = last_b // q_per_tile + 1
        rows = (last_b + 1) * (QB // SLC_LEN)
        pl.when(step_id // per_branch == k)(functools.partial(core, nt, rows))


def _cmp_attention(projn, projt, cn, ct, bias_c, bias_t, ov_t, t, nsr, nq):
    nb = t // QB
    ncp = t // CMP_STRIDE
    nslc = t // SLC_LEN
    per_kt = KT // QB
    per = per_kt // nq
    nw = nq + WINDOW // QB
    n_sel = min(SLC_TOPN, nslc)
    assert n_sel > 3
    n_steps = nb // nq
    n_branch = 8 if nb % 128 == 0 else (4 if nb % 64 == 0 else 1)
    resident = dict(pipeline_mode=pl.Buffered(1))

    def kw_spec(jw):
        return pl.BlockSpec((QB, HD), lambda hk, s, jw=jw: (jnp.maximum(s * nq - WINDOW // QB + jw, 0), _C_KW + hk))

    def vw_spec(jw):
        def imap(hk, s, jw=jw):
            bb = jnp.maximum(s * nq - WINDOW // QB + jw, 0)
            return (bb // per_kt, _R_VW + hk, bb % per_kt)
        return pl.BlockSpec((1, HD, QB), imap)

    return pl.pallas_call(
        functools.partial(_cmp_kernel, n_sel=n_sel, n_steps=n_steps, n_branch=n_branch, nq=nq),
        grid=(N_KV, n_steps),
        in_specs=[pl.BlockSpec((1, GROUP * HD, nq * QB), lambda hk, s: (s // per, hk, s % per)),
                  pl.BlockSpec((1, 1, ncp, HD), lambda hk, s: (0, hk, 0, 0), **resident),
                  pl.BlockSpec((1, 1, HD, ncp), lambda hk, s: (1, hk, 0, 0), **resident),
                  pl.BlockSpec((1, N_CMP_BIAS + 1, LANES, GROUP * LANES), lambda hk, s: (hk, 0, 0, 0), **resident),
                  pl.BlockSpec((nslc, ncp), lambda hk, s: (0, 0), **resident),
                  pl.BlockSpec((1, 16, nq * QB), lambda hk, s: (s // per, _R_G * 8 + hk, s % per))]
                 + [kw_spec(jw) for jw in range(nw)]
                 + [vw_spec(jw) for jw in range(nw)]
                 + [pl.BlockSpec((1, N_TOK_BIAS + 1, LANES, GROUP * LANES), lambda hk, s: (hk, 0, 0, 0), **resident)],
        out_specs=[pl.BlockSpec((nq * QB, GROUP * HD), lambda hk, s: (s, hk)),
                   pl.BlockSpec((1, nsr, nq * QB), lambda hk, s: (hk, 0, s))],
        out_shape=[jax.ShapeDtypeStruct((t, N_HEADS * HD), f32),
                   jax.ShapeDtypeStruct((N_KV, nsr, t), bf16)],
        scratch_shapes=[pltpu.VMEM((ncp, nq * GROUP * LANES), f32)],
        compiler_params=_params(("parallel", "arbitrary")),
        name="cmp_attention",
    )(projt, cn, ct, bias_c, ov_t, projt, *([projn] * nw), *([projt] * nw), bias_t)


def _slc_kernel(q_ref, ks_ref, vst_ref, ns_ref, bt_ref, ind_ref, o1_ref, g_ref, o_ref,
                qa_ref, s0_ref, s1_ref, p0_ref, p1_ref, acc_ref, st_ref, *, nq):
    b0 = pl.program_id(1) * nq
    gl = GROUP * LANES
    m_ref, al_ref, mx_ref, off_ref = (st_ref.at[pl.ds(r, 1), :] for r in range(4))
    qt = _query_lanes(q_ref, nq)
    gate = jax.nn.sigmoid(g_ref[0].astype(f32))

    n_chunks = qa_ref.shape[0]
    for jc in range(n_chunks):
        nsj = -ns_ref[0, jc * SEL_CHUNK:(jc + 1) * SEL_CHUNK, :]
        qa_ref[jc, 0:HD, :] = qt
        qa_ref[jc, HD:, :] = jnp.concatenate([nsj[:, h * QB:(h + 1) * QB]
                                              for h in range(nq) for _ in range(GROUP)], axis=1)

    n_tiles = vst_ref.shape[0]
    per_chunk = SEL_CHUNK * SLC_LEN // KT
    sub = KT // LANES

    far_bias = jnp.concatenate([bt_ref[0, N_TOK_BIAS - 1, 0:1, :]] * nq, axis=1)
    ones_rows = jnp.ones((8, KT), bf16)

    def scores(kt, s_ref, far):
        jc = kt // per_chunk
        r0 = pl.multiple_of(kt * KT, KT)
        i0 = pl.multiple_of((kt - jc * per_chunk) * KT, KT)
        kaug = jnp.concatenate([ks_ref[pl.ds(r0, KT), :], ind_ref[pl.ds(i0, KT), :]], axis=1)
        s = _dot(kaug, qa_ref[jc])
        if far:
            s_ref[...] = s
            mx_ref[...] = jnp.max(s, axis=0, keepdims=True) + far_bias
            off_ref[...] = far_bias
            return
        mxs = []
        for h in range(nq):
            cols = slice(h * gl, (h + 1) * gl)
            mx = None
            for jj in range(sub):
                rows = slice(jj * LANES, (jj + 1) * LANES)
                dl = b0 + h - (kt * sub + jj)
                sj = s[rows, cols] + bt_ref[0, jnp.clip(dl, 0, N_TOK_BIAS - 1)]
                s_ref[rows, cols] = sj
                mj = jnp.max(sj, axis=0, keepdims=True)
                mx = mj if mx is None else jnp.maximum(mx, mj)
            mxs.append(mx)
        mx_ref[...] = jnp.concatenate(mxs, axis=1) if nq > 1 else mxs[0]
        off_ref[...] = jnp.zeros_like(off_ref)

    def values(kt):
        return jnp.concatenate([vst_ref[kt], ones_rows], axis=0)

    def step(kt, s_cur, p_cur, s_nxt, p_prev, far):
        m_old = m_ref[...]
        m_new = jnp.maximum(m_old, mx_ref[...])
        shift = m_new - off_ref[...]
        al_old = al_ref[...]
        scores(jnp.minimum(kt + 1, n_tiles - 1), s_nxt, far)
        acc_ref[...] = al_old * acc_ref[...] + _dot(values(jnp.maximum(kt - 1, 0)), p_prev[...])
        p_cur[...] = jnp.exp2(s_cur[...] - shift).astype(bf16)
        m_ref[...] = m_new
        al_ref[...] = jnp.exp2(m_old - m_new)

    acc_ref[...] = jnp.zeros_like(acc_ref)
    m_ref[...] = jnp.full_like(m_ref, NEG)
    al_ref[...] = jnp.ones_like(al_ref)
    p1_ref[...] = jnp.zeros_like(p1_ref)
    scores(0, s0_ref, False)

    n_pairs = (b0 + nq - 1) // (2 * KT // QB) + 1
    n_far = jnp.maximum((b0 - (N_TOK_BIAS - 2)) // sub, 0)
    n_far_pairs = jnp.maximum((n_far - 1) // 2, 0)

    def pair(far):
        def body(i, carry):
            step(2 * i, s0_ref, p0_ref, s1_ref, p1_ref, far)
            step(2 * i + 1, s1_ref, p1_ref, s0_ref, p0_ref, far)
            return carry
        return body

    lax.fori_loop(0, n_far_pairs, pair(True), 0)
    lax.fori_loop(n_far_pairs, n_pairs, pair(False), 0)
    acc = al_ref[...] * acc_ref[...] + _dot(values(2 * n_pairs - 1), p1_ref[...])

    ost = acc[0:HD] / acc[HD:HD + 1]
    for h in range(nq):
        qs = slice(h * QB, (h + 1) * QB)
        for g in range(GROUP):
            ls = slice(h * gl + g * LANES, h * gl + (g + 1) * LANES)
            ds = slice(g * HD, (g + 1) * HD)
            o_ref[qs, ds] = ((ost[:, ls] * gate[g * 3 + 1:g * 3 + 2, qs]).T + o1_ref[qs, ds]).astype(o_ref.dtype)


def _slc_attention(projn, projt, nsel_t, bias_t, ind, o1, t, nq):
    nb = t // QB
    per = KT // QB // nq
    nsr = nsel_t.shape[1]
    lanes = nq * GROUP * LANES
    resident = dict(pipeline_mode=pl.Buffered(1))
    return pl.pallas_call(
        functools.partial(_slc_kernel, nq=nq),
        grid=(N_KV, nb // nq),
        in_specs=[pl.BlockSpec((1, GROUP * HD, nq * QB), lambda hk, bq: (bq // per, hk, bq % per)),
                  pl.BlockSpec((t, HD), lambda hk, bq: (0, _C_KS + hk), **resident),
                  pl.BlockSpec((t // KT, HD, KT), lambda hk, bq: (0, _R_VS + hk, 0), **resident),
                  pl.BlockSpec((1, nsr, nq * QB), lambda hk, bq: (hk, 0, bq)),
                  pl.BlockSpec((1, N_TOK_BIAS + 1, LANES, GROUP * LANES), lambda hk, bq: (hk, 0, 0, 0), **resident),
                  pl.BlockSpec(ind.shape, lambda hk, bq: (0, 0), **resident),
                  pl.BlockSpec((nq * QB, GROUP * HD), lambda hk, bq: (bq, hk)),
                  pl.BlockSpec((1, 16, nq * QB), lambda hk, bq: (bq // per, _R_G * 8 + hk, bq % per))],
        out_specs=pl.BlockSpec((nq * QB, GROUP * HD), lambda hk, bq: (bq, hk)),
        out_shape=jax.ShapeDtypeStruct((t, N_HEADS * HD), bf16),
        scratch_shapes=[pltpu.VMEM((nsr // SEL_CHUNK, 2 * HD, lanes), bf16),
                        pltpu.VMEM((KT, lanes), f32),
                        pltpu.VMEM((KT, lanes), f32),
                        pltpu.VMEM((KT, lanes), bf16),
                        pltpu.VMEM((KT, lanes), bf16),
                        pltpu.VMEM((HD + 8, lanes), f32),
                        pltpu.VMEM((8, lanes), f32)],
        compiler_params=_params(("parallel", "arbitrary")),
        name="slc_attention",
    )(projt, projn, projt, nsel_t, bias_t, ind, o1, projt)


def _layer_norm(z, g, b):
    mu = jnp.mean(z, axis=-1, keepdims=True)
    var = jnp.mean(jnp.square(z - mu), axis=-1, keepdims=True)
    return (z - mu) * lax.rsqrt(var + LN_EPS) * g + b


def _post_kernel(o2_ref, ya_ref, mb_ref, x_ref, p_ref, wao_ref, wmix_ref, wple_ref, wpg_ref,
                 g_ref, b_ref, x1_ref, r2_ref, *, alpha):
    yb = _dot(o2_ref[...], wao_ref[...])
    mixed = ya_ref[...] + jax.nn.sigmoid(mb_ref[...].astype(f32)) * yb
    z = alpha * x_ref[...] + _dot(mixed.astype(bf16), wmix_ref[...])
    x1 = _layer_norm(z, g_ref[...], b_ref[...])
    x1b = x1.astype(bf16)
    x1_ref[...] = x1b
    ple = _dot(p_ref[...].astype(bf16), wple_ref[...]) * jax.nn.sigmoid(_dot(x1b, wpg_ref[...]))
    r2_ref[...] = alpha * x1 + ple


def _post_attention(o2, ya, mab, x2d, p2d, wao, wmix, wple, wpg, g1, b1, alpha, tm):
    t = x2d.shape[0]
    row = lambda w: pl.BlockSpec((tm, w), lambda i: (i, 0))
    full = lambda a: pl.BlockSpec(a.shape, lambda i: (0, 0), pipeline_mode=pl.Buffered(1))
    return pl.pallas_call(
        functools.partial(_post_kernel, alpha=alpha),
        grid=(t // tm,),
        in_specs=[row(D_MODEL), row(D_MODEL),
                  pl.BlockSpec((tm, D_MODEL), lambda i: (i, 1)),
                  row(D_MODEL), row(PLE_DIM),
                  full(wao), full(wmix), full(wple), full(wpg), full(g1), full(b1)],
        out_specs=[row(D_MODEL), row(D_MODEL)],
        out_shape=[jax.ShapeDtypeStruct((t, D_MODEL), bf16),
                   jax.ShapeDtypeStruct((t, D_MODEL), f32)],
        compiler_params=_params(("parallel",)),
        name="post_attention",
    )(o2, ya, mab, x2d, p2d, wao, wmix, wple, wpg, g1, b1)


def _mlp_kernel(x1_ref, r2_ref, wu_ref, wd_ref, g_ref, b_ref, o_ref):
    k = pl.program_id(1)

    @pl.when(k == 0)
    def _():
        o_ref[...] = r2_ref[...]

    h = jnp.square(jnp.maximum(_dot(x1_ref[...], wu_ref[...]), 0.0))
    o_ref[...] += _dot(h.astype(bf16), wd_ref[...])

    @pl.when(k == pl.num_programs(1) - 1)
    def _():
        o_ref[...] = _layer_norm(o_ref[...], g_ref[...], b_ref[...])


def _mlp(x1b, r2, wu, wd, g2, b2, tm, fc):
    t = x1b.shape[0]
    return pl.pallas_call(
        _mlp_kernel,
        grid=(t // tm, D_FF // fc),
        in_specs=[pl.BlockSpec((tm, D_MODEL), lambda i, k: (i, 0)),
                  pl.BlockSpec((tm, D_MODEL), lambda i, k: (i, 0)),
                  pl.BlockSpec((D_MODEL, fc), lambda i, k: (0, k)),
                  pl.BlockSpec((fc, D_MODEL), lambda i, k: (k, 0)),
                  pl.BlockSpec((1, D_MODEL), lambda i, k: (0, 0)),
                  pl.BlockSpec((1, D_MODEL), lambda i, k: (0, 0))],
        out_specs=pl.BlockSpec((tm, D_MODEL), lambda i, k: (i, 0)),
        out_shape=jax.ShapeDtypeStruct((t, D_MODEL), f32),
        compiler_params=_params(("parallel", "arbitrary")),
        name="mlp",
    )(x1b, r2, wu, wd, g2, b2)


def _layer(x2d, p2d, w_in, conv_w, pe_k, w1_k, w2_k, pe_v, w1_v, w2_v, w_conv_out, w_attn_out,
           w_mix_out, ln1_g, ln1_b, w_mlp_up, w_mlp_down, w_ple, w_ple_gate, ln2_g, ln2_b,
           rel_bias, alpha):
    t = x2d.shape[0]
    assert t % 1024 == 0 and x2d.shape[1] == D_MODEL
    kvw = N_KV * HD
    o_q = 3 * D_MODEL
    o_kc = o_q + N_HEADS * HD
    o_ng = o_kc + 6 * kvw
    o_ma = o_ng + N_HEADS * 3

    n_in = w_in.shape[1]
    cast_rows = max(r for r in range(16, D_MODEL + 1, 16) if n_in % r == 0)
    w_bf = _to_bf16(w_in.T, cast_rows, "cast_w_in")

    def cols(a, n):
        return w_bf[a:a + n]

    tn = kvw
    conv_tiles = 3 * D_MODEL // tn
    ks_tile, kw_tile = (o_kc + 2 * kvw) // tn, (o_kc + 4 * kvw) // tn
    assert _C_KS * LANES == conv_tiles * tn and _C_KW * LANES == (conv_tiles + 1) * tn

    def nat_tile(j):
        return jnp.where(j < conv_tiles, j, jnp.where(j == conv_tiles, ks_tile, kw_tile))

    wng = jnp.pad(cols(o_ng, N_HEADS * 3).reshape(N_KV, GROUP * 3, D_MODEL), ((0, 0), (0, 16 - GROUP * 3), (0, 0)))
    w_t = jnp.concatenate(
        [cols(o_q, N_HEADS * HD), cols(o_kc + 3 * kvw, kvw), cols(o_kc + 5 * kvw, kvw),
         wng.reshape(N_KV * 16, D_MODEL),
         jnp.zeros((_N_T - _R_G * LANES - N_KV * 16, D_MODEL), bf16)], axis=0)
    row_scale = jnp.concatenate([jnp.full((N_HEADS * HD, 1), HD ** -0.5 * LOG2E, f32),
                                 jnp.ones((_N_T - N_HEADS * HD, 1), f32)], axis=0)

    kvc, xb = _proj_chunked(x2d, w_bf, 1024, 2 * kvw, o_kc // (2 * kvw))
    projn = _proj_nat(xb, w_bf, 2048, tn, "proj_nat", _N_NAT, nat_tile)
    mab = _proj_nat(xb, cols(o_ma, 2 * D_MODEL), 2048, 1024, "proj_gate")
    projt = _proj_t(w_t, xb, row_scale, _N_T, KT)

    ya = _mixer_a(projn, mab, conv_w, w_conv_out.astype(bf16), 512)

    half = CMP_STRIDE * HD
    w1ab = jnp.stack([jnp.concatenate([w[:half], w[half:]], axis=1) for w in (w1_k, w1_v)]).astype(bf16)
    pe2 = jnp.stack([jnp.stack([pe[:CMP_STRIDE], pe[CMP_STRIDE:]], axis=1) for pe in (pe_k, pe_v)])
    pe2 = jnp.pad(pe2, ((0, 0), (0, 0), (0, 6), (0, 0)))
    w2 = jnp.stack([w2_k, w2_v]).astype(bf16)
    cn, ct = _compress(kvc, pe2, w1ab, w2, jnp.swapaxes(w2, 1, 2))

    bias_t = _bias_tiles(rel_bias, N_TOK_BIAS, 1, 0, WINDOW // QB, WINDOW - 1, "bias_tok")
    bias_c = _bias_tiles(rel_bias, N_CMP_BIAS, CMP_STRIDE, CMP_LEN - 1, 0, -1, "bias_cmp")

    ncp = t // CMP_STRIDE
    nslc = t // SLC_LEN
    nsr = -(-nslc // SEL_CHUNK) * SEL_CHUNK
    ci = jnp.arange(ncp)[None, :] * CMP_STRIDE
    sj = jnp.arange(nslc)[:, None] * SLC_LEN
    ov_t = ((ci < sj + SLC_LEN) & (ci + CMP_LEN > sj)).astype(bf16)
    pos = jnp.arange(min(t, SEL_CHUNK * SLC_LEN))[:, None] // SLC_LEN
    ind = jnp.where(pos == jnp.arange(SEL_CHUNK)[None, :], SEL_BIG, 0.0).astype(bf16)

    o1, nsel_t = _cmp_attention(projn, projt, cn, ct, bias_c, bias_t, ov_t, t, nsr, CMP_NQ)
    o2 = _slc_attention(projn, projt, nsel_t, bias_t, ind, o1, t, SLC_NQ)

    x1b, r2 = _post_attention(o2, ya, mab, x2d, p2d, w_attn_out.astype(bf16), w_mix_out.astype(bf16),
                              w_ple.astype(bf16), w_ple_gate.astype(bf16),
                              ln1_g[None, :], ln1_b[None, :], alpha, 256)
    return _mlp(x1b, r2, w_mlp_up.astype(bf16), w_mlp_down.astype(bf16), ln2_g[None, :], ln2_b[None, :], 1024, 512)


def kernel(x, p, w_in, conv_w, cmp_pe_k, cmp_w1_k, cmp_w2_k, cmp_pe_v, cmp_w1_v, cmp_w2_v, w_conv_out, w_attn_out, w_mix_out, ln1_g, ln1_b, w_mlp_up, w_mlp_down, w_ple, w_ple_gate, ln2_g, ln2_b, rel_bias):
    bsz, t, d = x.shape
    assert bsz == 1
    depth = w_in.shape[0]
    alpha = (2 * depth) ** 0.25
    x2d = x[0]
    for i in range(depth):
        x2d = _layer(x2d, p[i, 0], w_in[i], conv_w[i], cmp_pe_k[i], cmp_w1_k[i], cmp_w2_k[i],
                     cmp_pe_v[i], cmp_w1_v[i], cmp_w2_v[i], w_conv_out[i], w_attn_out[i], w_mix_out[i],
                     ln1_g[i], ln1_b[i], w_mlp_up[i], w_mlp_down[i], w_ple[i], w_ple_gate[i],
                     ln2_g[i], ln2_b[i], rel_bias, alpha)
    return x2d[None]
```
